```python
import math
import jax
import jax.numpy as jnp
from jax import lax
import numpy as np

D_MODEL = 2048
BATCH = 4
SEQ = 2048
DEPTH = 2

D_FF = 5632
CONV_CH = D_MODEL // 4
CONV_WIDTH = 31
MOBA_HEADS = 6
MOBA_HEAD_DIM = 128
MOBA_BLOCK = 256
MOBA_TOPK = 3
MOBA_Q_CHUNK = 32
DIFF_HEADS = 6
DIFF_QK_DIM = 64
DIFF_V_DIM = 128
DIFF_Q_BLOCK = 128
MEM_LEN = 256
MEM_HEADS = 4
MEM_HEAD_DIM = 128
N_BUCKETS = 32
MAX_DISTANCE = 128
N_BIAS_HEADS = MOBA_HEADS + DIFF_HEADS
MIX_WIDTH = CONV_CH + MOBA_HEADS * MOBA_HEAD_DIM + DIFF_HEADS * DIFF_V_DIM
PROJ_SPLITS = (CONV_CH, CONV_CH, MOBA_HEADS * MOBA_HEAD_DIM, MOBA_HEADS * MOBA_HEAD_DIM, MOBA_HEADS * MOBA_HEAD_DIM, DIFF_HEADS * 2 * DIFF_QK_DIM, DIFF_HEADS * 2 * DIFF_QK_DIM, DIFF_HEADS * DIFF_V_DIM)
PROJ_WIDTH = sum(PROJ_SPLITS)
NEG_INF = -1e30

kernel_name = "hybrid_conv_moba_diffattn_macaron"


def rms_norm(x, g, eps=1e-6):
    xf = x.astype(jnp.float32)
    y = xf * lax.rsqrt(jnp.mean(xf * xf, axis=-1, keepdims=True) + eps)
    return (y * g.astype(jnp.float32)).astype(x.dtype)


def layer_norm(x, g, b, eps=1e-5):
    xf = x.astype(jnp.float32)
    mu = jnp.mean(xf, axis=-1, keepdims=True)
    var = jnp.mean(jnp.square(xf - mu), axis=-1, keepdims=True)
    y = (xf - mu) * lax.rsqrt(var + eps)
    return (y * g.astype(jnp.float32) + b.astype(jnp.float32)).astype(x.dtype)


def swiglu(h, w_gate, w_up, w_down):
    return (jax.nn.silu(h @ w_gate) * (h @ w_up)) @ w_down


def t5_bucket(dist):
    dist = jnp.maximum(dist, 0)
    max_exact = N_BUCKETS // 2
    log_ratio = jnp.log(jnp.maximum(dist, 1).astype(jnp.float32) / max_exact) / math.log(MAX_DISTANCE / max_exact)
    large = jnp.minimum(max_exact + (log_ratio * (N_BUCKETS - max_exact)).astype(jnp.int32), N_BUCKETS - 1)
    return jnp.where(dist < max_exact, dist, large)


def conv_module(a, g, conv_w, conv_b, ln_g, ln_b):
    u = a * jax.nn.sigmoid(g)
    u = lax.conv_general_dilated(u, conv_w[:, None, :].astype(u.dtype), window_strides=(1,), padding=[(CONV_WIDTH - 1, 0)], dimension_numbers=('NWC', 'WIO', 'NWC'), feature_group_count=CONV_CH) + conv_b
    return jax.nn.silu(layer_norm(u, ln_g, ln_b))


def moba_attention(q, k, v, bias_tab):
    B, H, S, Dh = q.shape
    nb = -(-S // MOBA_BLOCK)
    pad = nb * MOBA_BLOCK - S
    kp = jnp.pad(k, ((0, 0), (0, 0), (0, pad), (0, 0)))
    vp = jnp.pad(v, ((0, 0), (0, 0), (0, pad), (0, 0)))
    kb = kp.reshape(B, H, nb, MOBA_BLOCK, Dh)
    vb = vp.reshape(B, H, nb, MOBA_BLOCK, Dh)
    counts = jnp.clip(S - jnp.arange(nb) * MOBA_BLOCK, 1, MOBA_BLOCK).astype(jnp.float32)
    kmean = jnp.sum(kb.astype(jnp.float32), axis=3) / counts[:, None]
    n_sel = min(MOBA_TOPK, nb)
    scale = Dh ** -0.5
    b_ix = jnp.arange(B)[:, None, None, None]
    h_ix = jnp.arange(H)[None, :, None, None]
    h_ix5 = jnp.arange(H)[None, :, None, None, None]
    offs = jnp.arange(MOBA_BLOCK)

    def chunk(start):
        qc = lax.dynamic_slice_in_dim(q, start, MOBA_Q_CHUNK, axis=2)
        qpos = start + jnp.arange(MOBA_Q_CHUNK)
        cur = start // MOBA_BLOCK
        gate = jnp.einsum('bhqd,bhnd->bhqn', qc.astype(jnp.float32), kmean)
        gate = jnp.where(jnp.arange(nb) < cur, gate, NEG_INF)
        _, idx = lax.top_k(gate, n_sel)
        valid = idx < cur
        k_sel = kb[b_ix, h_ix, idx]
        v_sel = vb[b_ix, h_ix, idx]
        kpos_sel = idx[..., None] * MOBA_BLOCK + offs
        bias_sel = bias_tab[h_ix5, t5_bucket(qpos[:, None, None] - kpos_sel)]
        s_sel = jnp.einsum('bhqd,bhqnjd->bhqnj', qc, k_sel).astype(jnp.float32) * scale + bias_sel
        s_sel = jnp.where(valid[..., None], s_sel, NEG_INF)
        k_own = lax.dynamic_slice_in_dim(kp, cur * MOBA_BLOCK, MOBA_BLOCK, axis=2)
        v_own = lax.dynamic_slice_in_dim(vp, cur * MOBA_BLOCK, MOBA_BLOCK, axis=2)
        kpos_own = cur * MOBA_BLOCK + offs
        dist_own = qpos[:, None] - kpos_own[None, :]
        s_own = jnp.einsum('bhqd,bhjd->bhqj', qc, k_own).astype(jnp.float32) * scale + bias_tab[:, t5_bucket(dist_own)][None]
        s_own = jnp.where(dist_own >= 0, s_own, NEG_INF)
        logits = jnp.concatenate([s_sel.reshape(B, H, MOBA_Q_CHUNK, n_sel * MOBA_BLOCK), s_own], axis=-1)
        p = jax.nn.softmax(logits, axis=-1).astype(v.dtype)
        p_sel = p[..., :n_sel * MOBA_BLOCK].reshape(B, H, MOBA_Q_CHUNK, n_sel, MOBA_BLOCK)
        p_own = p[..., n_sel * MOBA_BLOCK:]
        return jnp.einsum('bhqnj,bhqnjd->bhqd', p_sel, v_sel) + jnp.einsum('bhqj,bhjd->bhqd', p_own, v_own)

    out = lax.map(chunk, jnp.arange(0, S, MOBA_Q_CHUNK))
    return out.transpose(1, 2, 0, 3, 4).reshape(B, H, S, Dh)


def diff_attention(q, k, v, lam, lam_init, subln_g, bias_tab):
    B, H, _, S, dqk = q.shape
    scale = dqk ** -0.5
    kpos = jnp.arange(S)

    def block(start):
        qc = lax.dynamic_slice_in_dim(q, start, DIFF_Q_BLOCK, axis=3)
        qpos = start + jnp.arange(DIFF_Q_BLOCK)
        dist = qpos[:, None] - kpos[None, :]
        bias = bias_tab[:, t5_bucket(dist)]
        s = jnp.einsum('bhcqd,bhckd->bhcqk', qc, k).astype(jnp.float32) * scale + bias[None, :, None]
        s = jnp.where(dist >= 0, s, NEG_INF)
        p = jax.nn.softmax(s, axis=-1)
        a = p[:, :, 0] - lam * p[:, :, 1]
        return jnp.einsum('bhqk,bhkd->bhqd', a.astype(v.dtype), v)

    o = lax.map(block, jnp.arange(0, S, DIFF_Q_BLOCK))
    o = o.transpose(1, 2, 0, 3, 4).reshape(B, H, S, v.shape[-1])
    return rms_norm(o, subln_g) * (1.0 - lam_init)


def memory_attention(h, m, wq, wkv, q_norm, k_norm, wo):
    B, S, _ = h.shape
    M = m.shape[1]
    q = rms_norm((h @ wq).reshape(B, S, MEM_HEADS, MEM_HEAD_DIM), q_norm)
    kv = (m @ wkv).reshape(B, M, 2, MEM_HEADS, MEM_HEAD_DIM)
    k = rms_norm(kv[:, :, 0], k_norm)
    v = kv[:, :, 1]
    s = jnp.einsum('bshd,bmhd->bhsm', q, k).astype(jnp.float32) * (MEM_HEAD_DIM ** -0.5)
    p = jax.nn.softmax(s, axis=-1).astype(v.dtype)
    o = jnp.einsum('bhsm,bmhd->bshd', p, v).reshape(B, S, MEM_HEADS * MEM_HEAD_DIM)
    return o @ wo


def setup_inputs(seed: int = 0) -> dict:
    key = jax.random.key(seed)
    ks = jax.random.split(key, 32)
    L = DEPTH
    nrm = lambda k, shape, s: s * jax.random.normal(k, shape, jnp.float32)
    gain = lambda k, shape: 1.0 + 0.05 * jax.random.normal(k, shape, jnp.float32)
    return {
        "x": nrm(ks[0], (BATCH, SEQ, D_MODEL), 1.0),
        "mem": nrm(ks[1], (BATCH, MEM_LEN, D_MODEL), 1.0),
        "rel_bias": nrm(ks[2], (N_BUCKETS, N_BIAS_HEADS), 0.5),
        "ffn1_norm": gain(ks[3], (L, D_MODEL)),
        "ffn1_w_gate": nrm(ks[4], (L, D_MODEL, D_FF), D_MODEL ** -0.5),
        "ffn1_w_up": nrm(ks[5], (L, D_MODEL, D_FF), D_MODEL ** -0.5),
        "ffn1_w_down": nrm(ks[6], (L, D_FF, D_MODEL), D_FF ** -0.5),
        "mix_norm": gain(ks[7], (L, D_MODEL)),
        "w_in": nrm(ks[8], (L, D_MODEL, PROJ_WIDTH), D_MODEL ** -0.5),
        "conv_w": nrm(ks[9], (L, CONV_WIDTH, CONV_CH), CONV_WIDTH ** -0.5),
        "conv_b": nrm(ks[10], (L, CONV_CH), 0.02),
        "conv_ln_g": gain(ks[11], (L, CONV_CH)),
        "conv_ln_b": nrm(ks[12], (L, CONV_CH), 0.02),
        "moba_q_norm": gain(ks[13], (L, MOBA_HEAD_DIM)),
        "moba_k_norm": gain(ks[14], (L, MOBA_HEAD_DIM)),
        "diff_q_norm": gain(ks[15], (L, DIFF_QK_DIM)),
        "diff_k_norm": gain(ks[16], (L, DIFF_QK_DIM)),
        "diff_lambda": nrm(ks[17], (L, 4, DIFF_QK_DIM), 0.1),
        "diff_subln": gain(ks[18], (L, DIFF_V_DIM)),
        "w_out": nrm(ks[19], (L, MIX_WIDTH, D_MODEL), MIX_WIDTH ** -0.5),
        "mem_norm_x": gain(ks[20], (L, D_MODEL)),
        "mem_norm_m": gain(ks[21], (L, D_MODEL)),
        "mem_wq": nrm(ks[22], (L, D_MODEL, MEM_HEADS * MEM_HEAD_DIM), D_MODEL ** -0.5),
        "mem_wkv": nrm(ks[23], (L, D_MODEL, 2 * MEM_HEADS * MEM_HEAD_DIM), D_MODEL ** -0.5),
        "mem_q_norm": gain(ks[24], (L, MEM_HEAD_DIM)),
        "mem_k_norm": gain(ks[25], (L, MEM_HEAD_DIM)),
        "mem_wo": nrm(ks[26], (L, MEM_HEADS * MEM_HEAD_DIM, D_MODEL), (MEM_HEADS * MEM_HEAD_DIM) ** -0.5),
        "ffn2_norm": gain(ks[27], (L, D_MODEL)),
        "ffn2_w_gate": nrm(ks[28], (L, D_MODEL, D_FF), D_MODEL ** -0.5),
        "ffn2_w_up": nrm(ks[29], (L, D_MODEL, D_FF), D_MODEL ** -0.5),
        "ffn2_w_down": nrm(ks[30], (L, D_FF, D_MODEL), D_FF ** -0.5),
    }


def reference(x, mem, rel_bias, ffn1_norm, ffn1_w_gate, ffn1_w_up, ffn1_w_down, mix_norm, w_in, conv_w, conv_b, conv_ln_g, conv_ln_b, moba_q_norm, moba_k_norm, diff_q_norm, diff_k_norm, diff_lambda, diff_subln, w_out, mem_norm_x, mem_norm_m, mem_wq, mem_wkv, mem_q_norm, mem_k_norm, mem_wo, ffn2_norm, ffn2_w_gate, ffn2_w_up, ffn2_w_down):
    B, S, _ = x.shape
    split_points = np.cumsum(PROJ_SPLITS)[:-1].tolist()
    bias_t = rel_bias.T
    moba_bias = bias_t[:MOBA_HEADS]
    diff_bias = bias_t[MOBA_HEADS:]
    for l in range(DEPTH):
        x = x + 0.5 * swiglu(rms_norm(x, ffn1_norm[l]), ffn1_w_gate[l], ffn1_w_up[l], ffn1_w_down[l])
        h = rms_norm(x, mix_norm[l])
        proj = h @ w_in[l]
        ca, cg, mq, mk, mv, dq, dk, dv = jnp.split(proj, split_points, axis=-1)
        y_conv = conv_module(ca, cg, conv_w[l], conv_b[l], conv_ln_g[l], conv_ln_b[l])
        mq = rms_norm(mq.reshape(B, S, MOBA_HEADS, MOBA_HEAD_DIM), moba_q_norm[l]).transpose(0, 2, 1, 3)
        mk = rms_norm(mk.reshape(B, S, MOBA_HEADS, MOBA_HEAD_DIM), moba_k_norm[l]).transpose(0, 2, 1, 3)
        mv = mv.reshape(B, S, MOBA_HEADS, MOBA_HEAD_DIM).transpose(0, 2, 1, 3)
        y_moba = moba_attention(mq, mk, mv, moba_bias).transpose(0, 2, 1, 3).reshape(B, S, MOBA_HEADS * MOBA_HEAD_DIM)
        dq = rms_norm(dq.reshape(B, S, DIFF_HEADS, 2, DIFF_QK_DIM), diff_q_norm[l]).transpose(0, 2, 3, 1, 4)
        dk = rms_norm(dk.reshape(B, S, DIFF_HEADS, 2, DIFF_QK_DIM), diff_k_norm[l]).transpose(0, 2, 3, 1, 4)
        dv = dv.reshape(B, S, DIFF_HEADS, DIFF_V_DIM).transpose(0, 2, 1, 3)
        lam_init = 0.8 - 0.6 * math.exp(-0.3 * l)
        lp = diff_lambda[l].astype(jnp.float32)
        lam = jnp.exp(jnp.sum(lp[0] * lp[1])) - jnp.exp(jnp.sum(lp[2] * lp[3])) + lam_init
        y_diff = diff_attention(dq, dk, dv, lam, lam_init, diff_subln[l], diff_bias)
        y_diff = y_diff.transpose(0, 2, 1, 3).reshape(B, S, DIFF_HEADS * DIFF_V_DIM)
        x = x + jnp.concatenate([y_conv, y_moba, y_diff], axis=-1) @ w_out[l]
        x = x + memory_attention(rms_norm(x, mem_norm_x[l]), rms_norm(mem, mem_norm_m[l]), mem_wq[l], mem_wkv[l], mem_q_norm[l], mem_k_norm[l], mem_wo[l])
        x = x + 0.5 * swiglu(rms_norm(x, ffn2_norm[l]), ffn2_w_gate[l], ffn2_w_up[l], ffn2_w_down[l])
    return x
```

```python
import functools
import math

import jax
import jax.numpy as jnp
from jax import lax
from jax.experimental import pallas as pl
from jax.experimental.pallas import tpu as pltpu

F32 = jnp.float32
BF16 = jnp.bfloat16

D_MODEL = 2048
D_FF = 5632
CONV_CH = 512
CONV_WIDTH = 31
CONV_PAD = 32
HEAD_DIM = 128
MOBA_HEADS = 6
MOBA_BLOCK = 256
MOBA_TOPK = 3
DIFF_HEADS = 6
DIFF_QK_DIM = 64
MEM_HEADS = 4
N_BUCKETS = 32
MAX_DISTANCE = 128
N_BIAS_HEADS = MOBA_HEADS + DIFF_HEADS
NEG_INF = -1e30
TAKEN = -3e38
PROJ_WIDTH = 5632
COL_CONV_A = 0
COL_CONV_G = CONV_CH // HEAD_DIM
COL_MOBA_Q = 2 * CONV_CH // HEAD_DIM
COL_MOBA_K = COL_MOBA_Q + MOBA_HEADS
COL_MOBA_V = COL_MOBA_K + MOBA_HEADS
COL_DIFF_Q = COL_MOBA_V + MOBA_HEADS
COL_DIFF_K = COL_DIFF_Q + DIFF_HEADS
COL_DIFF_V = COL_DIFF_K + DIFF_HEADS

ATT_TILE = 256
VMEM_LIMIT = 56 * 1024 * 1024


def _params(semantics, vmem=VMEM_LIMIT):
    return pltpu.CompilerParams(dimension_semantics=semantics, vmem_limit_bytes=vmem)


def _rms(x, gain, eps=1e-6):
    return x * lax.rsqrt(jnp.mean(x * x, axis=-1, keepdims=True) + eps) * gain


def _dot(a, b):
    return jnp.dot(a, b, preferred_element_type=F32)


def _dot_nt(a, b):
    return lax.dot_general(a, b, (((1,), (1,)), ((), ())), preferred_element_type=F32)


def _ffn_kernel(x_ref, g_ref, wg_ref, wu_ref, wd_ref, o_ref, h_ref):
    k = pl.program_id(1)

    @pl.when(k == 0)
    def _():
        x = x_ref[...]
        h_ref[...] = _rms(x, g_ref[...]).astype(BF16)
        o_ref[...] = x

    h = h_ref[...]
    gate = _dot(h, wg_ref[...])
    up = _dot(h, wu_ref[...])
    act = (0.5 * gate / (1.0 + jnp.exp(-gate)) * up).astype(BF16)
    o_ref[...] += _dot(act, wd_ref[...])


def _ffn(x, gain, wg, wu, wd, tm=512, tf=512):
    t, d = x.shape
    f = wg.shape[1]
    return pl.pallas_call(
        _ffn_kernel,
        out_shape=jax.ShapeDtypeStruct((t, d), F32),
        grid=(t // tm, f // tf),
        in_specs=[
            pl.BlockSpec((tm, d), lambda i, k: (i, 0)),
            pl.BlockSpec((1, d), lambda i, k: (0, 0)),
            pl.BlockSpec((d, tf), lambda i, k: (0, k)),
            pl.BlockSpec((d, tf), lambda i, k: (0, k)),
            pl.BlockSpec((tf, d), lambda i, k: (k, 0)),
        ],
        out_specs=pl.BlockSpec((tm, d), lambda i, k: (i, 0)),
        scratch_shapes=[pltpu.VMEM((tm, d), BF16)],
        compiler_params=_params(("parallel", "arbitrary")),
        name="ffn",
    )(x, gain.reshape(1, d), wg, wu, wd)


def _proj_kernel(x_ref, g_ref, w_ref, o_ref, h_ref):
    @pl.when(pl.program_id(1) == 0)
    def _():
        h_ref[...] = _rms(x_ref[...], g_ref[...]).astype(BF16)

    o_ref[...] = _dot(h_ref[...], w_ref[...])


def _proj(x, gain, w, tm=1024, tn=512):
    t, d = x.shape
    n = w.shape[1]
    return pl.pallas_call(
        _proj_kernel,
        out_shape=jax.ShapeDtypeStruct((t, n), F32),
        grid=(t // tm, n // tn),
        in_specs=[
            pl.BlockSpec((tm, d), lambda i, j: (i, 0)),
            pl.BlockSpec((1, d), lambda i, j: (0, 0)),
            pl.BlockSpec((d, tn), lambda i, j: (0, j)),
        ],
        out_specs=pl.BlockSpec((tm, tn), lambda i, j: (i, j)),
        scratch_shapes=[pltpu.VMEM((tm, d), BF16)],
        compiler_params=_params(("parallel", "arbitrary")),
        name="proj",
    )(x, gain.reshape(1, d), w)


def _bias_kernel(tab_ref, o_ref):
    h = pl.program_id(0)
    shape = (ATT_TILE, 2 * ATT_TILE)
    dist = ATT_TILE + lax.broadcasted_iota(jnp.int32, shape, 0) - lax.broadcasted_iota(jnp.int32, shape, 1)
    dist = jnp.maximum(dist, 0)
    max_exact = N_BUCKETS // 2
    log_ratio = jnp.log(jnp.maximum(dist, 1).astype(F32) / max_exact) / math.log(MAX_DISTANCE / max_exact)
    large = jnp.minimum(max_exact + (log_ratio * (N_BUCKETS - max_exact)).astype(jnp.int32), N_BUCKETS - 1)
    bucket = jnp.where(dist < max_exact, dist, large)
    out = jnp.zeros(shape, F32)
    for b in range(N_BUCKETS):
        out = jnp.where(bucket == b, tab_ref[h * N_BUCKETS + b], out)
    o_ref[0] = out


def _bias_strips(tab_flat):
    return pl.pallas_call(
        _bias_kernel,
        out_shape=jax.ShapeDtypeStruct((N_BIAS_HEADS, ATT_TILE, 2 * ATT_TILE), F32),
        grid=(N_BIAS_HEADS,),
        in_specs=[pl.BlockSpec(memory_space=pltpu.SMEM)],
        out_specs=pl.BlockSpec((1, ATT_TILE, 2 * ATT_TILE), lambda h: (h, 0, 0)),
        compiler_params=_params(("arbitrary",)),
        name="bias_strips",
    )(tab_flat)


CONV_ROWS = 64


def _conv_kernel(a_ref, g_ref, w_ref, cb_ref, lg_ref, lb_ref, o_ref, u_ref):
    s = a_ref.shape[1]
    u_ref[0:CONV_PAD, :] = jnp.zeros((CONV_PAD, CONV_CH), F32)

    def glu(r, carry):
        rows = pl.ds(pl.multiple_of(r * CONV_ROWS, CONV_ROWS), CONV_ROWS)
        a = a_ref[0, rows, :]
        g = g_ref[0, rows, :]
        u_ref[pl.ds(pl.multiple_of(CONV_PAD + r * CONV_ROWS, CONV_ROWS // 2), CONV_ROWS), :] = a / (1.0 + jnp.exp(-g))
        return carry

    lax.fori_loop(0, s // CONV_ROWS, glu, 0)

    def conv(r, carry):
        base = pl.multiple_of(r * CONV_ROWS, CONV_ROWS)
        win = u_ref[pl.ds(base, CONV_ROWS + CONV_PAD), :]
        acc = jnp.zeros((CONV_ROWS, CONV_CH), F32) + cb_ref[...]
        for j in range(CONV_WIDTH):
            off = CONV_PAD - (CONV_WIDTH - 1) + j
            acc = acc + win[off:off + CONV_ROWS, :] * w_ref[j:j + 1, :]
        mu = jnp.mean(acc, axis=-1, keepdims=True)
        cen = acc - mu
        var = jnp.mean(cen * cen, axis=-1, keepdims=True)
        y = cen * lax.rsqrt(var + 1e-5) * lg_ref[...] + lb_ref[...]
        o_ref[0, pl.ds(base, CONV_ROWS), :] = (y / (1.0 + jnp.exp(-y))).astype(o_ref.dtype)
        return carry

    lax.fori_loop(0, s // CONV_ROWS, conv, 0)


def _conv(proj, w, cb, lg, lb):
    b, s, _ = proj.shape
    vec = lambda v: v.reshape(1, CONV_CH)
    return pl.pallas_call(
        _conv_kernel,
        out_shape=jax.ShapeDtypeStruct((b, s, CONV_CH), BF16),
        grid=(b,),
        in_specs=[
            pl.BlockSpec((1, s, CONV_CH), lambda i: (i, 0, 0)),
            pl.BlockSpec((1, s, CONV_CH), lambda i: (i, 0, 1)),
            pl.BlockSpec((CONV_WIDTH, CONV_CH), lambda i: (0, 0)),
            pl.BlockSpec((1, CONV_CH), lambda i: (0, 0)),
            pl.BlockSpec((1, CONV_CH), lambda i: (0, 0)),
            pl.BlockSpec((1, CONV_CH), lambda i: (0, 0)),
        ],
        out_specs=pl.BlockSpec((1, s, CONV_CH), lambda i: (i, 0, 0)),
        scratch_shapes=[pltpu.VMEM((s + CONV_PAD, CONV_CH), F32)],
        compiler_params=_params(("parallel",)),
        name="conv",
    )(proj, proj, w, vec(cb), vec(lg), vec(lb))


def _softmax_step(s, v, m_ref, l_ref, acc_ref):
    m_old = m_ref[...]
    m_new = jnp.maximum(m_old, jnp.max(s, axis=-1, keepdims=True))
    alpha = jnp.exp(m_old - m_new)
    p = jnp.exp(s - m_new)
    l_ref[...] = alpha * l_ref[...] + jnp.sum(p, axis=-1, keepdims=True)
    acc_ref[...] = alpha * acc_ref[...] + _dot(p.astype(BF16), v)
    m_ref[...] = m_new


def _softmax_first(s, v, m_ref, l_ref, acc_ref):
    m = jnp.max(s, axis=-1, keepdims=True)
    p = jnp.exp(s - m)
    m_ref[...] = m
    l_ref[...] = jnp.sum(p, axis=-1, keepdims=True)
    acc_ref[...] = _dot(p.astype(BF16), v)


def _causal_mask():
    shape = (ATT_TILE, ATT_TILE)
    return lax.broadcasted_iota(jnp.int32, shape, 0) >= lax.broadcasted_iota(jnp.int32, shape, 1)


def _moba_kernel(tab_ref, q_ref, k_ref, v_ref, qg_ref, kg_ref, strip_ref, o_ref,
                 kn_ref, vb_ref, kmean_ref, sel_ref, m_ref, l_ref, acc_ref):
    h = pl.program_id(1)
    i = pl.program_id(2)
    nb = k_ref.shape[1] // MOBA_BLOCK
    far_bias = tab_ref[h * N_BUCKETS + N_BUCKETS - 1]

    @pl.when(i == 0)
    def _():
        for j in range(nb):
            rows = slice(j * MOBA_BLOCK, (j + 1) * MOBA_BLOCK)
            kn = _rms(k_ref[0, rows, :], kg_ref[...])
            kn_ref[rows, :] = kn.astype(BF16)
            kmean_ref[j:j + 1, :] = jnp.sum(kn, axis=0, keepdims=True) / float(MOBA_BLOCK)
            vb_ref[rows, :] = v_ref[0, rows, :].astype(BF16)

    qn = _rms(q_ref[0], qg_ref[...])
    gate = lax.dot_general(qn, kmean_ref[...], (((1,), (1,)), ((), ())),
                           precision=lax.Precision.HIGHEST, preferred_element_type=F32)
    blk = lax.broadcasted_iota(jnp.int32, gate.shape, 1)
    gate = jnp.where(blk < i, gate, NEG_INF)
    sel = jnp.zeros(gate.shape, jnp.bool_)
    for _ in range(min(MOBA_TOPK, nb)):
        top = jnp.max(gate, axis=-1, keepdims=True)
        first = jnp.min(jnp.where(gate == top, blk, nb), axis=-1, keepdims=True)
        pick = blk == first
        sel = jnp.logical_or(sel, pick)
        gate = jnp.where(pick, TAKEN, gate)
    sel_add = jnp.where(jnp.logical_and(sel, blk < i), 0.0, NEG_INF)
    for j in range(nb):
        sel_ref[j] = sel_add[:, j:j + 1]

    q = (qn * HEAD_DIM ** -0.5).astype(BF16)
    own = pl.ds(pl.multiple_of(i * MOBA_BLOCK, MOBA_BLOCK), MOBA_BLOCK)
    s_own = _dot_nt(q, kn_ref[own, :]) + strip_ref[0, :, ATT_TILE:]
    s_own = jnp.where(_causal_mask(), s_own, NEG_INF)
    _softmax_first(s_own, vb_ref[own, :], m_ref, l_ref, acc_ref)

    def far(j, carry):
        rows = pl.ds(pl.multiple_of(j * MOBA_BLOCK, MOBA_BLOCK), MOBA_BLOCK)
        s = _dot_nt(q, kn_ref[rows, :]) + (far_bias + sel_ref[j])
        _softmax_step(s, vb_ref[rows, :], m_ref, l_ref, acc_ref)
        return carry

    lax.fori_loop(0, i - 1, far, 0)

    @pl.when(i >= 1)
    def _():
        rows = pl.ds(pl.multiple_of((i - 1) * MOBA_BLOCK, MOBA_BLOCK), MOBA_BLOCK)
        s = _dot_nt(q, kn_ref[rows, :]) + strip_ref[0, :, :ATT_TILE] + sel_ref[i - 1]
        _softmax_step(s, vb_ref[rows, :], m_ref, l_ref, acc_ref)

    o_ref[0] = (acc_ref[...] / l_ref[...]).astype(o_ref.dtype)


def _moba(proj, tab_flat, strips, q_gain, k_gain):
    b, s, _ = proj.shape
    nq = s // MOBA_BLOCK
    nb = s // MOBA_BLOCK
    return pl.pallas_call(
        _moba_kernel,
        out_shape=jax.ShapeDtypeStruct((b, s, MOBA_HEADS * HEAD_DIM), BF16),
        grid=(b, MOBA_HEADS, nq),
        in_specs=[
            pl.BlockSpec(memory_space=pltpu.SMEM),
            pl.BlockSpec((1, MOBA_BLOCK, HEAD_DIM), lambda bi, h, i: (bi, i, COL_MOBA_Q + h)),
            pl.BlockSpec((1, s, HEAD_DIM), lambda bi, h, i: (bi, 0, COL_MOBA_K + h)),
            pl.BlockSpec((1, s, HEAD_DIM), lambda bi, h, i: (bi, 0, COL_MOBA_V + h)),
            pl.BlockSpec((1, HEAD_DIM), lambda bi, h, i: (0, 0)),
            pl.BlockSpec((1, HEAD_DIM), lambda bi, h, i: (0, 0)),
            pl.BlockSpec((1, ATT_TILE, 2 * ATT_TILE), lambda bi, h, i: (h, 0, 0)),
        ],
        out_specs=pl.BlockSpec((1, MOBA_BLOCK, HEAD_DIM), lambda bi, h, i: (bi, i, h)),
        scratch_shapes=[
            pltpu.VMEM((s, HEAD_DIM), BF16),
            pltpu.VMEM((s, HEAD_DIM), BF16),
            pltpu.VMEM((nb, HEAD_DIM), F32),
            pltpu.VMEM((nb, MOBA_BLOCK, 1), F32),
            pltpu.VMEM((MOBA_BLOCK, 1), F32),
            pltpu.VMEM((MOBA_BLOCK, 1), F32),
            pltpu.VMEM((MOBA_BLOCK, HEAD_DIM), F32),
        ],
        compiler_params=_params(("parallel", "parallel", "arbitrary")),
        name="moba",
    )(tab_flat, proj, proj, proj, q_gain.reshape(1, HEAD_DIM), k_gain.reshape(1, HEAD_DIM), strips)


def _diff_kernel(lam_init, tab_ref, q_ref, k_ref, v_ref, qg_ref, kg_ref, lam_ref, sg_ref, strip_ref, o_ref,
                 k0_ref, k1_ref, vb_ref, m0_ref, l0_ref, a0_ref, m1_ref, l1_ref, a1_ref):
    h = pl.program_id(1)
    i = pl.program_id(2)
    nt = k_ref.shape[1] // ATT_TILE
    far_bias = tab_ref[(MOBA_HEADS + h) * N_BUCKETS + N_BUCKETS - 1]
    dq = DIFF_QK_DIM

    @pl.when(i == 0)
    def _():
        for j in range(nt):
            rows = slice(j * ATT_TILE, (j + 1) * ATT_TILE)
            k = k_ref[0, rows, :]
            k0_ref[rows, :] = _rms(k[:, :dq], kg_ref[...]).astype(BF16)
            k1_ref[rows, :] = _rms(k[:, dq:], kg_ref[...]).astype(BF16)
            vb_ref[rows, :] = v_ref[0, rows, :].astype(BF16)

    q = q_ref[0]
    scale = dq ** -0.5
    qs = ((_rms(q[:, :dq], qg_ref[...]) * scale).astype(BF16), (_rms(q[:, dq:], qg_ref[...]) * scale).astype(BF16))
    ks = (k0_ref, k1_ref)
    state = ((m0_ref, l0_ref, a0_ref), (m1_ref, l1_ref, a1_ref))

    own = pl.ds(pl.multiple_of(i * ATT_TILE, ATT_TILE), ATT_TILE)
    mask = _causal_mask()
    for c in range(2):
        s = _dot_nt(qs[c], ks[c][own, :]) + strip_ref[0, :, ATT_TILE:]
        s = jnp.where(mask, s, NEG_INF)
        _softmax_first(s, vb_ref[own, :], *state[c])

    def far(j, carry):
        rows = pl.ds(pl.multiple_of(j * ATT_TILE, ATT_TILE), ATT_TILE)
        for c in range(2):
            s = _dot_nt(qs[c], ks[c][rows, :]) + far_bias
            _softmax_step(s, vb_ref[rows, :], *state[c])
        return carry

    lax.fori_loop(0, i - 1, far, 0)

    @pl.when(i >= 1)
    def _():
        rows = pl.ds(pl.multiple_of((i - 1) * ATT_TILE, ATT_TILE), ATT_TILE)
        for c in range(2):
            s = _dot_nt(qs[c], ks[c][rows, :]) + strip_ref[0, :, :ATT_TILE]
            _softmax_step(s, vb_ref[rows, :], *state[c])

    lp = lam_ref[...]
    lam = (jnp.exp(jnp.sum(lp[0:1] * lp[1:2], axis=-1, keepdims=True))
           - jnp.exp(jnp.sum(lp[2:3] * lp[3:4], axis=-1, keepdims=True)) + lam_init)
    o = a0_ref[...] / l0_ref[...] - lam * (a1_ref[...] / l1_ref[...])
    o_ref[0] = (_rms(o, sg_ref[...]) * (1.0 - lam_init)).astype(o_ref.dtype)


def _diff(proj, tab_flat, strips, q_gain, k_gain, lam_params, subln, lam_init):
    b, s, _ = proj.shape
    nq = s // ATT_TILE
    col = lambda v: (1, v.shape[-1])
    return pl.pallas_call(
        functools.partial(_diff_kernel, lam_init),
        out_shape=jax.ShapeDtypeStruct((b, s, DIFF_HEADS * HEAD_DIM), BF16),
        grid=(b, DIFF_HEADS, nq),
        in_specs=[
            pl.BlockSpec(memory_space=pltpu.SMEM),
            pl.BlockSpec((1, ATT_TILE, HEAD_DIM), lambda bi, h, i: (bi, i, COL_DIFF_Q + h)),
            pl.BlockSpec((1, s, HEAD_DIM), lambda bi, h, i: (bi, 0, COL_DIFF_K + h)),
            pl.BlockSpec((1, s, HEAD_DIM), lambda bi, h, i: (bi, 0, COL_DIFF_V + h)),
            pl.BlockSpec((1, DIFF_QK_DIM), lambda bi, h, i: (0, 0)),
            pl.BlockSpec((1, DIFF_QK_DIM), lambda bi, h, i: (0, 0)),
            pl.BlockSpec((4, DIFF_QK_DIM), lambda bi, h, i: (0, 0)),
            pl.BlockSpec((1, HEAD_DIM), lambda bi, h, i: (0, 0)),
            pl.BlockSpec((1, ATT_TILE, 2 * ATT_TILE), lambda bi, h, i: (MOBA_HEADS + h, 0, 0)),
        ],
        out_specs=pl.BlockSpec((1, ATT_TILE, HEAD_DIM), lambda bi, h, i: (bi, i, h)),
        scratch_shapes=[
            pltpu.VMEM((s, DIFF_QK_DIM), BF16),
            pltpu.VMEM((s, DIFF_QK_DIM), BF16),
            pltpu.VMEM((s, HEAD_DIM), BF16),
            pltpu.VMEM((ATT_TILE, 1), F32),
            pltpu.VMEM((ATT_TILE, 1), F32),
            pltpu.VMEM((ATT_TILE, HEAD_DIM), F32),
            pltpu.VMEM((ATT_TILE, 1), F32),
            pltpu.VMEM((ATT_TILE, 1), F32),
            pltpu.VMEM((ATT_TILE, HEAD_DIM), F32),
        ],
        compiler_params=_params(("parallel", "parallel", "arbitrary")),
        name="diff",
    )(tab_flat, proj, proj, proj, q_gain.reshape(col(q_gain)), k_gain.reshape(col(k_gain)), lam_params,
      subln.reshape(col(subln)), strips)


def _wout_kernel(x_ref, yc_ref, ym_ref, yd_ref, w_ref, o_ref):
    c0 = CONV_CH
    c1 = c0 + MOBA_HEADS * HEAD_DIM
    o_ref[...] = (x_ref[...] + _dot(yc_ref[...], w_ref[0:c0, :]) + _dot(ym_ref[...], w_ref[c0:c1, :])
                  + _dot(yd_ref[...], w_ref[c1:, :]))


def _wout(x, yc, ym, yd, w, tm=1024, tn=1024):
    t, d = x.shape
    return pl.pallas_call(
        _wout_kernel,
        out_shape=jax.ShapeDtypeStruct((t, d), F32),
        grid=(t // tm, d // tn),
        in_specs=[
            pl.BlockSpec((tm, tn), lambda i, j: (i, j)),
            pl.BlockSpec((tm, yc.shape[1]), lambda i, j: (i, 0)),
            pl.BlockSpec((tm, ym.shape[1]), lambda i, j: (i, 0)),
            pl.BlockSpec((tm, yd.shape[1]), lambda i, j: (i, 0)),
            pl.BlockSpec((w.shape[0], tn), lambda i, j: (0, j)),
        ],
        out_specs=pl.BlockSpec((tm, tn), lambda i, j: (i, j)),
        compiler_params=_params(("parallel", "parallel")),
        name="wout",
    )(x, yc, ym, yd, w)


def _memkv_kernel(m_ref, g_ref, w_ref, kg_ref, k_ref, v_ref):
    hm = _rms(m_ref[0], g_ref[...]).astype(BF16)
    kv = _dot(hm, w_ref[...])
    width = MEM_HEADS * HEAD_DIM
    for h in range(MEM_HEADS):
        cols = slice(h * HEAD_DIM, (h + 1) * HEAD_DIM)
        k_ref[0, :, cols] = _rms(kv[:, cols], kg_ref[...]).astype(BF16)
    v_ref[0] = kv[:, width:].astype(BF16)


def _memkv(mem, gain, wkv, k_gain):
    b, m, d = mem.shape
    width = MEM_HEADS * HEAD_DIM
    out = jax.ShapeDtypeStruct((b, m, width), BF16)
    return pl.pallas_call(
        _memkv_kernel,
        out_shape=(out, out),
        grid=(b,),
        in_specs=[
            pl.BlockSpec((1, m, d), lambda i: (i, 0, 0)),
            pl.BlockSpec((1, d), lambda i: (0, 0)),
            pl.BlockSpec((d, 2 * width), lambda i: (0, 0)),
            pl.BlockSpec((1, HEAD_DIM), lambda i: (0, 0)),
        ],
        out_specs=(pl.BlockSpec((1, m, width), lambda i: (i, 0, 0)), pl.BlockSpec((1, m, width), lambda i: (i, 0, 0))),
        compiler_params=_params(("parallel",)),
        name="memkv",
    )(mem, gain.reshape(1, d), wkv, k_gain.reshape(1, HEAD_DIM))


def _memattn_kernel(x_ref, g_ref, wq_ref, qg_ref, k_ref, v_ref, wo_ref, o_ref, ob_ref):
    x = x_ref[0]
    hx = _rms(x, g_ref[...]).astype(BF16)
    q = _dot(hx, wq_ref[...])
    for h in range(MEM_HEADS):
        cols = slice(h * HEAD_DIM, (h + 1) * HEAD_DIM)
        qh = (_rms(q[:, cols], qg_ref[...]) * HEAD_DIM ** -0.5).astype(BF16)
        s = _dot_nt(qh, k_ref[0, :, cols])
        p = jnp.exp(s - jnp.max(s, axis=-1, keepdims=True))
        oh = _dot(p.astype(BF16), v_ref[0, :, cols]) / jnp.sum(p, axis=-1, keepdims=True)
        ob_ref[:, cols] = oh.astype(BF16)
    o_ref[0] = x + _dot(ob_ref[...], wo_ref[...])


def _memattn(x, gain, wq, q_gain, kn, v, wo, tm=512):
    b, s, d = x.shape
    m = kn.shape[1]
    width = MEM_HEADS * HEAD_DIM
    return pl.pallas_call(
        _memattn_kernel,
        out_shape=jax.ShapeDtypeStruct((b, s, d), F32),
        grid=(b, s // tm),
        in_specs=[
            pl.BlockSpec((1, tm, d), lambda bi, i: (bi, i, 0)),
            pl.BlockSpec((1, d), lambda bi, i: (0, 0)),
            pl.BlockSpec((d, width), lambda bi, i: (0, 0)),
            pl.BlockSpec((1, HEAD_DIM), lambda bi, i: (0, 0)),
            pl.BlockSpec((1, m, width), lambda bi, i: (bi, 0, 0)),
            pl.BlockSpec((1, m, width), lambda bi, i: (bi, 0, 0)),
            pl.BlockSpec((width, d), lambda bi, i: (0, 0)),
        ],
        out_specs=pl.BlockSpec((1, tm, d), lambda bi, i: (bi, i, 0)),
        scratch_shapes=[pltpu.VMEM((tm, width), BF16)],
        compiler_params=_params(("parallel", "parallel")),
        name="memattn",
    )(x, gain.reshape(1, d), wq, q_gain.reshape(1, HEAD_DIM), kn, v, wo)


def kernel(x, mem, rel_bias, ffn1_norm, ffn1_w_gate, ffn1_w_up, ffn1_w_down, mix_norm, w_in, conv_w, conv_b, conv_ln_g, conv_ln_b, moba_q_norm, moba_k_norm, diff_q_norm, diff_k_norm, diff_lambda, diff_subln, w_out, mem_norm_x, mem_norm_m, mem_wq, mem_wkv, mem_q_norm, mem_k_norm, mem_wo, ffn2_norm, ffn2_w_gate, ffn2_w_up, ffn2_w_down):
    b, s, d = x.shape
    depth = w_in.shape[0]
    t = b * s
    tab_flat = rel_bias.T.reshape(-1)
    strips = _bias_strips(tab_flat)
    cast = lambda w: w.astype(BF16)
    xt = x.reshape(t, d)
    for l in range(depth):
        xt = _ffn(xt, ffn1_norm[l], cast(ffn1_w_gate[l]), cast(ffn1_w_up[l]), cast(ffn1_w_down[l]))
        proj = _proj(xt, mix_norm[l], cast(w_in[l])).reshape(b, s, PROJ_WIDTH)
        y_conv = _conv(proj, conv_w[l], conv_b[l], conv_ln_g[l], conv_ln_b[l])
        y_moba = _moba(proj, tab_flat, strips, moba_q_norm[l], moba_k_norm[l])
        lam_init = 0.8 - 0.6 * math.exp(-0.3 * l)
        y_diff = _diff(proj, tab_flat, strips, diff_q_norm[l], diff_k_norm[l], diff_lambda[l], diff_subln[l], lam_init)
        xt = _wout(xt, y_conv.reshape(t, -1), y_moba.reshape(t, -1), y_diff.reshape(t, -1), cast(w_out[l]))
        kn, v = _memkv(mem, mem_norm_m[l], cast(mem_wkv[l]), mem_k_norm[l])
        xt = _memattn(xt.reshape(b, s, d), mem_norm_x[l], cast(mem_wq[l]), mem_q_norm[l], kn, v,
                      cast(mem_wo[l])).reshape(t, d)
        xt = _ffn(xt, ffn2_norm[l], cast(ffn2_w_gate[l]), cast(ffn2_w_up[l]), cast(ffn2_w_down[l]))
    return xt.reshape(b, s, d)
```

```python
import functools
import math

import jax
import jax.numpy as jnp
from jax import lax
from jax.experimental import pallas as pl
from jax.experimental.pallas import tpu as pltpu

F32 = jnp.float32
BF16 = jnp.bfloat16

D_MODEL = 2048
D_FF = 5632
CONV_CH = 512
CONV_WIDTH = 31
CONV_PAD = 32
HEAD_DIM = 128
LANES = 128
MOBA_HEADS = 6
MOBA_BLOCK = 256
MOBA_TOPK = 3
DIFF_HEADS = 6
DIFF_QK_DIM = 64
MEM_HEADS = 4
N_BUCKETS = 32
MAX_DISTANCE = 128
N_BIAS_HEADS = MOBA_HEADS + DIFF_HEADS
NEG_INF = -1e30
TAKEN = -3e38
PROJ_WIDTH = 5632
COL_CONV_A = 0
COL_CONV_G = CONV_CH // HEAD_DIM
COL_MOBA_Q = 2 * CONV_CH // HEAD_DIM
COL_MOBA_K = COL_MOBA_Q + MOBA_HEADS
COL_MOBA_V = COL_MOBA_K + MOBA_HEADS
COL_DIFF_Q = COL_MOBA_V + MOBA_HEADS
COL_DIFF_K = COL_DIFF_Q + DIFF_HEADS
COL_DIFF_V = COL_DIFF_K + DIFF_HEADS

ATT_TILE = 256
assert ATT_TILE == MOBA_BLOCK and ATT_TILE >= MAX_DISTANCE
LOG2E = 1.4426950408889634
VMEM_LIMIT = 56 * 1024 * 1024


def _params(semantics, vmem=VMEM_LIMIT):
    return pltpu.CompilerParams(dimension_semantics=semantics, vmem_limit_bytes=vmem)


def _rms(x, gain, eps=1e-6):
    return x * lax.rsqrt(jnp.mean(x * x, axis=-1, keepdims=True) + eps) * gain


def _dot(a, b):
    return jnp.dot(a, b, preferred_element_type=F32)


def _dot_nt(a, b, precision=None):
    return lax.dot_general(a, b, (((1,), (1,)), ((), ())), precision=precision, preferred_element_type=F32)


def _ffn_kernel(x_ref, g_ref, wg_ref, wu_ref, wd_ref, o_ref, h_ref):
    k = pl.program_id(1)

    @pl.when(k == 0)
    def _():
        x = x_ref[...]
        h_ref[...] = _rms(x, g_ref[...]).astype(BF16)
        o_ref[...] = x

    h = h_ref[...]
    gate = _dot(h, wg_ref[...])
    up = _dot(h, wu_ref[...])
    act = (0.5 * gate / (1.0 + jnp.exp(-gate)) * up).astype(BF16)
    o_ref[...] += _dot(act, wd_ref[...])


def _ffn(x, gain, wg, wu, wd, tm=512, tf=512):
    t, d = x.shape
    f = wg.shape[1]
    return pl.pallas_call(
        _ffn_kernel,
        out_shape=jax.ShapeDtypeStruct((t, d), F32),
        grid=(t // tm, f // tf),
        in_specs=[
            pl.BlockSpec((tm, d), lambda i, k: (i, 0)),
            pl.BlockSpec((1, d), lambda i, k: (0, 0)),
            pl.BlockSpec((d, tf), lambda i, k: (0, k)),
            pl.BlockSpec((d, tf), lambda i, k: (0, k)),
            pl.BlockSpec((tf, d), lambda i, k: (k, 0)),
        ],
        out_specs=pl.BlockSpec((tm, d), lambda i, k: (i, 0)),
        scratch_shapes=[pltpu.VMEM((tm, d), BF16)],
        compiler_params=_params(("parallel", "arbitrary")),
        name="ffn",
    )(x, gain.reshape(1, d), wg, wu, wd)


def _proj_kernel(x_ref, g_ref, w_ref, o_ref, h_ref):
    @pl.when(pl.program_id(1) == 0)
    def _():
        h_ref[...] = _rms(x_ref[...], g_ref[...]).astype(BF16)

    o_ref[...] = _dot(h_ref[...], w_ref[...])


def _proj(x, gain, w, tm=1024, tn=512):
    t, d = x.shape
    n = w.shape[1]
    return pl.pallas_call(
        _proj_kernel,
        out_shape=jax.ShapeDtypeStruct((t, n), F32),
        grid=(t // tm, n // tn),
        in_specs=[
            pl.BlockSpec((tm, d), lambda i, j: (i, 0)),
            pl.BlockSpec((1, d), lambda i, j: (0, 0)),
            pl.BlockSpec((d, tn), lambda i, j: (0, j)),
        ],
        out_specs=pl.BlockSpec((tm, tn), lambda i, j: (i, j)),
        scratch_shapes=[pltpu.VMEM((tm, d), BF16)],
        compiler_params=_params(("parallel", "arbitrary")),
        name="proj",
    )(x, gain.reshape(1, d), w)


def _bias_kernel(tab_ref, o_ref):
    h = pl.program_id(0)
    far = tab_ref[h * N_BUCKETS + N_BUCKETS - 1]
    shape = (ATT_TILE, 2 * ATT_TILE)
    dist = ATT_TILE + lax.broadcasted_iota(jnp.int32, shape, 0) - lax.broadcasted_iota(jnp.int32, shape, 1)
    dist = jnp.maximum(dist, 0)
    max_exact = N_BUCKETS // 2
    log_ratio = jnp.log(jnp.maximum(dist, 1).astype(F32) / max_exact) / math.log(MAX_DISTANCE / max_exact)
    large = jnp.minimum(max_exact + (log_ratio * (N_BUCKETS - max_exact)).astype(jnp.int32), N_BUCKETS - 1)
    bucket = jnp.where(dist < max_exact, dist, large)
    out = jnp.zeros(shape, F32)
    for b in range(N_BUCKETS):
        out = jnp.where(bucket == b, (tab_ref[h * N_BUCKETS + b] - far) * LOG2E, out)
    o_ref[0] = out


def _bias_strips(tab_flat):
    return pl.pallas_call(
        _bias_kernel,
        out_shape=jax.ShapeDtypeStruct((N_BIAS_HEADS, ATT_TILE, 2 * ATT_TILE), F32),
        grid=(N_BIAS_HEADS,),
        in_specs=[pl.BlockSpec(memory_space=pltpu.SMEM)],
        out_specs=pl.BlockSpec((1, ATT_TILE, 2 * ATT_TILE), lambda h: (h, 0, 0)),
        compiler_params=_params(("arbitrary",)),
        name="bias_strips",
    )(tab_flat)


CONV_ROWS = 64


def _conv_kernel(a_ref, g_ref, w_ref, cb_ref, lg_ref, lb_ref, o_ref, u_ref):
    s = a_ref.shape[1]
    u_ref[0:CONV_PAD, :] = jnp.zeros((CONV_PAD, CONV_CH), F32)

    def glu(r, carry):
        rows = pl.ds(pl.multiple_of(r * CONV_ROWS, CONV_ROWS), CONV_ROWS)
        a = a_ref[0, rows, :]
        g = g_ref[0, rows, :]
        u_ref[pl.ds(pl.multiple_of(CONV_PAD + r * CONV_ROWS, CONV_ROWS // 2), CONV_ROWS), :] = a / (1.0 + jnp.exp(-g))
        return carry

    lax.fori_loop(0, s // CONV_ROWS, glu, 0)

    def conv(r, carry):
        base = pl.multiple_of(r * CONV_ROWS, CONV_ROWS)
        win = u_ref[pl.ds(base, CONV_ROWS + CONV_PAD), :]
        acc = jnp.zeros((CONV_ROWS, CONV_CH), F32) + cb_ref[...]
        for j in range(CONV_WIDTH):
            off = CONV_PAD - (CONV_WIDTH - 1) + j
            acc = acc + win[off:off + CONV_ROWS, :] * w_ref[j:j + 1, :]
        mu = jnp.mean(acc, axis=-1, keepdims=True)
        cen = acc - mu
        var = jnp.mean(cen * cen, axis=-1, keepdims=True)
        y = cen * lax.rsqrt(var + 1e-5) * lg_ref[...] + lb_ref[...]
        o_ref[0, pl.ds(base, CONV_ROWS), :] = (y / (1.0 + jnp.exp(-y))).astype(o_ref.dtype)
        return carry

    lax.fori_loop(0, s // CONV_ROWS, conv, 0)


def _conv(proj, w, cb, lg, lb):
    b, s, _ = proj.shape
    vec = lambda v: v.reshape(1, CONV_CH)
    return pl.pallas_call(
        _conv_kernel,
        out_shape=jax.ShapeDtypeStruct((b, s, CONV_CH), BF16),
        grid=(b,),
        in_specs=[
            pl.BlockSpec((1, s, CONV_CH), lambda i: (i, 0, 0)),
            pl.BlockSpec((1, s, CONV_CH), lambda i: (i, 0, 1)),
            pl.BlockSpec((CONV_WIDTH, CONV_CH), lambda i: (0, 0)),
            pl.BlockSpec((1, CONV_CH), lambda i: (0, 0)),
            pl.BlockSpec((1, CONV_CH), lambda i: (0, 0)),
            pl.BlockSpec((1, CONV_CH), lambda i: (0, 0)),
        ],
        out_specs=pl.BlockSpec((1, s, CONV_CH), lambda i: (i, 0, 0)),
        scratch_shapes=[pltpu.VMEM((s + CONV_PAD, CONV_CH), F32)],
        compiler_params=_params(("parallel",)),
        name="conv",
    )(proj, proj, w, vec(cb), vec(lg), vec(lb))


def _tile_rows(j):
    return slice(j * ATT_TILE, (j + 1) * ATT_TILE)


def _lane_fold(op, x):
    out = x[:, :LANES]
    for c in range(1, x.shape[1] // LANES):
        out = op(out, x[:, c * LANES:(c + 1) * LANES])
    return out


def _attend_tile(i, q, k_ref, v_ref, strip_ref, sel_add, s_ref, p_ref):
    mask = (lax.broadcasted_iota(jnp.int32, (ATT_TILE, ATT_TILE), 0)
            >= lax.broadcasted_iota(jnp.int32, (ATT_TILE, ATT_TILE), 1))
    m_acc = None
    for j in range(i + 1):
        s = _dot_nt(q, k_ref[_tile_rows(j), :])
        if j == i:
            s = jnp.where(mask, s + strip_ref[0, :, ATT_TILE:], NEG_INF)
        elif j == i - 1:
            s = s + strip_ref[0, :, :ATT_TILE]
        if sel_add is not None and j < i:
            s = s + sel_add[:, j:j + 1]
        s_ref[:, _tile_rows(j)] = s
        t = _lane_fold(jnp.maximum, s)
        m_acc = t if m_acc is None else jnp.maximum(m_acc, t)
    m = jnp.max(m_acc, axis=-1, keepdims=True)
    l_acc = None
    for j in range(i + 1):
        p = jnp.exp2(s_ref[:, _tile_rows(j)] - m)
        p_ref[:, _tile_rows(j)] = p.astype(BF16)
        t = _lane_fold(jnp.add, p)
        l_acc = t if l_acc is None else l_acc + t
    n = (i + 1) * ATT_TILE
    return _dot(p_ref[:, :n], v_ref[:n, :]), jnp.sum(l_acc, axis=-1, keepdims=True)


def _moba_kernel(q_ref, k_ref, v_ref, qg_ref, kg_ref, strip_ref, o_ref, kn_ref, vb_ref, kmean_ref, s_ref, p_ref):
    nb = k_ref.shape[1] // MOBA_BLOCK
    for j in range(nb):
        rows = _tile_rows(j)
        kn = _rms(k_ref[0, rows, :], kg_ref[...])
        kn_ref[rows, :] = kn.astype(BF16)
        kmean_ref[j:j + 1, :] = jnp.sum(kn, axis=0, keepdims=True) / float(MOBA_BLOCK)
        vb_ref[rows, :] = v_ref[0, rows, :].astype(BF16)

    eye = (lax.broadcasted_iota(jnp.int32, (MOBA_BLOCK, MOBA_BLOCK), 0)
           == lax.broadcasted_iota(jnp.int32, (MOBA_BLOCK, MOBA_BLOCK), 1)).astype(F32).astype(BF16)
    n_sel = min(MOBA_TOPK, nb)
    for i in range(nb):
        qn = _rms(q_ref[0, _tile_rows(i), :], qg_ref[...])
        sel_add = None
        if i > n_sel:
            gate = _dot_nt(kmean_ref[...], qn, precision=lax.Precision.HIGHEST)
            blk = lax.broadcasted_iota(jnp.int32, gate.shape, 0)
            gate = jnp.where(blk < i, gate, TAKEN)
            sel = jnp.zeros(gate.shape, F32)
            for _ in range(n_sel):
                top = jnp.max(gate, axis=0, keepdims=True)
                first = jnp.min(jnp.where(gate == top, blk, nb), axis=0, keepdims=True)
                pick = blk == first
                sel = jnp.where(pick, 1.0, sel)
                gate = jnp.where(pick, TAKEN, gate)
            sel_t = _dot_nt(eye, sel.astype(BF16))
            sel_add = (sel_t - 1.0) * -NEG_INF
        q = (qn * (HEAD_DIM ** -0.5 * LOG2E)).astype(BF16)
        acc, l = _attend_tile(i, q, kn_ref, vb_ref, strip_ref, sel_add, s_ref, p_ref)
        o_ref[0, _tile_rows(i), :] = (acc / l).astype(o_ref.dtype)


def _moba(proj, strips, q_gain, k_gain):
    b, s, _ = proj.shape
    nb = s // MOBA_BLOCK
    return pl.pallas_call(
        _moba_kernel,
        out_shape=jax.ShapeDtypeStruct((b, s, MOBA_HEADS * HEAD_DIM), BF16),
        grid=(b, MOBA_HEADS),
        in_specs=[
            pl.BlockSpec((1, s, HEAD_DIM), lambda bi, h: (bi, 0, COL_MOBA_Q + h)),
            pl.BlockSpec((1, s, HEAD_DIM), lambda bi, h: (bi, 0, COL_MOBA_K + h)),
            pl.BlockSpec((1, s, HEAD_DIM), lambda bi, h: (bi, 0, COL_MOBA_V + h)),
            pl.BlockSpec((1, HEAD_DIM), lambda bi, h: (0, 0)),
            pl.BlockSpec((1, HEAD_DIM), lambda bi, h: (0, 0)),
            pl.BlockSpec((1, ATT_TILE, 2 * ATT_TILE), lambda bi, h: (h, 0, 0)),
        ],
        out_specs=pl.BlockSpec((1, s, HEAD_DIM), lambda bi, h: (bi, 0, h)),
        scratch_shapes=[
            pltpu.VMEM((s, HEAD_DIM), BF16),
            pltpu.VMEM((s, HEAD_DIM), BF16),
            pltpu.VMEM((nb, HEAD_DIM), F32),
            pltpu.VMEM((ATT_TILE, s), F32),
            pltpu.VMEM((ATT_TILE, s), BF16),
        ],
        compiler_params=_params(("parallel", "parallel")),
        name="moba",
    )(proj, proj, proj, q_gain.reshape(1, HEAD_DIM), k_gain.reshape(1, HEAD_DIM), strips)


def _diff_kernel(lam_init, q_ref, k_ref, v_ref, qg_ref, kg_ref, lam_ref, sg_ref, strip_ref, o_ref,
                 k0_ref, k1_ref, vb_ref, s0_ref, s1_ref, p0_ref, p1_ref):
    nt = k_ref.shape[1] // ATT_TILE
    dq = DIFF_QK_DIM
    for j in range(nt):
        rows = _tile_rows(j)
        k = k_ref[0, rows, :]
        k0_ref[rows, :] = _rms(k[:, :dq], kg_ref[...]).astype(BF16)
        k1_ref[rows, :] = _rms(k[:, dq:], kg_ref[...]).astype(BF16)
        vb_ref[rows, :] = v_ref[0, rows, :].astype(BF16)

    lp = lam_ref[...]
    lam = (jnp.exp(jnp.sum(lp[0:1] * lp[1:2], axis=-1, keepdims=True))
           - jnp.exp(jnp.sum(lp[2:3] * lp[3:4], axis=-1, keepdims=True)) + lam_init)
    scale = dq ** -0.5 * LOG2E
    for i in range(nt):
        q = q_ref[0, _tile_rows(i), :]
        q0 = (_rms(q[:, :dq], qg_ref[...]) * scale).astype(BF16)
        q1 = (_rms(q[:, dq:], qg_ref[...]) * scale).astype(BF16)
        a0, l0 = _attend_tile(i, q0, k0_ref, vb_ref, strip_ref, None, s0_ref, p0_ref)
        a1, l1 = _attend_tile(i, q1, k1_ref, vb_ref, strip_ref, None, s1_ref, p1_ref)
        o = a0 / l0 - lam * (a1 / l1)
        o_ref[0, _tile_rows(i), :] = (_rms(o, sg_ref[...]) * (1.0 - lam_init)).astype(o_ref.dtype)


def _diff(proj, strips, q_gain, k_gain, lam_params, subln, lam_init):
    b, s, _ = proj.shape
    col = lambda v: (1, v.shape[-1])
    return pl.pallas_call(
        functools.partial(_diff_kernel, lam_init),
        out_shape=jax.ShapeDtypeStruct((b, s, DIFF_HEADS * HEAD_DIM), BF16),
        grid=(b, DIFF_HEADS),
        in_specs=[
            pl.BlockSpec((1, s, HEAD_DIM), lambda bi, h: (bi, 0, COL_DIFF_Q + h)),
            pl.BlockSpec((1, s, HEAD_DIM), lambda bi, h: (bi, 0, COL_DIFF_K + h)),
            pl.BlockSpec((1, s, HEAD_DIM), lambda bi, h: (bi, 0, COL_DIFF_V + h)),
            pl.BlockSpec((1, DIFF_QK_DIM), lambda bi, h: (0, 0)),
            pl.BlockSpec((1, DIFF_QK_DIM), lambda bi, h: (0, 0)),
            pl.BlockSpec((4, DIFF_QK_DIM), lambda bi, h: (0, 0)),
            pl.BlockSpec((1, HEAD_DIM), lambda bi, h: (0, 0)),
            pl.BlockSpec((1, ATT_TILE, 2 * ATT_TILE), lambda bi, h: (MOBA_HEADS + h, 0, 0)),
        ],
        out_specs=pl.BlockSpec((1, s, HEAD_DIM), lambda bi, h: (bi, 0, h)),
        scratch_shapes=[
            pltpu.VMEM((s, DIFF_QK_DIM), BF16),
            pltpu.VMEM((s, DIFF_QK_DIM), BF16),
            pltpu.VMEM((s, HEAD_DIM), BF16),
            pltpu.VMEM((ATT_TILE, s), F32),
            pltpu.VMEM((ATT_TILE, s), F32),
            pltpu.VMEM((ATT_TILE, s), BF16),
            pltpu.VMEM((ATT_TILE, s), BF16),
        ],
        compiler_params=_params(("parallel", "parallel")),
        name="diff",
    )(proj, proj, proj, q_gain.reshape(col(q_gain)), k_gain.reshape(col(k_gain)), lam_params,
      subln.reshape(col(subln)), strips)


def _wout_kernel(x_ref, yc_ref, ym_ref, yd_ref, w_ref, o_ref):
    c0 = CONV_CH
    c1 = c0 + MOBA_HEADS * HEAD_DIM
    o_ref[...] = (x_ref[...] + _dot(yc_ref[...], w_ref[0:c0, :]) + _dot(ym_ref[...], w_ref[c0:c1, :])
                  + _dot(yd_ref[...], w_ref[c1:, :]))


def _wout(x, yc, ym, yd, w, tm=1024, tn=1024):
    t, d = x.shape
    return pl.pallas_call(
        _wout_kernel,
        out_shape=jax.ShapeDtypeStruct((t, d), F32),
        grid=(t // tm, d // tn),
        in_specs=[
            pl.BlockSpec((tm, tn), lambda i, j: (i, j)),
            pl.BlockSpec((tm, yc.shape[1]), lambda i, j: (i, 0)),
            pl.BlockSpec((tm, ym.shape[1]), lambda i, j: (i, 0)),
            pl.BlockSpec((tm, yd.shape[1]), lambda i, j: (i, 0)),
            pl.BlockSpec((w.shape[0], tn), lambda i, j: (0, j)),
        ],
        out_specs=pl.BlockSpec((tm, tn), lambda i, j: (i, j)),
        compiler_params=_params(("parallel", "parallel")),
        name="wout",
    )(x, yc, ym, yd, w)


def _memkv_kernel(m_ref, g_ref, w_ref, kg_ref, k_ref, v_ref):
    hm = _rms(m_ref[0], g_ref[...]).astype(BF16)
    kv = _dot(hm, w_ref[...])
    width = MEM_HEADS * HEAD_DIM
    for h in range(MEM_HEADS):
        cols = slice(h * HEAD_DIM, (h + 1) * HEAD_DIM)
        k_ref[0, :, cols] = _rms(kv[:, cols], kg_ref[...]).astype(BF16)
    v_ref[0] = kv[:, width:].astype(BF16)


def _memkv(mem, gain, wkv, k_gain):
    b, m, d = mem.shape
    width = MEM_HEADS * HEAD_DIM
    out = jax.ShapeDtypeStruct((b, m, width), BF16)
    return pl.pallas_call(
        _memkv_kernel,
        out_shape=(out, out),
        grid=(b,),
        in_specs=[
            pl.BlockSpec((1, m, d), lambda i: (i, 0, 0)),
            pl.BlockSpec((1, d), lambda i: (0, 0)),
            pl.BlockSpec((d, 2 * width), lambda i: (0, 0)),
            pl.BlockSpec((1, HEAD_DIM), lambda i: (0, 0)),
        ],
        out_specs=(pl.BlockSpec((1, m, width), lambda i: (i, 0, 0)), pl.BlockSpec((1, m, width), lambda i: (i, 0, 0))),
        compiler_params=_params(("parallel",)),
        name="memkv",
    )(mem, gain.reshape(1, d), wkv, k_gain.reshape(1, HEAD_DIM))


def _memattn_kernel(x_ref, g_ref, wq_ref, qg_ref, k_ref, v_ref, wo_ref, o_ref, ob_ref):
    x = x_ref[0]
    hx = _rms(x, g_ref[...]).astype(BF16)
    q = _dot(hx, wq_ref[...])
    for h in range(MEM_HEADS):
        cols = slice(h * HEAD_DIM, (h + 1) * HEAD_DIM)
        qh = (_rms(q[:, cols], qg_ref[...]) * HEAD_DIM ** -0.5).astype(BF16)
        s = _dot_nt(qh, k_ref[0, :, cols])
        p = jnp.exp(s - jnp.max(s, axis=-1, keepdims=True))
        oh = _dot(p.astype(BF16), v_ref[0, :, cols]) / jnp.sum(p, axis=-1, keepdims=True)
        ob_ref[:, cols] = oh.astype(BF16)
    o_ref[0] = x + _dot(ob_ref[...], wo_ref[...])


def _memattn(x, gain, wq, q_gain, kn, v, wo, tm=512):
    b, s, d = x.shape
    m = kn.shape[1]
    width = MEM_HEADS * HEAD_DIM
    return pl.pallas_call(
        _memattn_kernel,
        out_shape=jax.ShapeDtypeStruct((b, s, d), F32),
        grid=(b, s // tm),
        in_specs=[
            pl.BlockSpec((1, tm, d), lambda bi, i: (bi, i, 0)),
            pl.BlockSpec((1, d), lambda bi, i: (0, 0)),
            pl.BlockSpec((d, width), lambda bi, i: (0, 0)),
            pl.BlockSpec((1, HEAD_DIM), lambda bi, i: (0, 0)),
            pl.BlockSpec((1, m, width), lambda bi, i: (bi, 0, 0)),
            pl.BlockSpec((1, m, width), lambda bi, i: (bi, 0, 0)),
            pl.BlockSpec((width, d), lambda bi, i: (0, 0)),
        ],
        out_specs=pl.BlockSpec((1, tm, d), lambda bi, i: (bi, i, 0)),
        scratch_shapes=[pltpu.VMEM((tm, width), BF16)],
        compiler_params=_params(("parallel", "parallel")),
        name="memattn",
    )(x, gain.reshape(1, d), wq, q_gain.reshape(1, HEAD_DIM), kn, v, wo)


def kernel(x, mem, rel_bias, ffn1_norm, ffn1_w_gate, ffn1_w_up, ffn1_w_down, mix_norm, w_in, conv_w, conv_b, conv_ln_g, conv_ln_b, moba_q_norm, moba_k_norm, diff_q_norm, diff_k_norm, diff_lambda, diff_subln, w_out, mem_norm_x, mem_norm_m, mem_wq, mem_wkv, mem_q_norm, mem_k_norm, mem_wo, ffn2_norm, ffn2_w_gate, ffn2_w_up, ffn2_w_down):
    b, s, d = x.shape
    depth = w_in.shape[0]
    t = b * s
    tab_flat = rel_bias.T.reshape(-1)
    strips = _bias_strips(tab_flat)
    cast = lambda w: w.astype(BF16)
    xt = x.reshape(t, d)
    for l in range(depth):
        xt = _ffn(xt, ffn1_norm[l], cast(ffn1_w_gate[l]), cast(ffn1_w_up[l]), cast(ffn1_w_down[l]))
        proj = _proj(xt, mix_norm[l], cast(w_in[l])).reshape(b, s, PROJ_WIDTH)
        y_conv = _conv(proj, conv_w[l], conv_b[l], conv_ln_g[l], conv_ln_b[l])
        y_moba = _moba(proj, strips, moba_q_norm[l], moba_k_norm[l])
        lam_init = 0.8 - 0.6 * math.exp(-0.3 * l)
        y_diff = _diff(proj, strips, diff_q_norm[l], diff_k_norm[l], diff_lambda[l], diff_subln[l], lam_init)
        xt = _wout(xt, y_conv.reshape(t, -1), y_moba.reshape(t, -1), y_diff.reshape(t, -1), cast(w_out[l]))
        kn, v = _memkv(mem, mem_norm_m[l], cast(mem_wkv[l]), mem_k_norm[l])
        xt = _memattn(xt.reshape(b, s, d), mem_norm_x[l], cast(mem_wq[l]), mem_q_norm[l], kn, v,
                      cast(mem_wo[l])).reshape(t, d)
        xt = _ffn(xt, ffn2_norm[l], cast(ffn2_w_gate[l]), cast(ffn2_w_up[l]), cast(ffn2_w_down[l]))
    return xt.reshape(b, s, d)
```

```python
import functools
import math

import jax
import jax.numpy as jnp
from jax import lax
from jax.experimental import pallas as pl
from jax.experimental.pallas import tpu as pltpu

F32 = jnp.float32
BF16 = jnp.bfloat16

D_MODEL = 2048
D_FF = 5632
CONV_CH = 512
CONV_WIDTH = 31
CONV_PAD = 32
HEAD_DIM = 128
LANES = 128
MOBA_HEADS = 6
MOBA_BLOCK = 256
MOBA_TOPK = 3
DIFF_HEADS = 6
DIFF_QK_DIM = 64
MEM_HEADS = 4
N_BUCKETS = 32
MAX_DISTANCE = 128
N_BIAS_HEADS = MOBA_HEADS + DIFF_HEADS
NEG_INF = -1e30
TAKEN = -3e38
PROJ_WIDTH = 5632
COL_CONV_A = 0
COL_CONV_G = CONV_CH // HEAD_DIM
COL_MOBA_Q = 2 * CONV_CH // HEAD_DIM
COL_MOBA_K = COL_MOBA_Q + MOBA_HEADS
COL_MOBA_V = COL_MOBA_K + MOBA_HEADS
COL_DIFF_Q = COL_MOBA_V + MOBA_HEADS
COL_DIFF_K = COL_DIFF_Q + DIFF_HEADS
COL_DIFF_V = COL_DIFF_K + DIFF_HEADS

ATT_TILE = 256
assert ATT_TILE == MOBA_BLOCK and ATT_TILE >= MAX_DISTANCE
LOG2E = 1.4426950408889634
VMEM_LIMIT = 56 * 1024 * 1024


def _params(semantics, vmem=VMEM_LIMIT):
    return pltpu.CompilerParams(dimension_semantics=semantics, vmem_limit_bytes=vmem)


def _rms(x, gain, eps=1e-6):
    return x * lax.rsqrt(jnp.mean(x * x, axis=-1, keepdims=True) + eps) * gain


def _dot(a, b):
    return jnp.dot(a, b, preferred_element_type=F32)


def _dot_nt(a, b, precision=None):
    return lax.dot_general(a, b, (((1,), (1,)), ((), ())), precision=precision, preferred_element_type=F32)


def _ffn_kernel(x_ref, g_ref, wg_ref, wu_ref, wd_ref, o_ref, h_ref):
    k = pl.program_id(1)

    @pl.when(k == 0)
    def _():
        x = x_ref[...]
        h_ref[...] = _rms(x, g_ref[...]).astype(BF16)
        o_ref[...] = x

    h = h_ref[...]
    gate = _dot(h, wg_ref[...])
    up = _dot(h, wu_ref[...])
    act = (0.5 * gate / (1.0 + jnp.exp(-gate)) * up).astype(BF16)
    o_ref[...] += _dot(act, wd_ref[...])


def _ffn(x, gain, wg, wu, wd, l, tm=1024, tf=256):
    t, d = x.shape
    f = wg.shape[2]
    return pl.pallas_call(
        _ffn_kernel,
        out_shape=jax.ShapeDtypeStruct((t, d), F32),
        grid=(t // tm, f // tf),
        in_specs=[
            pl.BlockSpec((tm, d), lambda i, k: (i, 0)),
            pl.BlockSpec((1, d), lambda i, k: (0, 0)),
            pl.BlockSpec((None, d, tf), lambda i, k: (l, 0, k)),
            pl.BlockSpec((None, d, tf), lambda i, k: (l, 0, k)),
            pl.BlockSpec((None, tf, d), lambda i, k: (l, k, 0)),
        ],
        out_specs=pl.BlockSpec((tm, d), lambda i, k: (i, 0)),
        scratch_shapes=[pltpu.VMEM((tm, d), BF16)],
        compiler_params=_params(("parallel", "arbitrary")),
        name="ffn",
    )(x, gain.reshape(1, d), wg, wu, wd)


def _proj_kernel(x_ref, g_ref, w_ref, o_ref, h_ref):
    @pl.when(pl.program_id(1) == 0)
    def _():
        h_ref[...] = _rms(x_ref[...], g_ref[...]).astype(BF16)

    o_ref[...] = _dot(h_ref[...], w_ref[...]).astype(o_ref.dtype)


def _proj(x, gain, w, l, tm=1024, tn=512):
    t, d = x.shape
    n = w.shape[2]
    return pl.pallas_call(
        _proj_kernel,
        out_shape=jax.ShapeDtypeStruct((t, n), BF16),
        grid=(t // tm, n // tn),
        in_specs=[
            pl.BlockSpec((tm, d), lambda i, j: (i, 0)),
            pl.BlockSpec((1, d), lambda i, j: (0, 0)),
            pl.BlockSpec((None, d, tn), lambda i, j: (l, 0, j)),
        ],
        out_specs=pl.BlockSpec((tm, tn), lambda i, j: (i, j)),
        scratch_shapes=[pltpu.VMEM((tm, d), BF16)],
        compiler_params=_params(("parallel", "arbitrary")),
        name="proj",
    )(x, gain.reshape(1, d), w)


def _bias_kernel(tab_ref, o_ref):
    h = pl.program_id(0)
    far = tab_ref[h * N_BUCKETS + N_BUCKETS - 1]
    shape = (ATT_TILE, 2 * ATT_TILE)
    dist = ATT_TILE + lax.broadcasted_iota(jnp.int32, shape, 0) - lax.broadcasted_iota(jnp.int32, shape, 1)
    dist = jnp.maximum(dist, 0)
    max_exact = N_BUCKETS // 2
    log_ratio = jnp.log(jnp.maximum(dist, 1).astype(F32) / max_exact) / math.log(MAX_DISTANCE / max_exact)
    large = jnp.minimum(max_exact + (log_ratio * (N_BUCKETS - max_exact)).astype(jnp.int32), N_BUCKETS - 1)
    bucket = jnp.where(dist < max_exact, dist, large)
    out = jnp.zeros(shape, F32)
    for b in range(N_BUCKETS):
        out = jnp.where(bucket == b, (tab_ref[h * N_BUCKETS + b] - far) * LOG2E, out)
    o_ref[0] = out


def _bias_strips(tab_flat):
    return pl.pallas_call(
        _bias_kernel,
        out_shape=jax.ShapeDtypeStruct((N_BIAS_HEADS, ATT_TILE, 2 * ATT_TILE), F32),
        grid=(N_BIAS_HEADS,),
        in_specs=[pl.BlockSpec(memory_space=pltpu.SMEM)],
        out_specs=pl.BlockSpec((1, ATT_TILE, 2 * ATT_TILE), lambda h: (h, 0, 0)),
        compiler_params=_params(("arbitrary",)),
        name="bias_strips",
    )(tab_flat)


CONV_ROWS = 64


def _conv_kernel(a_ref, g_ref, w_ref, cb_ref, lg_ref, lb_ref, o_ref, u_ref):
    s = a_ref.shape[1]
    u_ref[0:CONV_PAD, :] = jnp.zeros((CONV_PAD, CONV_CH), F32)

    def glu(r, carry):
        rows = pl.ds(pl.multiple_of(r * CONV_ROWS, CONV_ROWS), CONV_ROWS)
        a = a_ref[0, rows, :].astype(F32)
        g = g_ref[0, rows, :].astype(F32)
        u_ref[pl.ds(pl.multiple_of(CONV_PAD + r * CONV_ROWS, CONV_ROWS // 2), CONV_ROWS), :] = a / (1.0 + jnp.exp(-g))
        return carry

    lax.fori_loop(0, s // CONV_ROWS, glu, 0)

    def conv(r, carry):
        base = pl.multiple_of(r * CONV_ROWS, CONV_ROWS)
        win = u_ref[pl.ds(base, CONV_ROWS + CONV_PAD), :]
        acc = jnp.zeros((CONV_ROWS, CONV_CH), F32) + cb_ref[...]
        for j in range(CONV_WIDTH):
            off = CONV_PAD - (CONV_WIDTH - 1) + j
            acc = acc + win[off:off + CONV_ROWS, :] * w_ref[j:j + 1, :]
        mu = jnp.mean(acc, axis=-1, keepdims=True)
        cen = acc - mu
        var = jnp.mean(cen * cen, axis=-1, keepdims=True)
        y = cen * lax.rsqrt(var + 1e-5) * lg_ref[...] + lb_ref[...]
        o_ref[0, pl.ds(base, CONV_ROWS), :] = (y / (1.0 + jnp.exp(-y))).astype(o_ref.dtype)
        return carry

    lax.fori_loop(0, s // CONV_ROWS, conv, 0)


def _conv(proj, w, cb, lg, lb):
    b, s, _ = proj.shape
    vec = lambda v: v.reshape(1, CONV_CH)
    return pl.pallas_call(
        _conv_kernel,
        out_shape=jax.ShapeDtypeStruct((b, s, CONV_CH), BF16),
        grid=(b,),
        in_specs=[
            pl.BlockSpec((1, s, CONV_CH), lambda i: (i, 0, 0)),
            pl.BlockSpec((1, s, CONV_CH), lambda i: (i, 0, 1)),
            pl.BlockSpec((CONV_WIDTH, CONV_CH), lambda i: (0, 0)),
            pl.BlockSpec((1, CONV_CH), lambda i: (0, 0)),
            pl.BlockSpec((1, CONV_CH), lambda i: (0, 0)),
            pl.BlockSpec((1, CONV_CH), lambda i: (0, 0)),
        ],
        out_specs=pl.BlockSpec((1, s, CONV_CH), lambda i: (i, 0, 0)),
        scratch_shapes=[pltpu.VMEM((s + CONV_PAD, CONV_CH), F32)],
        compiler_params=_params(("parallel",)),
        name="conv",
    )(proj, proj, w, vec(cb), vec(lg), vec(lb))


def _tile_rows(j):
    return slice(j * ATT_TILE, (j + 1) * ATT_TILE)


def _lane_fold(op, x):
    out = x[:, :LANES]
    for c in range(1, x.shape[1] // LANES):
        out = op(out, x[:, c * LANES:(c + 1) * LANES])
    return out


def _attend_tile(i, q, k_ref, v_ref, strip_ref, sel_add, s_ref, p_ref):
    mask = (lax.broadcasted_iota(jnp.int32, (ATT_TILE, ATT_TILE), 0)
            >= lax.broadcasted_iota(jnp.int32, (ATT_TILE, ATT_TILE), 1))
    m_acc = None
    for j in range(i + 1):
        s = _dot_nt(q, k_ref[_tile_rows(j), :])
        if j == i:
            s = jnp.where(mask, s + strip_ref[0, :, ATT_TILE:], NEG_INF)
        elif j == i - 1:
            s = s + strip_ref[0, :, :ATT_TILE]
        if sel_add is not None and j < i:
            s = s + sel_add[:, j:j + 1]
        s_ref[:, _tile_rows(j)] = s
        t = _lane_fold(jnp.maximum, s)
        m_acc = t if m_acc is None else jnp.maximum(m_acc, t)
    m = jnp.max(m_acc, axis=-1, keepdims=True)
    l_acc = None
    for j in range(i + 1):
        p = jnp.exp2(s_ref[:, _tile_rows(j)] - m)
        p_ref[:, _tile_rows(j)] = p.astype(BF16)
        t = _lane_fold(jnp.add, p)
        l_acc = t if l_acc is None else l_acc + t
    n = (i + 1) * ATT_TILE
    return _dot(p_ref[:, :n], v_ref[:n, :]), jnp.sum(l_acc, axis=-1, keepdims=True)


def _moba_kernel(q_ref, k_ref, v_ref, qg_ref, kg_ref, strip_ref, o_ref, kn_ref, kmean_ref, s_ref, p_ref):
    nb = k_ref.shape[1] // MOBA_BLOCK
    for j in range(nb):
        rows = _tile_rows(j)
        kn = _rms(k_ref[0, rows, :].astype(F32), kg_ref[...])
        kn_ref[rows, :] = kn.astype(BF16)
        kmean_ref[j:j + 1, :] = jnp.sum(kn, axis=0, keepdims=True) / float(MOBA_BLOCK)

    eye = (lax.broadcasted_iota(jnp.int32, (MOBA_BLOCK, MOBA_BLOCK), 0)
           == lax.broadcasted_iota(jnp.int32, (MOBA_BLOCK, MOBA_BLOCK), 1)).astype(F32).astype(BF16)
    n_sel = min(MOBA_TOPK, nb)
    for i in range(nb):
        qn = _rms(q_ref[0, _tile_rows(i), :].astype(F32), qg_ref[...])
        sel_add = None
        if i > n_sel:
            gate = _dot_nt(kmean_ref[...], qn, precision=lax.Precision.HIGHEST)
            blk = lax.broadcasted_iota(jnp.int32, gate.shape, 0)
            gate = jnp.where(blk < i, gate, TAKEN)
            sel = jnp.zeros(gate.shape, F32)
            for _ in range(n_sel):
                top = jnp.max(gate, axis=0, keepdims=True)
                first = jnp.min(jnp.where(gate == top, blk, nb), axis=0, keepdims=True)
                pick = blk == first
                sel = jnp.where(pick, 1.0, sel)
                gate = jnp.where(pick, TAKEN, gate)
            sel_t = _dot_nt(eye, sel.astype(BF16))
            sel_add = (sel_t - 1.0) * -NEG_INF
        q = (qn * (HEAD_DIM ** -0.5 * LOG2E)).astype(BF16)
        acc, l = _attend_tile(i, q, kn_ref, v_ref.at[0], strip_ref, sel_add, s_ref, p_ref)
        o_ref[0, _tile_rows(i), :] = (acc / l).astype(o_ref.dtype)


def _moba(proj, strips, q_gain, k_gain):
    b, s, _ = proj.shape
    nb = s // MOBA_BLOCK
    return pl.pallas_call(
        _moba_kernel,
        out_shape=jax.ShapeDtypeStruct((b, s, MOBA_HEADS * HEAD_DIM), BF16),
        grid=(b, MOBA_HEADS),
        in_specs=[
            pl.BlockSpec((1, s, HEAD_DIM), lambda bi, h: (bi, 0, COL_MOBA_Q + h)),
            pl.BlockSpec((1, s, HEAD_DIM), lambda bi, h: (bi, 0, COL_MOBA_K + h)),
            pl.BlockSpec((1, s, HEAD_DIM), lambda bi, h: (bi, 0, COL_MOBA_V + h)),
            pl.BlockSpec((1, HEAD_DIM), lambda bi, h: (0, 0)),
            pl.BlockSpec((1, HEAD_DIM), lambda bi, h: (0, 0)),
            pl.BlockSpec((1, ATT_TILE, 2 * ATT_TILE), lambda bi, h: (h, 0, 0)),
        ],
        out_specs=pl.BlockSpec((1, s, HEAD_DIM), lambda bi, h: (bi, 0, h)),
        scratch_shapes=[
            pltpu.VMEM((s, HEAD_DIM), BF16),
            pltpu.VMEM((nb, HEAD_DIM), F32),
            pltpu.VMEM((ATT_TILE, s), F32),
            pltpu.VMEM((ATT_TILE, s), BF16),
        ],
        compiler_params=_params(("parallel", "parallel")),
        name="moba",
    )(proj, proj, proj, q_gain.reshape(1, HEAD_DIM), k_gain.reshape(1, HEAD_DIM), strips)


def _diff_kernel(lam_init, q_ref, k_ref, v_ref, qg_ref, kg_ref, lam_ref, sg_ref, strip_ref, o_ref,
                 k0_ref, k1_ref, s0_ref, s1_ref, p0_ref, p1_ref):
    nt = k_ref.shape[1] // ATT_TILE
    dq = DIFF_QK_DIM
    for j in range(nt):
        rows = _tile_rows(j)
        k = k_ref[0, rows, :].astype(F32)
        k0_ref[rows, :] = _rms(k[:, :dq], kg_ref[...]).astype(BF16)
        k1_ref[rows, :] = _rms(k[:, dq:], kg_ref[...]).astype(BF16)

    lp = lam_ref[...]
    lam = (jnp.exp(jnp.sum(lp[0:1] * lp[1:2], axis=-1, keepdims=True))
           - jnp.exp(jnp.sum(lp[2:3] * lp[3:4], axis=-1, keepdims=True)) + lam_init)
    scale = dq ** -0.5 * LOG2E
    for i in range(nt):
        q = q_ref[0, _tile_rows(i), :].astype(F32)
        q0 = (_rms(q[:, :dq], qg_ref[...]) * scale).astype(BF16)
        q1 = (_rms(q[:, dq:], qg_ref[...]) * scale).astype(BF16)
        a0, l0 = _attend_tile(i, q0, k0_ref, v_ref.at[0], strip_ref, None, s0_ref, p0_ref)
        a1, l1 = _attend_tile(i, q1, k1_ref, v_ref.at[0], strip_ref, None, s1_ref, p1_ref)
        o = a0 / l0 - lam * (a1 / l1)
        o_ref[0, _tile_rows(i), :] = (_rms(o, sg_ref[...]) * (1.0 - lam_init)).astype(o_ref.dtype)


def _diff(proj, strips, q_gain, k_gain, lam_params, subln, lam_init):
    b, s, _ = proj.shape
    col = lambda v: (1, v.shape[-1])
    return pl.pallas_call(
        functools.partial(_diff_kernel, lam_init),
        out_shape=jax.ShapeDtypeStruct((b, s, DIFF_HEADS * HEAD_DIM), BF16),
        grid=(b, DIFF_HEADS),
        in_specs=[
            pl.BlockSpec((1, s, HEAD_DIM), lambda bi, h: (bi, 0, COL_DIFF_Q + h)),
            pl.BlockSpec((1, s, HEAD_DIM), lambda bi, h: (bi, 0, COL_DIFF_K + h)),
            pl.BlockSpec((1, s, HEAD_DIM), lambda bi, h: (bi, 0, COL_DIFF_V + h)),
            pl.BlockSpec((1, DIFF_QK_DIM), lambda bi, h: (0, 0)),
            pl.BlockSpec((1, DIFF_QK_DIM), lambda bi, h: (0, 0)),
            pl.BlockSpec((4, DIFF_QK_DIM), lambda bi, h: (0, 0)),
            pl.BlockSpec((1, HEAD_DIM), lambda bi, h: (0, 0)),
            pl.BlockSpec((1, ATT_TILE, 2 * ATT_TILE), lambda bi, h: (MOBA_HEADS + h, 0, 0)),
        ],
        out_specs=pl.BlockSpec((1, s, HEAD_DIM), lambda bi, h: (bi, 0, h)),
        scratch_shapes=[
            pltpu.VMEM((s, DIFF_QK_DIM), BF16),
            pltpu.VMEM((s, DIFF_QK_DIM), BF16),
            pltpu.VMEM((ATT_TILE, s), F32),
            pltpu.VMEM((ATT_TILE, s), F32),
            pltpu.VMEM((ATT_TILE, s), BF16),
            pltpu.VMEM((ATT_TILE, s), BF16),
        ],
        compiler_params=_params(("parallel", "parallel")),
        name="diff",
    )(proj, proj, proj, q_gain.reshape(col(q_gain)), k_gain.reshape(col(k_gain)), lam_params,
      subln.reshape(col(subln)), strips)


def _wout_kernel(x_ref, yc_ref, ym_ref, yd_ref, w_ref, o_ref):
    c0 = CONV_CH
    c1 = c0 + MOBA_HEADS * HEAD_DIM
    o_ref[...] = (x_ref[...] + _dot(yc_ref[...], w_ref[0:c0, :]) + _dot(ym_ref[...], w_ref[c0:c1, :])
                  + _dot(yd_ref[...], w_ref[c1:, :]))


def _wout(x, yc, ym, yd, w, l, tm=1024, tn=1024):
    t, d = x.shape
    return pl.pallas_call(
        _wout_kernel,
        out_shape=jax.ShapeDtypeStruct((t, d), F32),
        grid=(t // tm, d // tn),
        in_specs=[
            pl.BlockSpec((tm, tn), lambda i, j: (i, j)),
            pl.BlockSpec((tm, yc.shape[1]), lambda i, j: (i, 0)),
            pl.BlockSpec((tm, ym.shape[1]), lambda i, j: (i, 0)),
            pl.BlockSpec((tm, yd.shape[1]), lambda i, j: (i, 0)),
            pl.BlockSpec((None, w.shape[1], tn), lambda i, j: (l, 0, j)),
        ],
        out_specs=pl.BlockSpec((tm, tn), lambda i, j: (i, j)),
        compiler_params=_params(("parallel", "parallel")),
        name="wout",
    )(x, yc, ym, yd, w)


def _memkv_kernel(m_ref, g_ref, w_ref, kg_ref, k_ref, v_ref):
    hm = _rms(m_ref[0], g_ref[...]).astype(BF16)
    kv = _dot(hm, w_ref[...])
    width = MEM_HEADS * HEAD_DIM
    for h in range(MEM_HEADS):
        cols = slice(h * HEAD_DIM, (h + 1) * HEAD_DIM)
        k_ref[0, :, cols] = _rms(kv[:, cols], kg_ref[...]).astype(BF16)
    v_ref[0] = kv[:, width:].astype(BF16)


def _memkv(mem, gain, wkv, k_gain, l):
    b, m, d = mem.shape
    width = MEM_HEADS * HEAD_DIM
    out = jax.ShapeDtypeStruct((b, m, width), BF16)
    return pl.pallas_call(
        _memkv_kernel,
        out_shape=(out, out),
        grid=(b,),
        in_specs=[
            pl.BlockSpec((1, m, d), lambda i: (i, 0, 0)),
            pl.BlockSpec((1, d), lambda i: (0, 0)),
            pl.BlockSpec((None, d, 2 * width), lambda i: (l, 0, 0)),
            pl.BlockSpec((1, HEAD_DIM), lambda i: (0, 0)),
        ],
        out_specs=(pl.BlockSpec((1, m, width), lambda i: (i, 0, 0)), pl.BlockSpec((1, m, width), lambda i: (i, 0, 0))),
        compiler_params=_params(("parallel",)),
        name="memkv",
    )(mem, gain.reshape(1, d), wkv, k_gain.reshape(1, HEAD_DIM))


def _memattn_kernel(x_ref, g_ref, wq_ref, qg_ref, k_ref, v_ref, wo_ref, o_ref, ob_ref):
    x = x_ref[0]
    hx = _rms(x, g_ref[...]).astype(BF16)
    q = _dot(hx, wq_ref[...])
    for h in range(MEM_HEADS):
        cols = slice(h * HEAD_DIM, (h + 1) * HEAD_DIM)
        qh = (_rms(q[:, cols], qg_ref[...]) * HEAD_DIM ** -0.5).astype(BF16)
        s = _dot_nt(qh, k_ref[0, :, cols])
        p = jnp.exp(s - jnp.max(s, axis=-1, keepdims=True))
        oh = _dot(p.astype(BF16), v_ref[0, :, cols]) / jnp.sum(p, axis=-1, keepdims=True)
        ob_ref[:, cols] = oh.astype(BF16)
    o_ref[0] = x + _dot(ob_ref[...], wo_ref[...])


def _memattn(x, gain, wq, q_gain, kn, v, wo, l, tm=512):
    b, s, d = x.shape
    m = kn.shape[1]
    width = MEM_HEADS * HEAD_DIM
    return pl.pallas_call(
        _memattn_kernel,
        out_shape=jax.ShapeDtypeStruct((b, s, d), F32),
        grid=(b, s // tm),
        in_specs=[
            pl.BlockSpec((1, tm, d), lambda bi, i: (bi, i, 0)),
            pl.BlockSpec((1, d), lambda bi, i: (0, 0)),
            pl.BlockSpec((None, d, width), lambda bi, i: (l, 0, 0)),
            pl.BlockSpec((1, HEAD_DIM), lambda bi, i: (0, 0)),
            pl.BlockSpec((1, m, width), lambda bi, i: (bi, 0, 0)),
            pl.BlockSpec((1, m, width), lambda bi, i: (bi, 0, 0)),
            pl.BlockSpec((None, width, d), lambda bi, i: (l, 0, 0)),
        ],
        out_specs=pl.BlockSpec((1, tm, d), lambda bi, i: (bi, i, 0)),
        scratch_shapes=[pltpu.VMEM((tm, width), BF16)],
        compiler_params=_params(("parallel", "parallel")),
        name="memattn",
    )(x, gain.reshape(1, d), wq, q_gain.reshape(1, HEAD_DIM), kn, v, wo)


CAST_BLOCK_BYTES = 6 * 1024 * 1024
BF16_SUBLANES = 16


def _cast_kernel(w_ref, o_ref):
    o_ref[...] = w_ref[...].astype(o_ref.dtype)


def _cast_bf16(w):
    depth, k, n = w.shape
    bk = max(r for r in range(BF16_SUBLANES, k + 1, BF16_SUBLANES) if k % r == 0 and r * n * 4 <= CAST_BLOCK_BYTES)
    return pl.pallas_call(
        _cast_kernel,
        out_shape=jax.ShapeDtypeStruct(w.shape, BF16),
        grid=(depth, k // bk),
        in_specs=[pl.BlockSpec((None, bk, n), lambda l, i: (l, i, 0))],
        out_specs=pl.BlockSpec((None, bk, n), lambda l, i: (l, i, 0)),
        compiler_params=_params(("parallel", "parallel")),
        name="cast_bf16",
    )(w)


def kernel(x, mem, rel_bias, ffn1_norm, ffn1_w_gate, ffn1_w_up, ffn1_w_down, mix_norm, w_in, conv_w, conv_b, conv_ln_g, conv_ln_b, moba_q_norm, moba_k_norm, diff_q_norm, diff_k_norm, diff_lambda, diff_subln, w_out, mem_norm_x, mem_norm_m, mem_wq, mem_wkv, mem_q_norm, mem_k_norm, mem_wo, ffn2_norm, ffn2_w_gate, ffn2_w_up, ffn2_w_down):
    b, s, d = x.shape
    depth = w_in.shape[0]
    t = b * s
    tab_flat = rel_bias.T.reshape(-1)
    strips = _bias_strips(tab_flat)
    ffn1_w = tuple(_cast_bf16(w) for w in (ffn1_w_gate, ffn1_w_up, ffn1_w_down))
    ffn2_w = tuple(_cast_bf16(w) for w in (ffn2_w_gate, ffn2_w_up, ffn2_w_down))
    w_in, w_out, mem_wq, mem_wkv, mem_wo = (_cast_bf16(w) for w in (w_in, w_out, mem_wq, mem_wkv, mem_wo))
    xt = x.reshape(t, d)
    for l in range(depth):
        xt = _ffn(xt, ffn1_norm[l], *ffn1_w, l)
        proj = _proj(xt, mix_norm[l], w_in, l).reshape(b, s, PROJ_WIDTH)
        y_conv = _conv(proj, conv_w[l], conv_b[l], conv_ln_g[l], conv_ln_b[l])
        y_moba = _moba(proj, strips, moba_q_norm[l], moba_k_norm[l])
        lam_init = 0.8 - 0.6 * math.exp(-0.3 * l)
        y_diff = _diff(proj, strips, diff_q_norm[l], diff_k_norm[l], diff_lambda[l], diff_subln[l], lam_init)
        xt = _wout(xt, y_conv.reshape(t, -1), y_moba.reshape(t, -1), y_diff.reshape(t, -1), w_out, l)
        kn, v = _memkv(mem, mem_norm_m[l], mem_wkv, mem_k_norm[l], l)
        xt = _memattn(xt.reshape(b, s, d), mem_norm_x[l], mem_wq, mem_q_norm[l], kn, v, mem_wo, l).reshape(t, d)
        xt = _ffn(xt, ffn2_norm[l], *ffn2_w, l)
    return xt.reshape(b, s, d)
```

```python
import functools
import math

import jax
import jax.numpy as jnp
from jax import lax
from jax.experimental import pallas as pl
from jax.experimental.pallas import tpu as pltpu

F32 = jnp.float32
BF16 = jnp.bfloat16

D_MODEL = 2048
D_FF = 5632
CONV_CH = 512
CONV_WIDTH = 31
CONV_PAD = 32
HEAD_DIM = 128
LANES = 128
BF16_SUBLANES = 16
MOBA_HEADS = 6
MOBA_BLOCK = 256
MOBA_TOPK = 3
DIFF_HEADS = 6
DIFF_QK_DIM = 64
MEM_HEADS = 4
N_BUCKETS = 32
MAX_DISTANCE = 128
N_BIAS_HEADS = MOBA_HEADS + DIFF_HEADS
NEG_INF = -1e30
TAKEN = -3e38
PROJ_WIDTH = 5632
COL_CONV_A = 0
COL_CONV_G = CONV_CH // HEAD_DIM
COL_MOBA_Q = 2 * CONV_CH // HEAD_DIM
COL_MOBA_K = COL_MOBA_Q + MOBA_HEADS
COL_MOBA_V = COL_MOBA_K + MOBA_HEADS
COL_DIFF_Q = COL_MOBA_V + MOBA_HEADS
COL_DIFF_K = COL_DIFF_Q + DIFF_HEADS
COL_DIFF_V = COL_DIFF_K + DIFF_HEADS

ATT_TILE = 256
assert ATT_TILE == MOBA_BLOCK and ATT_TILE >= MAX_DISTANCE
LOG2E = 1.4426950408889634
VMEM_LIMIT = 56 * 1024 * 1024


def _params(semantics, vmem=VMEM_LIMIT):
    return pltpu.CompilerParams(dimension_semantics=semantics, vmem_limit_bytes=vmem)


def _rms(x, gain, eps=1e-6):
    return x * lax.rsqrt(jnp.mean(x * x, axis=-1, keepdims=True) + eps) * gain


def _dot(a, b):
    return jnp.dot(a, b, preferred_element_type=F32)


def _dot_nt(a, b, precision=None):
    return lax.dot_general(a, b, (((1,), (1,)), ((), ())), precision=precision, preferred_element_type=F32)


def _ffn_kernel(n_side, x_ref, g_ref, wg_ref, wu_ref, wd_ref, *refs):
    side_in, o_ref, side_out, h_ref = refs[:n_side], refs[n_side], refs[n_side + 1:2 * n_side + 1], refs[-1]
    k = pl.program_id(1)

    @pl.when(k == 0)
    def _():
        x = x_ref[...]
        h_ref[...] = _rms(x, g_ref[...]).astype(BF16)
        o_ref[...] = x

    h = h_ref[...]
    gate = _dot(h, wg_ref[...])
    up = _dot(h, wu_ref[...])
    act = (0.5 * gate / (1.0 + jnp.exp(-gate)) * up).astype(BF16)
    o_ref[...] += _dot(act, wd_ref[...])
    for w_ref, wb_ref in zip(side_in, side_out):
        wb_ref[...] = w_ref[...].astype(BF16)


def _side_block(shape, gi, gk):
    k, n = shape
    if k % gi == 0 and n % gk == 0 and (k // gi) % BF16_SUBLANES == 0 and (n // gk) % LANES == 0:
        return (k // gi, n // gk), (lambda i, kk: (i, kk))
    assert k % gk == 0 and n % gi == 0 and (k // gk) % BF16_SUBLANES == 0 and (n // gi) % LANES == 0, shape
    return (k // gk, n // gi), (lambda i, kk: (kk, i))


def _ffn(x, gain, wg, wu, wd, l, side=(), tm=1024, tf=512):
    t, d = x.shape
    f = wg.shape[2]
    gi, gk = t // tm, f // tf
    side_in_specs, side_out_specs, side_shapes = [], [], []
    for w, ls in side:
        blk, order = _side_block(w.shape[1:], gi, gk)
        side_in_specs.append(pl.BlockSpec((None, *blk), lambda i, k, ls=ls, order=order: (ls, *order(i, k))))
        side_out_specs.append(pl.BlockSpec((None, *blk), lambda i, k, order=order: (0, *order(i, k))))
        side_shapes.append(jax.ShapeDtypeStruct((1, *w.shape[1:]), BF16))
    out, *cast = pl.pallas_call(
        functools.partial(_ffn_kernel, len(side)),
        out_shape=[jax.ShapeDtypeStruct((t, d), F32), *side_shapes],
        grid=(gi, gk),
        in_specs=[
            pl.BlockSpec((tm, d), lambda i, k: (i, 0), pipeline_mode=pl.Buffered(1)),
            pl.BlockSpec((1, d), lambda i, k: (0, 0)),
            pl.BlockSpec((None, d, tf), lambda i, k: (l, 0, k)),
            pl.BlockSpec((None, d, tf), lambda i, k: (l, 0, k)),
            pl.BlockSpec((None, tf, d), lambda i, k: (l, k, 0)),
            *side_in_specs,
        ],
        out_specs=[pl.BlockSpec((tm, d), lambda i, k: (i, 0)), *side_out_specs],
        scratch_shapes=[pltpu.VMEM((tm, d), BF16)],
        compiler_params=_params(("parallel", "arbitrary")),
        name="ffn",
    )(x, gain.reshape(1, d), wg, wu, wd, *(w for w, _ in side))
    return out, cast


def _proj_kernel(x_ref, g_ref, w_ref, o_ref, h_ref):
    @pl.when(pl.program_id(1) == 0)
    def _():
        h_ref[...] = _rms(x_ref[...], g_ref[...]).astype(BF16)

    o_ref[...] = _dot(h_ref[...], w_ref[...]).astype(o_ref.dtype)


def _proj(x, gain, w, l, tm=2048, tn=512):
    t, d = x.shape
    n = w.shape[2]
    return pl.pallas_call(
        _proj_kernel,
        out_shape=jax.ShapeDtypeStruct((t, n), BF16),
        grid=(t // tm, n // tn),
        in_specs=[
            pl.BlockSpec((tm, d), lambda i, j: (i, 0)),
            pl.BlockSpec((1, d), lambda i, j: (0, 0)),
            pl.BlockSpec((None, d, tn), lambda i, j: (l, 0, j)),
        ],
        out_specs=pl.BlockSpec((tm, tn), lambda i, j: (i, j)),
        scratch_shapes=[pltpu.VMEM((tm, d), BF16)],
        compiler_params=_params(("parallel", "arbitrary")),
        name="proj",
    )(x, gain.reshape(1, d), w)


def _bias_kernel(tab_ref, o_ref):
    h = pl.program_id(0)
    far = tab_ref[h * N_BUCKETS + N_BUCKETS - 1]
    shape = (ATT_TILE, 2 * ATT_TILE)
    dist = ATT_TILE + lax.broadcasted_iota(jnp.int32, shape, 0) - lax.broadcasted_iota(jnp.int32, shape, 1)
    dist = jnp.maximum(dist, 0)
    max_exact = N_BUCKETS // 2
    log_ratio = jnp.log(jnp.maximum(dist, 1).astype(F32) / max_exact) / math.log(MAX_DISTANCE / max_exact)
    large = jnp.minimum(max_exact + (log_ratio * (N_BUCKETS - max_exact)).astype(jnp.int32), N_BUCKETS - 1)
    bucket = jnp.where(dist < max_exact, dist, large)
    out = jnp.zeros(shape, F32)
    for b in range(N_BUCKETS):
        out = jnp.where(bucket == b, (tab_ref[h * N_BUCKETS + b] - far) * LOG2E, out)
    o_ref[0] = out


def _bias_strips(tab_flat):
    return pl.pallas_call(
        _bias_kernel,
        out_shape=jax.ShapeDtypeStruct((N_BIAS_HEADS, ATT_TILE, 2 * ATT_TILE), F32),
        grid=(N_BIAS_HEADS,),
        in_specs=[pl.BlockSpec(memory_space=pltpu.SMEM)],
        out_specs=pl.BlockSpec((1, ATT_TILE, 2 * ATT_TILE), lambda h: (h, 0, 0)),
        compiler_params=_params(("arbitrary",)),
        name="bias_strips",
    )(tab_flat)


CONV_ROWS = 32
CONV_UNROLL = 4
SUBLANES = 8
assert CONV_PAD % SUBLANES == 0 and CONV_PAD >= SUBLANES * -(-CONV_WIDTH // SUBLANES)


def _conv_kernel(a_ref, g_ref, w_ref, cb_ref, lg_ref, lb_ref, o_ref, u_ref):
    s = a_ref.shape[1]
    u_ref[0:CONV_PAD, :] = jnp.zeros((CONV_PAD, CONV_CH), F32)

    def glu(r, carry):
        rows = pl.ds(pl.multiple_of(r * CONV_ROWS, CONV_ROWS), CONV_ROWS)
        a = a_ref[0, rows, :].astype(F32)
        g = g_ref[0, rows, :].astype(F32)
        u_ref[pl.ds(pl.multiple_of(CONV_PAD + r * CONV_ROWS, CONV_ROWS), CONV_ROWS), :] = a / (1.0 + jnp.exp(-g))
        return carry

    lax.fori_loop(0, s // CONV_ROWS, glu, 0)

    win_rows = CONV_ROWS + CONV_PAD
    n_groups = CONV_CH // LANES

    def conv(r, carry):
        base = pl.multiple_of(r * CONV_ROWS, CONV_ROWS)
        accs = []
        for c in range(n_groups):
            lanes = slice(c * LANES, (c + 1) * LANES)
            win = u_ref[pl.ds(base, win_rows), lanes]
            acc = jnp.zeros((CONV_ROWS, LANES), F32) + cb_ref[:, lanes]
            for b in range(SUBLANES):
                shifted = pltpu.roll(win, b, 0) if b else win
                for a in range(-(-CONV_WIDTH // SUBLANES)):
                    d = SUBLANES * a + b
                    if d < CONV_WIDTH:
                        k0 = CONV_PAD - SUBLANES * a
                        tap = CONV_WIDTH - 1 - d
                        acc = acc + shifted[k0:k0 + CONV_ROWS, :] * w_ref[tap:tap + 1, lanes]
            accs.append(acc)
        mu = sum(jnp.sum(acc, axis=-1, keepdims=True) for acc in accs) / float(CONV_CH)
        cens = [acc - mu for acc in accs]
        var = sum(jnp.sum(cen * cen, axis=-1, keepdims=True) for cen in cens) / float(CONV_CH)
        inv = lax.rsqrt(var + 1e-5)
        for c, cen in enumerate(cens):
            lanes = slice(c * LANES, (c + 1) * LANES)
            y = cen * inv * lg_ref[:, lanes] + lb_ref[:, lanes]
            o_ref[0, pl.ds(base, CONV_ROWS), lanes] = (y / (1.0 + jnp.exp(-y))).astype(o_ref.dtype)
        return carry

    lax.fori_loop(0, s // CONV_ROWS, conv, 0, unroll=CONV_UNROLL)


def _conv(proj, w, cb, lg, lb):
    b, s, _ = proj.shape
    vec = lambda v: v.reshape(1, CONV_CH)
    return pl.pallas_call(
        _conv_kernel,
        out_shape=jax.ShapeDtypeStruct((b, s, CONV_CH), BF16),
        grid=(b,),
        in_specs=[
            pl.BlockSpec((1, s, CONV_CH), lambda i: (i, 0, 0)),
            pl.BlockSpec((1, s, CONV_CH), lambda i: (i, 0, 1)),
            pl.BlockSpec((CONV_WIDTH, CONV_CH), lambda i: (0, 0)),
            pl.BlockSpec((1, CONV_CH), lambda i: (0, 0)),
            pl.BlockSpec((1, CONV_CH), lambda i: (0, 0)),
            pl.BlockSpec((1, CONV_CH), lambda i: (0, 0)),
        ],
        out_specs=pl.BlockSpec((1, s, CONV_CH), lambda i: (i, 0, 0)),
        scratch_shapes=[pltpu.VMEM((s + CONV_PAD, CONV_CH), F32)],
        compiler_params=_params(("parallel",)),
        name="conv",
    )(proj, proj, w, vec(cb), vec(lg), vec(lb))


def _tile_rows(j):
    return slice(j * ATT_TILE, (j + 1) * ATT_TILE)


def _lane_fold(op, x):
    out = x[:, :LANES]
    for c in range(1, x.shape[1] // LANES):
        out = op(out, x[:, c * LANES:(c + 1) * LANES])
    return out


def _attend_tile(i, q, k_ref, v_ref, strip_ref, sel_add, s_ref, p_ref):
    mask = (lax.broadcasted_iota(jnp.int32, (ATT_TILE, ATT_TILE), 0)
            >= lax.broadcasted_iota(jnp.int32, (ATT_TILE, ATT_TILE), 1))
    m_acc = None
    for j in range(i + 1):
        s = _dot_nt(q, k_ref[_tile_rows(j), :])
        if j == i:
            s = jnp.where(mask, s + strip_ref[0, :, ATT_TILE:], NEG_INF)
        elif j == i - 1:
            s = s + strip_ref[0, :, :ATT_TILE]
        if sel_add is not None and j < i:
            s = s + sel_add[:, j:j + 1]
        s_ref[:, _tile_rows(j)] = s
        t = _lane_fold(jnp.maximum, s)
        m_acc = t if m_acc is None else jnp.maximum(m_acc, t)
    m = jnp.max(m_acc, axis=-1, keepdims=True)
    l_acc = None
    for j in range(i + 1):
        p = jnp.exp2(s_ref[:, _tile_rows(j)] - m)
        p_ref[:, _tile_rows(j)] = p.astype(BF16)
        t = _lane_fold(jnp.add, p)
        l_acc = t if l_acc is None else l_acc + t
    n = (i + 1) * ATT_TILE
    return _dot(p_ref[:, :n], v_ref[:n, :]), jnp.sum(l_acc, axis=-1, keepdims=True)


def _moba_kernel(q_ref, k_ref, v_ref, qg_ref, kg_ref, strip_ref, o_ref, kn_ref, kmean_ref, s_ref, p_ref):
    nb = k_ref.shape[1] // MOBA_BLOCK
    for j in range(nb):
        rows = _tile_rows(j)
        kn = _rms(k_ref[0, rows, :].astype(F32), kg_ref[...])
        kn_ref[rows, :] = kn.astype(BF16)
        kmean_ref[j:j + 1, :] = jnp.sum(kn, axis=0, keepdims=True) / float(MOBA_BLOCK)

    eye = (lax.broadcasted_iota(jnp.int32, (MOBA_BLOCK, MOBA_BLOCK), 0)
           == lax.broadcasted_iota(jnp.int32, (MOBA_BLOCK, MOBA_BLOCK), 1)).astype(F32).astype(BF16)
    n_sel = min(MOBA_TOPK, nb)
    for i in range(nb):
        qn = _rms(q_ref[0, _tile_rows(i), :].astype(F32), qg_ref[...])
        sel_add = None
        if i > n_sel:
            gate = _dot_nt(kmean_ref[...], qn, precision=lax.Precision.HIGHEST)
            blk = lax.broadcasted_iota(jnp.int32, gate.shape, 0)
            gate = jnp.where(blk < i, gate, TAKEN)
            sel = jnp.zeros(gate.shape, F32)
            for _ in range(n_sel):
                top = jnp.max(gate, axis=0, keepdims=True)
                first = jnp.min(jnp.where(gate == top, blk, nb), axis=0, keepdims=True)
                pick = blk == first
                sel = jnp.where(pick, 1.0, sel)
                gate = jnp.where(pick, TAKEN, gate)
            sel_t = _dot_nt(eye, sel.astype(BF16))
            sel_add = (sel_t - 1.0) * -NEG_INF
        q = (qn * (HEAD_DIM ** -0.5 * LOG2E)).astype(BF16)
        acc, l = _attend_tile(i, q, kn_ref, v_ref.at[0], strip_ref, sel_add, s_ref, p_ref)
        o_ref[0, _tile_rows(i), :] = (acc / l).astype(o_ref.dtype)


def _moba(proj, strips, q_gain, k_gain):
    b, s, _ = proj.shape
    nb = s // MOBA_BLOCK
    return pl.pallas_call(
        _moba_kernel,
        out_shape=jax.ShapeDtypeStruct((b, s, MOBA_HEADS * HEAD_DIM), BF16),
        grid=(b, MOBA_HEADS),
        in_specs=[
            pl.BlockSpec((1, s, HEAD_DIM), lambda bi, h: (bi, 0, COL_MOBA_Q + h)),
            pl.BlockSpec((1, s, HEAD_DIM), lambda bi, h: (bi, 0, COL_MOBA_K + h)),
            pl.BlockSpec((1, s, HEAD_DIM), lambda bi, h: (bi, 0, COL_MOBA_V + h)),
            pl.BlockSpec((1, HEAD_DIM), lambda bi, h: (0, 0)),
            pl.BlockSpec((1, HEAD_DIM), lambda bi, h: (0, 0)),
            pl.BlockSpec((1, ATT_TILE, 2 * ATT_TILE), lambda bi, h: (h, 0, 0)),
        ],
        out_specs=pl.BlockSpec((1, s, HEAD_DIM), lambda bi, h: (bi, 0, h)),
        scratch_shapes=[
            pltpu.VMEM((s, HEAD_DIM), BF16),
            pltpu.VMEM((nb, HEAD_DIM), F32),
            pltpu.VMEM((ATT_TILE, s), F32),
            pltpu.VMEM((ATT_TILE, s), BF16),
        ],
        compiler_params=_params(("parallel", "parallel")),
        name="moba",
    )(proj, proj, proj, q_gain.reshape(1, HEAD_DIM), k_gain.reshape(1, HEAD_DIM), strips)


def _diff_kernel(lam_init, q_ref, k_ref, v_ref, qg_ref, kg_ref, lam_ref, sg_ref, strip_ref, o_ref,
                 k0_ref, k1_ref, s0_ref, s1_ref, p0_ref, p1_ref):
    nt = k_ref.shape[1] // ATT_TILE
    dq = DIFF_QK_DIM
    for j in range(nt):
        rows = _tile_rows(j)
        k = k_ref[0, rows, :].astype(F32)
        k0_ref[rows, :] = _rms(k[:, :dq], kg_ref[...]).astype(BF16)
        k1_ref[rows, :] = _rms(k[:, dq:], kg_ref[...]).astype(BF16)

    lp = lam_ref[...]
    lam = (jnp.exp(jnp.sum(lp[0:1] * lp[1:2], axis=-1, keepdims=True))
           - jnp.exp(jnp.sum(lp[2:3] * lp[3:4], axis=-1, keepdims=True)) + lam_init)
    scale = dq ** -0.5 * LOG2E
    for i in range(nt):
        q = q_ref[0, _tile_rows(i), :].astype(F32)
        q0 = (_rms(q[:, :dq], qg_ref[...]) * scale).astype(BF16)
        q1 = (_rms(q[:, dq:], qg_ref[...]) * scale).astype(BF16)
        a0, l0 = _attend_tile(i, q0, k0_ref, v_ref.at[0], strip_ref, None, s0_ref, p0_ref)
        a1, l1 = _attend_tile(i, q1, k1_ref, v_ref.at[0], strip_ref, None, s1_ref, p1_ref)
        o = a0 / l0 - lam * (a1 / l1)
        o_ref[0, _tile_rows(i), :] = (_rms(o, sg_ref[...]) * (1.0 - lam_init)).astype(o_ref.dtype)


def _diff(proj, strips, q_gain, k_gain, lam_params, subln, lam_init):
    b, s, _ = proj.shape
    col = lambda v: (1, v.shape[-1])
    return pl.pallas_call(
        functools.partial(_diff_kernel, lam_init),
        out_shape=jax.ShapeDtypeStruct((b, s, DIFF_HEADS * HEAD_DIM), BF16),
        grid=(b, DIFF_HEADS),
        in_specs=[
            pl.BlockSpec((1, s, HEAD_DIM), lambda bi, h: (bi, 0, COL_DIFF_Q + h)),
            pl.BlockSpec((1, s, HEAD_DIM), lambda bi, h: (bi, 0, COL_DIFF_K + h)),
            pl.BlockSpec((1, s, HEAD_DIM), lambda bi, h: (bi, 0, COL_DIFF_V + h)),
            pl.BlockSpec((1, DIFF_QK_DIM), lambda bi, h: (0, 0)),
            pl.BlockSpec((1, DIFF_QK_DIM), lambda bi, h: (0, 0)),
            pl.BlockSpec((4, DIFF_QK_DIM), lambda bi, h: (0, 0)),
            pl.BlockSpec((1, HEAD_DIM), lambda bi, h: (0, 0)),
            pl.BlockSpec((1, ATT_TILE, 2 * ATT_TILE), lambda bi, h: (MOBA_HEADS + h, 0, 0)),
        ],
        out_specs=pl.BlockSpec((1, s, HEAD_DIM), lambda bi, h: (bi, 0, h)),
        scratch_shapes=[
            pltpu.VMEM((s, DIFF_QK_DIM), BF16),
            pltpu.VMEM((s, DIFF_QK_DIM), BF16),
            pltpu.VMEM((ATT_TILE, s), F32),
            pltpu.VMEM((ATT_TILE, s), F32),
            pltpu.VMEM((ATT_TILE, s), BF16),
            pltpu.VMEM((ATT_TILE, s), BF16),
        ],
        compiler_params=_params(("parallel", "parallel")),
        name="diff",
    )(proj, proj, proj, q_gain.reshape(col(q_gain)), k_gain.reshape(col(k_gain)), lam_params,
      subln.reshape(col(subln)), strips)


def _wout_kernel(x_ref, yc_ref, ym_ref, yd_ref, w_ref, o_ref):
    c0 = CONV_CH
    c1 = c0 + MOBA_HEADS * HEAD_DIM
    o_ref[...] = (x_ref[...] + _dot(yc_ref[...], w_ref[0:c0, :]) + _dot(ym_ref[...], w_ref[c0:c1, :])
                  + _dot(yd_ref[...], w_ref[c1:, :]))


def _wout(x, yc, ym, yd, w, l, tm=1024, tn=1024):
    t, d = x.shape
    return pl.pallas_call(
        _wout_kernel,
        out_shape=jax.ShapeDtypeStruct((t, d), F32),
        grid=(t // tm, d // tn),
        in_specs=[
            pl.BlockSpec((tm, tn), lambda i, j: (i, j)),
            pl.BlockSpec((tm, yc.shape[1]), lambda i, j: (i, 0)),
            pl.BlockSpec((tm, ym.shape[1]), lambda i, j: (i, 0)),
            pl.BlockSpec((tm, yd.shape[1]), lambda i, j: (i, 0)),
            pl.BlockSpec((None, w.shape[1], tn), lambda i, j: (l, 0, j)),
        ],
        out_specs=pl.BlockSpec((tm, tn), lambda i, j: (i, j)),
        compiler_params=_params(("parallel", "parallel")),
        name="wout",
    )(x, yc, ym, yd, w)


def _memkv_kernel(m_ref, g_ref, w_ref, kg_ref, k_ref, v_ref):
    hm = _rms(m_ref[0], g_ref[...]).astype(BF16)
    kv = _dot(hm, w_ref[...])
    width = MEM_HEADS * HEAD_DIM
    for h in range(MEM_HEADS):
        cols = slice(h * HEAD_DIM, (h + 1) * HEAD_DIM)
        k_ref[0, :, cols] = _rms(kv[:, cols], kg_ref[...]).astype(BF16)
    v_ref[0] = kv[:, width:].astype(BF16)


def _memkv(mem, gain, wkv, k_gain, l):
    b, m, d = mem.shape
    width = MEM_HEADS * HEAD_DIM
    out = jax.ShapeDtypeStruct((b, m, width), BF16)
    return pl.pallas_call(
        _memkv_kernel,
        out_shape=(out, out),
        grid=(b,),
        in_specs=[
            pl.BlockSpec((1, m, d), lambda i: (i, 0, 0)),
            pl.BlockSpec((1, d), lambda i: (0, 0)),
            pl.BlockSpec((None, d, 2 * width), lambda i: (l, 0, 0)),
            pl.BlockSpec((1, HEAD_DIM), lambda i: (0, 0)),
        ],
        out_specs=(pl.BlockSpec((1, m, width), lambda i: (i, 0, 0)), pl.BlockSpec((1, m, width), lambda i: (i, 0, 0))),
        compiler_params=_params(("parallel",)),
        name="memkv",
    )(mem, gain.reshape(1, d), wkv, k_gain.reshape(1, HEAD_DIM))


def _memattn_kernel(x_ref, g_ref, wq_ref, qg_ref, k_ref, v_ref, wo_ref, o_ref, ob_ref):
    x = x_ref[0]
    hx = _rms(x, g_ref[...]).astype(BF16)
    q = _dot(hx, wq_ref[...])
    for h in range(MEM_HEADS):
        cols = slice(h * HEAD_DIM, (h + 1) * HEAD_DIM)
        qh = (_rms(q[:, cols], qg_ref[...]) * HEAD_DIM ** -0.5).astype(BF16)
        s = _dot_nt(qh, k_ref[0, :, cols])
        p = jnp.exp(s - jnp.max(s, axis=-1, keepdims=True))
        oh = _dot(p.astype(BF16), v_ref[0, :, cols]) / jnp.sum(p, axis=-1, keepdims=True)
        ob_ref[:, cols] = oh.astype(BF16)
    o_ref[0] = x + _dot(ob_ref[...], wo_ref[...])


def _memattn(x, gain, wq, q_gain, kn, v, wo, l, tm=512):
    b, s, d = x.shape
    m = kn.shape[1]
    width = MEM_HEADS * HEAD_DIM
    return pl.pallas_call(
        _memattn_kernel,
        out_shape=jax.ShapeDtypeStruct((b, s, d), F32),
        grid=(b, s // tm),
        in_specs=[
            pl.BlockSpec((1, tm, d), lambda bi, i: (bi, i, 0)),
            pl.BlockSpec((1, d), lambda bi, i: (0, 0)),
            pl.BlockSpec((None, d, width), lambda bi, i: (l, 0, 0)),
            pl.BlockSpec((1, HEAD_DIM), lambda bi, i: (0, 0)),
            pl.BlockSpec((1, m, width), lambda bi, i: (bi, 0, 0)),
            pl.BlockSpec((1, m, width), lambda bi, i: (bi, 0, 0)),
            pl.BlockSpec((None, width, d), lambda bi, i: (l, 0, 0)),
        ],
        out_specs=pl.BlockSpec((1, tm, d), lambda bi, i: (bi, i, 0)),
        scratch_shapes=[pltpu.VMEM((tm, width), BF16)],
        compiler_params=_params(("parallel", "parallel")),
        name="memattn",
    )(x, gain.reshape(1, d), wq, q_gain.reshape(1, HEAD_DIM), kn, v, wo)


CAST_BLOCK_BYTES = 6 * 1024 * 1024


def _cast_kernel(w_ref, o_ref):
    o_ref[...] = w_ref[...].astype(o_ref.dtype)


def _cast_bf16(w, layer=None):
    depth, k, n = w.shape
    first, count = (0, depth) if layer is None else (layer, 1)
    bk = max(r for r in range(BF16_SUBLANES, k + 1, BF16_SUBLANES) if k % r == 0 and r * n * 4 <= CAST_BLOCK_BYTES)
    return pl.pallas_call(
        _cast_kernel,
        out_shape=jax.ShapeDtypeStruct((count, k, n), BF16),
        grid=(count, k // bk),
        in_specs=[pl.BlockSpec((None, bk, n), lambda l, i: (first + l, i, 0))],
        out_specs=pl.BlockSpec((None, bk, n), lambda l, i: (l, i, 0)),
        compiler_params=_params(("parallel", "parallel")),
        name="cast_bf16",
    )(w)


def kernel(x, mem, rel_bias, ffn1_norm, ffn1_w_gate, ffn1_w_up, ffn1_w_down, mix_norm, w_in, conv_w, conv_b, conv_ln_g, conv_ln_b, moba_q_norm, moba_k_norm, diff_q_norm, diff_k_norm, diff_lambda, diff_subln, w_out, mem_norm_x, mem_norm_m, mem_wq, mem_wkv, mem_q_norm, mem_k_norm, mem_wo, ffn2_norm, ffn2_w_gate, ffn2_w_up, ffn2_w_down):
    b, s, d = x.shape
    depth = w_in.shape[0]
    t = b * s
    tab_flat = rel_bias.T.reshape(-1)
    strips = _bias_strips(tab_flat)
    ffn1_f32 = (ffn1_w_gate, ffn1_w_up, ffn1_w_down)
    ffn2_f32 = (ffn2_w_gate, ffn2_w_up, ffn2_w_down)
    ffn_w = [_cast_bf16(w, 0) for w in ffn1_f32]
    w_out, mem_wq, mem_wkv, mem_wo = (_cast_bf16(w) for w in (w_out, mem_wq, mem_wkv, mem_wo))
    xt = x.reshape(t, d)
    for l in range(depth):
        xt, (*ffn_w, w_in_l) = _ffn(xt, ffn1_norm[l], *ffn_w, 0, side=[(w, l) for w in (*ffn2_f32, w_in)])
        proj = _proj(xt, mix_norm[l], w_in_l, 0).reshape(b, s, PROJ_WIDTH)
        y_conv = _conv(proj, conv_w[l], conv_b[l], conv_ln_g[l], conv_ln_b[l])
        y_moba = _moba(proj, strips, moba_q_norm[l], moba_k_norm[l])
        lam_init = 0.8 - 0.6 * math.exp(-0.3 * l)
        y_diff = _diff(proj, strips, diff_q_norm[l], diff_k_norm[l], diff_lambda[l], diff_subln[l], lam_init)
        xt = _wout(xt, y_conv.reshape(t, -1), y_moba.reshape(t, -1), y_diff.reshape(t, -1), w_out, l)
        kn, v = _memkv(mem, mem_norm_m[l], mem_wkv, mem_k_norm[l], l)
        xt = _memattn(xt.reshape(b, s, d), mem_norm_x[l], mem_wq, mem_q_norm[l], kn, v, mem_wo, l).reshape(t, d)
        nxt = [(w, l + 1) for w in ffn1_f32] if l + 1 < depth else []
        xt, ffn_w = _ffn(xt, ffn2_norm[l], *ffn_w, 0, side=nxt)
    return xt.reshape(b, s, d)
```

```python
import functools
import math

import jax
import jax.numpy as jnp
from jax import lax
from jax.experimental import pallas as pl
from jax.experimental.pallas import tpu as pltpu

F32 = jnp.float32
BF16 = jnp.bfloat16

D_MODEL = 2048
D_FF = 5632
CONV_CH = 512
CONV_WIDTH = 31
CONV_PAD = 32
HEAD_DIM = 128
LANES = 128
BF16_SUBLANES = 16
MOBA_HEADS = 6
MOBA_BLOCK = 256
MOBA_TOPK = 3
DIFF_HEADS = 6
DIFF_QK_DIM = 64
MEM_HEADS = 4
N_BUCKETS = 32
MAX_DISTANCE = 128
N_BIAS_HEADS = MOBA_HEADS + DIFF_HEADS
NEG_INF = -1e30
TAKEN = -3e38
PROJ_WIDTH = 5632
COL_CONV_A = 0
COL_CONV_G = CONV_CH // HEAD_DIM
COL_MOBA_Q = 2 * CONV_CH // HEAD_DIM
COL_MOBA_K = COL_MOBA_Q + MOBA_HEADS
COL_MOBA_V = COL_MOBA_K + MOBA_HEADS
COL_DIFF_Q = COL_MOBA_V + MOBA_HEADS
COL_DIFF_K = COL_DIFF_Q + DIFF_HEADS
COL_DIFF_V = COL_DIFF_K + DIFF_HEADS

ATT_TILE = 256
assert ATT_TILE == MOBA_BLOCK and ATT_TILE >= MAX_DISTANCE
LOG2E = 1.4426950408889634
VMEM_LIMIT = 56 * 1024 * 1024


def _params(semantics, vmem=VMEM_LIMIT):
    return pltpu.CompilerParams(dimension_semantics=semantics, vmem_limit_bytes=vmem)


def _rms(x, gain, eps=1e-6):
    return x * lax.rsqrt(jnp.mean(x * x, axis=-1, keepdims=True) + eps) * gain


def _dot(a, b):
    return jnp.dot(a, b, preferred_element_type=F32)


def _dot_nt(a, b, precision=None):
    return lax.dot_general(a, b, (((1,), (1,)), ((), ())), precision=precision, preferred_element_type=F32)


def _ffn_kernel(n_side, x_ref, g_ref, wg_ref, wu_ref, wd_ref, *refs):
    side_in, o_ref, side_out, h_ref = refs[:n_side], refs[n_side], refs[n_side + 1:2 * n_side + 1], refs[-1]
    k = pl.program_id(1)

    @pl.when(k == 0)
    def _():
        x = x_ref[...]
        h_ref[...] = _rms(x, g_ref[...]).astype(BF16)
        o_ref[...] = x

    h = h_ref[...]
    gate = _dot(h, wg_ref[...])
    up = _dot(h, wu_ref[...])
    act = (0.5 * gate / (1.0 + jnp.exp(-gate)) * up).astype(BF16)
    o_ref[...] += _dot(act, wd_ref[...])
    for w_ref, wb_ref in zip(side_in, side_out):
        wb_ref[...] = w_ref[...].astype(BF16)


def _side_block(shape, gi, gk):
    k, n = shape
    if k % gi == 0 and n % gk == 0 and (k // gi) % BF16_SUBLANES == 0 and (n // gk) % LANES == 0:
        return (k // gi, n // gk), (lambda i, kk: (i, kk))
    assert k % gk == 0 and n % gi == 0 and (k // gk) % BF16_SUBLANES == 0 and (n // gi) % LANES == 0, shape
    return (k // gk, n // gi), (lambda i, kk: (kk, i))


def _ffn(x, gain, wg, wu, wd, l, side=(), tm=1024, tf=512):
    t, d = x.shape
    f = wg.shape[2]
    gi, gk = t // tm, f // tf
    side_in_specs, side_out_specs, side_shapes = [], [], []
    for w, ls in side:
        blk, order = _side_block(w.shape[1:], gi, gk)
        side_in_specs.append(pl.BlockSpec((None, *blk), lambda i, k, ls=ls, order=order: (ls, *order(i, k))))
        side_out_specs.append(pl.BlockSpec((None, *blk), lambda i, k, order=order: (0, *order(i, k))))
        side_shapes.append(jax.ShapeDtypeStruct((1, *w.shape[1:]), BF16))
    out, *cast = pl.pallas_call(
        functools.partial(_ffn_kernel, len(side)),
        out_shape=[jax.ShapeDtypeStruct((t, d), F32), *side_shapes],
        grid=(gi, gk),
        in_specs=[
            pl.BlockSpec((tm, d), lambda i, k: (i, 0), pipeline_mode=pl.Buffered(1)),
            pl.BlockSpec((1, d), lambda i, k: (0, 0)),
            pl.BlockSpec((None, d, tf), lambda i, k: (l, 0, k)),
            pl.BlockSpec((None, d, tf), lambda i, k: (l, 0, k)),
            pl.BlockSpec((None, tf, d), lambda i, k: (l, k, 0)),
            *side_in_specs,
        ],
        out_specs=[pl.BlockSpec((tm, d), lambda i, k: (i, 0)), *side_out_specs],
        scratch_shapes=[pltpu.VMEM((tm, d), BF16)],
        compiler_params=_params(("parallel", "arbitrary")),
        name="ffn",
    )(x, gain.reshape(1, d), wg, wu, wd, *(w for w, _ in side))
    return out, cast


def _proj_kernel(x_ref, g_ref, w_ref, o_ref, h_ref):
    @pl.when(pl.program_id(1) == 0)
    def _():
        h_ref[...] = _rms(x_ref[...], g_ref[...]).astype(BF16)

    o_ref[...] = _dot(h_ref[...], w_ref[...]).astype(o_ref.dtype)


def _proj(x, gain, w, l, tm=2048, tn=512):
    t, d = x.shape
    n = w.shape[2]
    return pl.pallas_call(
        _proj_kernel,
        out_shape=jax.ShapeDtypeStruct((t, n), BF16),
        grid=(t // tm, n // tn),
        in_specs=[
            pl.BlockSpec((tm, d), lambda i, j: (i, 0)),
            pl.BlockSpec((1, d), lambda i, j: (0, 0)),
            pl.BlockSpec((None, d, tn), lambda i, j: (l, 0, j)),
        ],
        out_specs=pl.BlockSpec((tm, tn), lambda i, j: (i, j)),
        scratch_shapes=[pltpu.VMEM((tm, d), BF16)],
        compiler_params=_params(("parallel", "arbitrary")),
        name="proj",
    )(x, gain.reshape(1, d), w)


def _bias_kernel(tab_ref, o_ref):
    h = pl.program_id(0)
    far = tab_ref[h * N_BUCKETS + N_BUCKETS - 1]
    shape = (ATT_TILE, 2 * ATT_TILE)
    dist = ATT_TILE + lax.broadcasted_iota(jnp.int32, shape, 0) - lax.broadcasted_iota(jnp.int32, shape, 1)
    dist = jnp.maximum(dist, 0)
    max_exact = N_BUCKETS // 2
    log_ratio = jnp.log(jnp.maximum(dist, 1).astype(F32) / max_exact) / math.log(MAX_DISTANCE / max_exact)
    large = jnp.minimum(max_exact + (log_ratio * (N_BUCKETS - max_exact)).astype(jnp.int32), N_BUCKETS - 1)
    bucket = jnp.where(dist < max_exact, dist, large)
    out = jnp.zeros(shape, F32)
    for b in range(N_BUCKETS):
        out = jnp.where(bucket == b, (tab_ref[h * N_BUCKETS + b] - far) * LOG2E, out)
    o_ref[0] = out


def _bias_strips(tab_flat):
    return pl.pallas_call(
        _bias_kernel,
        out_shape=jax.ShapeDtypeStruct((N_BIAS_HEADS, ATT_TILE, 2 * ATT_TILE), F32),
        grid=(N_BIAS_HEADS,),
        in_specs=[pl.BlockSpec(memory_space=pltpu.SMEM)],
        out_specs=pl.BlockSpec((1, ATT_TILE, 2 * ATT_TILE), lambda h: (h, 0, 0)),
        compiler_params=_params(("arbitrary",)),
        name="bias_strips",
    )(tab_flat)


CONV_ROWS = 32
CONV_UNROLL = 4
SUBLANES = 8
assert CONV_PAD % SUBLANES == 0 and CONV_PAD >= SUBLANES * -(-CONV_WIDTH // SUBLANES)


def _conv_kernel(a_ref, g_ref, w_ref, cb_ref, lg_ref, lb_ref, o_ref, u_ref):
    s = a_ref.shape[1]
    u_ref[0:CONV_PAD, :] = jnp.zeros((CONV_PAD, CONV_CH), F32)

    def glu(r, carry):
        rows = pl.ds(pl.multiple_of(r * CONV_ROWS, CONV_ROWS), CONV_ROWS)
        a = a_ref[0, rows, :].astype(F32)
        g = g_ref[0, rows, :].astype(F32)
        u_ref[pl.ds(pl.multiple_of(CONV_PAD + r * CONV_ROWS, CONV_ROWS), CONV_ROWS), :] = a / (1.0 + jnp.exp(-g))
        return carry

    lax.fori_loop(0, s // CONV_ROWS, glu, 0)

    win_rows = CONV_ROWS + CONV_PAD
    n_groups = CONV_CH // LANES

    def conv(r, carry):
        base = pl.multiple_of(r * CONV_ROWS, CONV_ROWS)
        accs = []
        for c in range(n_groups):
            lanes = slice(c * LANES, (c + 1) * LANES)
            win = u_ref[pl.ds(base, win_rows), lanes]
            acc = jnp.zeros((CONV_ROWS, LANES), F32) + cb_ref[:, lanes]
            for b in range(SUBLANES):
                shifted = pltpu.roll(win, b, 0) if b else win
                for a in range(-(-CONV_WIDTH // SUBLANES)):
                    d = SUBLANES * a + b
                    if d < CONV_WIDTH:
                        k0 = CONV_PAD - SUBLANES * a
                        tap = CONV_WIDTH - 1 - d
                        acc = acc + shifted[k0:k0 + CONV_ROWS, :] * w_ref[tap:tap + 1, lanes]
            accs.append(acc)
        mu = sum(jnp.sum(acc, axis=-1, keepdims=True) for acc in accs) / float(CONV_CH)
        cens = [acc - mu for acc in accs]
        var = sum(jnp.sum(cen * cen, axis=-1, keepdims=True) for cen in cens) / float(CONV_CH)
        inv = lax.rsqrt(var + 1e-5)
        for c, cen in enumerate(cens):
            lanes = slice(c * LANES, (c + 1) * LANES)
            y = cen * inv * lg_ref[:, lanes] + lb_ref[:, lanes]
            o_ref[0, pl.ds(base, CONV_ROWS), lanes] = (y / (1.0 + jnp.exp(-y))).astype(o_ref.dtype)
        return carry

    lax.fori_loop(0, s // CONV_ROWS, conv, 0, unroll=CONV_UNROLL)


def _conv(proj, w, cb, lg, lb):
    b, s, _ = proj.shape
    vec = lambda v: v.reshape(1, CONV_CH)
    return pl.pallas_call(
        _conv_kernel,
        out_shape=jax.ShapeDtypeStruct((b, s, CONV_CH), BF16),
        grid=(b,),
        in_specs=[
            pl.BlockSpec((1, s, CONV_CH), lambda i: (i, 0, 0)),
            pl.BlockSpec((1, s, CONV_CH), lambda i: (i, 0, 1)),
            pl.BlockSpec((CONV_WIDTH, CONV_CH), lambda i: (0, 0)),
            pl.BlockSpec((1, CONV_CH), lambda i: (0, 0)),
            pl.BlockSpec((1, CONV_CH), lambda i: (0, 0)),
            pl.BlockSpec((1, CONV_CH), lambda i: (0, 0)),
        ],
        out_specs=pl.BlockSpec((1, s, CONV_CH), lambda i: (i, 0, 0)),
        scratch_shapes=[pltpu.VMEM((s + CONV_PAD, CONV_CH), F32)],
        compiler_params=_params(("parallel",)),
        name="conv",
    )(proj, proj, w, vec(cb), vec(lg), vec(lb))


def _tile_rows(j):
    return slice(j * ATT_TILE, (j + 1) * ATT_TILE)


def _lane_fold(op, x):
    out = x[:, :LANES]
    for c in range(1, x.shape[1] // LANES):
        out = op(out, x[:, c * LANES:(c + 1) * LANES])
    return out


def _group_mean_matrix(width, group):
    r = lax.broadcasted_iota(jnp.int32, (width, width), 0) // group
    c = lax.broadcasted_iota(jnp.int32, (width, width), 1) // group
    return jnp.where(r == c, 1.0 / group, 0.0).astype(BF16)


def _group_rms(x, gain, mean_mat, eps=1e-6):
    ms = _dot((x * x).astype(BF16), mean_mat)
    return x * lax.rsqrt(ms + eps) * gain


def _ones_columns(v_ref, vo_ref):
    vo_ref[:, :HEAD_DIM] = v_ref[...]
    vo_ref[:, HEAD_DIM:] = jnp.ones((v_ref.shape[0], HEAD_DIM), BF16)


def _logits_pass(i, q, k_ref, strip_ref, sel_add, s_ref):
    maps = q.shape[0] // ATT_TILE
    shape = (q.shape[0], ATT_TILE)
    mask = lax.broadcasted_iota(jnp.int32, shape, 0) % ATT_TILE >= lax.broadcasted_iota(jnp.int32, shape, 1)
    rep = lambda x: x if maps == 1 else jnp.concatenate([x] * maps, axis=0)
    m_acc = None
    for j in range(i + 1):
        s = _dot_nt(q, k_ref[_tile_rows(j), :])
        if j == i:
            s = jnp.where(mask, s + rep(strip_ref[0, :, ATT_TILE:]), NEG_INF)
        elif j == i - 1:
            s = s + rep(strip_ref[0, :, :ATT_TILE])
        if sel_add is not None and j < i:
            s = s + sel_add[:, j:j + 1]
        s_ref[:, _tile_rows(j)] = s
        t = _lane_fold(jnp.maximum, s)
        m_acc = t if m_acc is None else jnp.maximum(m_acc, t)
    return jnp.max(m_acc, axis=-1, keepdims=True)


def _softmax_pass(i, m, vo_ref, s_ref, p_ref):
    for j in range(i + 1):
        p_ref[:, _tile_rows(j)] = jnp.exp2(s_ref[:, _tile_rows(j)] - m).astype(BF16)
    n = (i + 1) * ATT_TILE
    out = _dot(p_ref[:, :n], vo_ref[:n, :])
    return out[:, :HEAD_DIM], out[:, HEAD_DIM:]


def _moba_kernel(q_ref, k_ref, v_ref, qg_ref, kg_ref, strip_ref, o_ref, kn_ref, vo_ref, kmean_ref, s_ref, p_ref):
    nb = k_ref.shape[1] // MOBA_BLOCK
    mean_mat = _group_mean_matrix(HEAD_DIM, HEAD_DIM)
    _ones_columns(v_ref.at[0], vo_ref)
    for j in range(nb):
        rows = _tile_rows(j)
        kn = _group_rms(k_ref[0, rows, :].astype(F32), kg_ref[...], mean_mat)
        kn_ref[rows, :] = kn.astype(BF16)
        kmean_ref[j:j + 1, :] = jnp.sum(kn, axis=0, keepdims=True) / float(MOBA_BLOCK)

    eye = (lax.broadcasted_iota(jnp.int32, (MOBA_BLOCK, MOBA_BLOCK), 0)
           == lax.broadcasted_iota(jnp.int32, (MOBA_BLOCK, MOBA_BLOCK), 1)).astype(F32).astype(BF16)
    n_sel = min(MOBA_TOPK, nb)

    def logits(i):
        qn = _group_rms(q_ref[0, _tile_rows(i), :].astype(F32), qg_ref[...], mean_mat)
        sel_add = None
        if i > n_sel:
            gate = _dot_nt(kmean_ref[...], qn, precision=lax.Precision.HIGHEST)
            blk = lax.broadcasted_iota(jnp.int32, gate.shape, 0)
            gate = jnp.where(blk < i, gate, TAKEN)
            sel = jnp.zeros(gate.shape, F32)
            for _ in range(n_sel):
                top = jnp.max(gate, axis=0, keepdims=True)
                first = jnp.min(jnp.where(gate == top, blk, nb), axis=0, keepdims=True)
                pick = blk == first
                sel = jnp.where(pick, 1.0, sel)
                gate = jnp.where(pick, TAKEN, gate)
            sel_t = _dot_nt(eye, sel.astype(BF16))
            sel_add = (sel_t - 1.0) * -NEG_INF
        q = (qn * (HEAD_DIM ** -0.5 * LOG2E)).astype(BF16)
        return _logits_pass(i, q, kn_ref, strip_ref, sel_add, s_ref.at[i % 2])

    m_next = logits(0)
    for i in range(nb):
        m = m_next
        if i + 1 < nb:
            m_next = logits(i + 1)
        acc, l = _softmax_pass(i, m, vo_ref, s_ref.at[i % 2], p_ref.at[i % 2])
        o_ref[0, _tile_rows(i), :] = (acc / l).astype(o_ref.dtype)


def _moba(proj, strips, q_gain, k_gain):
    b, s, _ = proj.shape
    nb = s // MOBA_BLOCK
    return pl.pallas_call(
        _moba_kernel,
        out_shape=jax.ShapeDtypeStruct((b, s, MOBA_HEADS * HEAD_DIM), BF16),
        grid=(b, MOBA_HEADS),
        in_specs=[
            pl.BlockSpec((1, s, HEAD_DIM), lambda bi, h: (bi, 0, COL_MOBA_Q + h)),
            pl.BlockSpec((1, s, HEAD_DIM), lambda bi, h: (bi, 0, COL_MOBA_K + h)),
            pl.BlockSpec((1, s, HEAD_DIM), lambda bi, h: (bi, 0, COL_MOBA_V + h)),
            pl.BlockSpec((1, HEAD_DIM), lambda bi, h: (0, 0)),
            pl.BlockSpec((1, HEAD_DIM), lambda bi, h: (0, 0)),
            pl.BlockSpec((1, ATT_TILE, 2 * ATT_TILE), lambda bi, h: (h, 0, 0)),
        ],
        out_specs=pl.BlockSpec((1, s, HEAD_DIM), lambda bi, h: (bi, 0, h)),
        scratch_shapes=[
            pltpu.VMEM((s, HEAD_DIM), BF16),
            pltpu.VMEM((s, 2 * HEAD_DIM), BF16),
            pltpu.VMEM((nb, HEAD_DIM), F32),
            pltpu.VMEM((2, ATT_TILE, s), F32),
            pltpu.VMEM((2, ATT_TILE, s), BF16),
        ],
        compiler_params=_params(("parallel", "parallel")),
        name="moba",
    )(proj, proj, proj, q_gain.reshape(1, HEAD_DIM), k_gain.reshape(1, HEAD_DIM), strips)


DIFF_MAPS = 2


def _diff_kernel(lam_init, q_ref, k_ref, v_ref, qg_ref, kg_ref, lam_ref, sg_ref, strip_ref, o_ref,
                 kn_ref, vo_ref, s_ref, p_ref):
    nt = k_ref.shape[1] // ATT_TILE
    mean_mat = _group_mean_matrix(HEAD_DIM, DIFF_QK_DIM)
    _ones_columns(v_ref.at[0], vo_ref)
    for j in range(nt):
        rows = _tile_rows(j)
        kn_ref[rows, :] = _group_rms(k_ref[0, rows, :].astype(F32), kg_ref[...], mean_mat).astype(BF16)

    lp = lam_ref[...]
    lam = (jnp.exp(jnp.sum(lp[0:1] * lp[1:2], axis=-1, keepdims=True))
           - jnp.exp(jnp.sum(lp[2:3] * lp[3:4], axis=-1, keepdims=True)) + lam_init)
    scale = DIFF_QK_DIM ** -0.5 * LOG2E
    lane_map = lax.broadcasted_iota(jnp.int32, (ATT_TILE, HEAD_DIM), 1) // DIFF_QK_DIM

    def logits(i):
        qn = _group_rms(q_ref[0, _tile_rows(i), :].astype(F32), qg_ref[...], mean_mat) * scale
        q = jnp.concatenate([jnp.where(lane_map == c, qn, 0.0) for c in range(DIFF_MAPS)], axis=0).astype(BF16)
        return _logits_pass(i, q, kn_ref, strip_ref, None, s_ref.at[i % 2])

    m_next = logits(0)
    for i in range(nt):
        m = m_next
        if i + 1 < nt:
            m_next = logits(i + 1)
        acc, l = _softmax_pass(i, m, vo_ref, s_ref.at[i % 2], p_ref.at[i % 2])
        a = acc / l
        o = a[:ATT_TILE] - lam * a[ATT_TILE:]
        o_ref[0, _tile_rows(i), :] = (_rms(o, sg_ref[...]) * (1.0 - lam_init)).astype(o_ref.dtype)


def _diff(proj, strips, q_gain, k_gain, lam_params, subln, lam_init):
    b, s, _ = proj.shape
    both = lambda g: jnp.tile(g, DIFF_MAPS).reshape(1, HEAD_DIM)
    return pl.pallas_call(
        functools.partial(_diff_kernel, lam_init),
        out_shape=jax.ShapeDtypeStruct((b, s, DIFF_HEADS * HEAD_DIM), BF16),
        grid=(b, DIFF_HEADS),
        in_specs=[
            pl.BlockSpec((1, s, HEAD_DIM), lambda bi, h: (bi, 0, COL_DIFF_Q + h)),
            pl.BlockSpec((1, s, HEAD_DIM), lambda bi, h: (bi, 0, COL_DIFF_K + h)),
            pl.BlockSpec((1, s, HEAD_DIM), lambda bi, h: (bi, 0, COL_DIFF_V + h)),
            pl.BlockSpec((1, HEAD_DIM), lambda bi, h: (0, 0)),
            pl.BlockSpec((1, HEAD_DIM), lambda bi, h: (0, 0)),
            pl.BlockSpec((4, DIFF_QK_DIM), lambda bi, h: (0, 0)),
            pl.BlockSpec((1, HEAD_DIM), lambda bi, h: (0, 0)),
            pl.BlockSpec((1, ATT_TILE, 2 * ATT_TILE), lambda bi, h: (MOBA_HEADS + h, 0, 0)),
        ],
        out_specs=pl.BlockSpec((1, s, HEAD_DIM), lambda bi, h: (bi, 0, h)),
        scratch_shapes=[
            pltpu.VMEM((s, HEAD_DIM), BF16),
            pltpu.VMEM((s, 2 * HEAD_DIM), BF16),
            pltpu.VMEM((2, DIFF_MAPS * ATT_TILE, s), F32),
            pltpu.VMEM((2, DIFF_MAPS * ATT_TILE, s), BF16),
        ],
        compiler_params=_params(("parallel", "parallel")),
        name="diff",
    )(proj, proj, proj, both(q_gain), both(k_gain), lam_params, subln.reshape(1, HEAD_DIM), strips)


def _wout_kernel(x_ref, yc_ref, ym_ref, yd_ref, w_ref, o_ref):
    c0 = CONV_CH
    c1 = c0 + MOBA_HEADS * HEAD_DIM
    o_ref[...] = (x_ref[...] + _dot(yc_ref[...], w_ref[0:c0, :]) + _dot(ym_ref[...], w_ref[c0:c1, :])
                  + _dot(yd_ref[...], w_ref[c1:, :]))


def _wout(x, yc, ym, yd, w, l, tm=1024, tn=1024):
    t, d = x.shape
    return pl.pallas_call(
        _wout_kernel,
        out_shape=jax.ShapeDtypeStruct((t, d), F32),
        grid=(t // tm, d // tn),
        in_specs=[
            pl.BlockSpec((tm, tn), lambda i, j: (i, j)),
            pl.BlockSpec((tm, yc.shape[1]), lambda i, j: (i, 0)),
            pl.BlockSpec((tm, ym.shape[1]), lambda i, j: (i, 0)),
            pl.BlockSpec((tm, yd.shape[1]), lambda i, j: (i, 0)),
            pl.BlockSpec((None, w.shape[1], tn), lambda i, j: (l, 0, j)),
        ],
        out_specs=pl.BlockSpec((tm, tn), lambda i, j: (i, j)),
        compiler_params=_params(("parallel", "parallel")),
        name="wout",
    )(x, yc, ym, yd, w)


def _memkv_kernel(m_ref, g_ref, w_ref, kg_ref, k_ref, v_ref):
    hm = _rms(m_ref[0], g_ref[...]).astype(BF16)
    kv = _dot(hm, w_ref[...])
    width = MEM_HEADS * HEAD_DIM
    for h in range(MEM_HEADS):
        cols = slice(h * HEAD_DIM, (h + 1) * HEAD_DIM)
        k_ref[0, :, cols] = _rms(kv[:, cols], kg_ref[...]).astype(BF16)
    v_ref[0] = kv[:, width:].astype(BF16)


def _memkv(mem, gain, wkv, k_gain, l):
    b, m, d = mem.shape
    width = MEM_HEADS * HEAD_DIM
    out = jax.ShapeDtypeStruct((b, m, width), BF16)
    return pl.pallas_call(
        _memkv_kernel,
        out_shape=(out, out),
        grid=(b,),
        in_specs=[
            pl.BlockSpec((1, m, d), lambda i: (i, 0, 0)),
            pl.BlockSpec((1, d), lambda i: (0, 0)),
            pl.BlockSpec((None, d, 2 * width), lambda i: (l, 0, 0)),
            pl.BlockSpec((1, HEAD_DIM), lambda i: (0, 0)),
        ],
        out_specs=(pl.BlockSpec((1, m, width), lambda i: (i, 0, 0)), pl.BlockSpec((1, m, width), lambda i: (i, 0, 0))),
        compiler_params=_params(("parallel",)),
        name="memkv",
    )(mem, gain.reshape(1, d), wkv, k_gain.reshape(1, HEAD_DIM))


def _memattn_kernel(x_ref, g_ref, wq_ref, qg_ref, k_ref, v_ref, wo_ref, o_ref, ob_ref):
    x = x_ref[0]
    hx = _rms(x, g_ref[...]).astype(BF16)
    q = _dot(hx, wq_ref[...])
    for h in range(MEM_HEADS):
        cols = slice(h * HEAD_DIM, (h + 1) * HEAD_DIM)
        qh = (_rms(q[:, cols], qg_ref[...]) * HEAD_DIM ** -0.5).astype(BF16)
        s = _dot_nt(qh, k_ref[0, :, cols])
        p = jnp.exp(s - jnp.max(s, axis=-1, keepdims=True))
        oh = _dot(p.astype(BF16), v_ref[0, :, cols]) / jnp.sum(p, axis=-1, keepdims=True)
        ob_ref[:, cols] = oh.astype(BF16)
    o_ref[0] = x + _dot(ob_ref[...], wo_ref[...])


def _memattn(x, gain, wq, q_gain, kn, v, wo, l, tm=512):
    b, s, d = x.shape
    m = kn.shape[1]
    width = MEM_HEADS * HEAD_DIM
    return pl.pallas_call(
        _memattn_kernel,
        out_shape=jax.ShapeDtypeStruct((b, s, d), F32),
        grid=(b, s // tm),
        in_specs=[
            pl.BlockSpec((1, tm, d), lambda bi, i: (bi, i, 0)),
            pl.BlockSpec((1, d), lambda bi, i: (0, 0)),
            pl.BlockSpec((None, d, width), lambda bi, i: (l, 0, 0)),
            pl.BlockSpec((1, HEAD_DIM), lambda bi, i: (0, 0)),
            pl.BlockSpec((1, m, width), lambda bi, i: (bi, 0, 0)),
            pl.BlockSpec((1, m, width), lambda bi, i: (bi, 0, 0)),
            pl.BlockSpec((None, width, d), lambda bi, i: (l, 0, 0)),
        ],
        out_specs=pl.BlockSpec((1, tm, d), lambda bi, i: (bi, i, 0)),
        scratch_shapes=[pltpu.VMEM((tm, width), BF16)],
        compiler_params=_params(("parallel", "parallel")),
        name="memattn",
    )(x, gain.reshape(1, d), wq, q_gain.reshape(1, HEAD_DIM), kn, v, wo)


CAST_BLOCK_BYTES = 6 * 1024 * 1024


def _cast_kernel(w_ref, o_ref):
    o_ref[...] = w_ref[...].astype(o_ref.dtype)


def _cast_bf16(w, layer=None):
    depth, k, n = w.shape
    first, count = (0, depth) if layer is None else (layer, 1)
    bk = max(r for r in range(BF16_SUBLANES, k + 1, BF16_SUBLANES) if k % r == 0 and r * n * 4 <= CAST_BLOCK_BYTES)
    return pl.pallas_call(
        _cast_kernel,
        out_shape=jax.ShapeDtypeStruct((count, k, n), BF16),
        grid=(count, k // bk),
        in_specs=[pl.BlockSpec((None, bk, n), lambda l, i: (first + l, i, 0))],
        out_specs=pl.BlockSpec((None, bk, n), lambda l, i: (l, i, 0)),
        compiler_params=_params(("parallel", "parallel")),
        name="cast_bf16",
    )(w)


def kernel(x, mem, rel_bias, ffn1_norm, ffn1_w_gate, ffn1_w_up, ffn1_w_down, mix_norm, w_in, conv_w, conv_b, conv_ln_g, conv_ln_b, moba_q_norm, moba_k_norm, diff_q_norm, diff_k_norm, diff_lambda, diff_subln, w_out, mem_norm_x, mem_norm_m, mem_wq, mem_wkv, mem_q_norm, mem_k_norm, mem_wo, ffn2_norm, ffn2_w_gate, ffn2_w_up, ffn2_w_down):
    b, s, d = x.shape
    depth = w_in.shape[0]
    t = b * s
    tab_flat = rel_bias.T.reshape(-1)
    strips = _bias_strips(tab_flat)
    ffn1_f32 = (ffn1_w_gate, ffn1_w_up, ffn1_w_down)
    ffn2_f32 = (ffn2_w_gate, ffn2_w_up, ffn2_w_down)
    ffn_w = [_cast_bf16(w, 0) for w in ffn1_f32]
    w_out, mem_wq, mem_wkv, mem_wo = (_cast_bf16(w) for w in (w_out, mem_wq, mem_wkv, mem_wo))
    xt = x.reshape(t, d)
    for l in range(depth):
        xt, (*ffn_w, w_in_l) = _ffn(xt, ffn1_norm[l], *ffn_w, 0, side=[(w, l) for w in (*ffn2_f32, w_in)])
        proj = _proj(xt, mix_norm[l], w_in_l, 0).reshape(b, s, PROJ_WIDTH)
        y_conv = _conv(proj, conv_w[l], conv_b[l], conv_ln_g[l], conv_ln_b[l])
        y_moba = _moba(proj, strips, moba_q_norm[l], moba_k_norm[l])
        lam_init = 0.8 - 0.6 * math.exp(-0.3 * l)
        y_diff = _diff(proj, strips, diff_q_norm[l], diff_k_norm[l], diff_lambda[l], diff_subln[l], lam_init)
        xt = _wout(xt, y_conv.reshape(t, -1), y_moba.reshape(t, -1), y_diff.reshape(t, -1), w_out, l)
        kn, v = _memkv(mem, mem_norm_m[l], mem_wkv, mem_k_norm[l], l)
        xt = _memattn(xt.reshape(b, s, d), mem_norm_x[l], mem_wq, mem_q_norm[l], kn, v, mem_wo, l).reshape(t, d)
        nxt = [(w, l + 1) for w in ffn1_f32] if l + 1 < depth else []
        xt, ffn_w = _ffn(xt, ffn2_norm[l], *ffn_w, 0, side=nxt)
    return xt.reshape(b, s, d)
```

```python
import functools
import math

import jax
import jax.numpy as jnp
from jax import lax
from jax.experimental import pallas as pl
from jax.experimental.pallas import tpu as pltpu

F32 = jnp.float32
BF16 = jnp.bfloat16

D_MODEL = 2048
D_FF = 5632
CONV_CH = 512
CONV_WIDTH = 31
CONV_PAD = 32
HEAD_DIM = 128
LANES = 128
BF16_SUBLANES = 16
MOBA_HEADS = 6
MOBA_BLOCK = 256
MOBA_TOPK = 3
DIFF_HEADS = 6
DIFF_QK_DIM = 64
MEM_HEADS = 4
N_BUCKETS = 32
MAX_DISTANCE = 128
N_BIAS_HEADS = MOBA_HEADS + DIFF_HEADS
NEG_INF = -1e30
TAKEN = -3e38
PROJ_WIDTH = 5632
COL_CONV_A = 0
COL_CONV_G = CONV_CH // HEAD_DIM
COL_MOBA_Q = 2 * CONV_CH // HEAD_DIM
COL_MOBA_K = COL_MOBA_Q + MOBA_HEADS
COL_MOBA_V = COL_MOBA_K + MOBA_HEADS
COL_DIFF_Q = COL_MOBA_V + MOBA_HEADS
COL_DIFF_K = COL_DIFF_Q + DIFF_HEADS
COL_DIFF_V = COL_DIFF_K + DIFF_HEADS

ATT_TILE = 256
assert ATT_TILE == MOBA_BLOCK and ATT_TILE >= MAX_DISTANCE
LOG2E = 1.4426950408889634
VMEM_LIMIT = 56 * 1024 * 1024


def _params(semantics, vmem=VMEM_LIMIT):
    return pltpu.CompilerParams(dimension_semantics=semantics, vmem_limit_bytes=vmem)


def _rms(x, gain, eps=1e-6):
    return x * lax.rsqrt(jnp.mean(x * x, axis=-1, keepdims=True) + eps) * gain


def _dot(a, b):
    return jnp.dot(a, b, preferred_element_type=F32)


def _dot_nt(a, b, precision=None):
    return lax.dot_general(a, b, (((1,), (1,)), ((), ())), precision=precision, preferred_element_type=F32)


def _ffn_kernel(n_side, x_ref, g_ref, wg_ref, wu_ref, wd_ref, *refs):
    side_in, o_ref, side_out, h_ref = refs[:n_side], refs[n_side], refs[n_side + 1:2 * n_side + 1], refs[-1]
    k = pl.program_id(1)

    @pl.when(k == 0)
    def _():
        x = x_ref[...]
        h_ref[...] = _rms(x, g_ref[...]).astype(BF16)
        o_ref[...] = x

    h = h_ref[...]
    gate = _dot(h, wg_ref[...].astype(BF16))
    up = _dot(h, wu_ref[...].astype(BF16))
    act = (0.5 * gate / (1.0 + jnp.exp(-gate)) * up).astype(BF16)
    o_ref[...] += _dot(act, wd_ref[...].astype(BF16))
    for w_ref, wb_ref in zip(side_in, side_out):
        wb_ref[...] = w_ref[...].astype(BF16)


def _side_block(shape, gi, gk):
    k, n = shape
    if k % gi == 0 and n % gk == 0 and (k // gi) % BF16_SUBLANES == 0 and (n // gk) % LANES == 0:
        return (k // gi, n // gk), (lambda i, kk: (i, kk))
    assert k % gk == 0 and n % gi == 0 and (k // gk) % BF16_SUBLANES == 0 and (n // gi) % LANES == 0, shape
    return (k // gk, n // gi), (lambda i, kk: (kk, i))


FFN_BLOCK_BF16 = 512
FFN_BLOCK_F32 = 256


def _ffn(x, gain, wg, wu, wd, l, side=(), tm=1024, tf=FFN_BLOCK_BF16):
    t, d = x.shape
    f = wg.shape[2]
    gi, gk = t // tm, f // tf
    side_in_specs, side_out_specs, side_shapes = [], [], []
    for w, ls in side:
        blk, order = _side_block(w.shape[1:], gi, gk)
        side_in_specs.append(pl.BlockSpec((None, *blk), lambda i, k, ls=ls, order=order: (ls, *order(i, k))))
        side_out_specs.append(pl.BlockSpec((None, *blk), lambda i, k, order=order: (0, *order(i, k))))
        side_shapes.append(jax.ShapeDtypeStruct((1, *w.shape[1:]), BF16))
    out, *cast = pl.pallas_call(
        functools.partial(_ffn_kernel, len(side)),
        out_shape=[jax.ShapeDtypeStruct((t, d), F32), *side_shapes],
        grid=(gi, gk),
        in_specs=[
            pl.BlockSpec((tm, d), lambda i, k: (i, 0), pipeline_mode=pl.Buffered(1)),
            pl.BlockSpec((1, d), lambda i, k: (0, 0)),
            pl.BlockSpec((None, d, tf), lambda i, k: (l, 0, k)),
            pl.BlockSpec((None, d, tf), lambda i, k: (l, 0, k)),
            pl.BlockSpec((None, tf, d), lambda i, k: (l, k, 0)),
            *side_in_specs,
        ],
        out_specs=[pl.BlockSpec((tm, d), lambda i, k: (i, 0)), *side_out_specs],
        scratch_shapes=[pltpu.VMEM((tm, d), BF16)],
        compiler_params=_params(("parallel", "arbitrary")),
        name="ffn",
    )(x, gain.reshape(1, d), wg, wu, wd, *(w for w, _ in side))
    return out, cast


def _proj_kernel(x_ref, g_ref, w_ref, o_ref, h_ref):
    @pl.when(pl.program_id(1) == 0)
    def _():
        h_ref[...] = _rms(x_ref[...], g_ref[...]).astype(BF16)

    o_ref[...] = _dot(h_ref[...], w_ref[...]).astype(o_ref.dtype)


def _proj(x, gain, w, l, tm=2048, tn=512):
    t, d = x.shape
    n = w.shape[2]
    return pl.pallas_call(
        _proj_kernel,
        out_shape=jax.ShapeDtypeStruct((t, n), BF16),
        grid=(t // tm, n // tn),
        in_specs=[
            pl.BlockSpec((tm, d), lambda i, j: (i, 0)),
            pl.BlockSpec((1, d), lambda i, j: (0, 0)),
            pl.BlockSpec((None, d, tn), lambda i, j: (l, 0, j)),
        ],
        out_specs=pl.BlockSpec((tm, tn), lambda i, j: (i, j)),
        scratch_shapes=[pltpu.VMEM((tm, d), BF16)],
        compiler_params=_params(("parallel", "arbitrary")),
        name="proj",
    )(x, gain.reshape(1, d), w)


def _bias_kernel(tab_ref, o_ref):
    h = pl.program_id(0)
    far = tab_ref[h * N_BUCKETS + N_BUCKETS - 1]
    shape = (ATT_TILE, 2 * ATT_TILE)
    dist = ATT_TILE + lax.broadcasted_iota(jnp.int32, shape, 0) - lax.broadcasted_iota(jnp.int32, shape, 1)
    dist = jnp.maximum(dist, 0)
    max_exact = N_BUCKETS // 2
    log_ratio = jnp.log(jnp.maximum(dist, 1).astype(F32) / max_exact) / math.log(MAX_DISTANCE / max_exact)
    large = jnp.minimum(max_exact + (log_ratio * (N_BUCKETS - max_exact)).astype(jnp.int32), N_BUCKETS - 1)
    bucket = jnp.where(dist < max_exact, dist, large)
    out = jnp.zeros(shape, F32)
    for b in range(N_BUCKETS):
        out = jnp.where(bucket == b, (tab_ref[h * N_BUCKETS + b] - far) * LOG2E, out)
    o_ref[0] = out


def _bias_strips(tab_flat):
    return pl.pallas_call(
        _bias_kernel,
        out_shape=jax.ShapeDtypeStruct((N_BIAS_HEADS, ATT_TILE, 2 * ATT_TILE), F32),
        grid=(N_BIAS_HEADS,),
        in_specs=[pl.BlockSpec(memory_space=pltpu.SMEM)],
        out_specs=pl.BlockSpec((1, ATT_TILE, 2 * ATT_TILE), lambda h: (h, 0, 0)),
        compiler_params=_params(("arbitrary",)),
        name="bias_strips",
    )(tab_flat)


CONV_ROWS = 32
CONV_UNROLL = 4
SUBLANES = 8
assert CONV_PAD % SUBLANES == 0 and CONV_PAD >= SUBLANES * -(-CONV_WIDTH // SUBLANES)


def _conv_kernel(a_ref, g_ref, w_ref, cb_ref, lg_ref, lb_ref, o_ref, u_ref):
    s = a_ref.shape[1]
    u_ref[0:CONV_PAD, :] = jnp.zeros((CONV_PAD, CONV_CH), F32)

    def glu(r, carry):
        rows = pl.ds(pl.multiple_of(r * CONV_ROWS, CONV_ROWS), CONV_ROWS)
        a = a_ref[0, rows, :].astype(F32)
        g = g_ref[0, rows, :].astype(F32)
        u_ref[pl.ds(pl.multiple_of(CONV_PAD + r * CONV_ROWS, CONV_ROWS), CONV_ROWS), :] = a / (1.0 + jnp.exp(-g))
        return carry

    lax.fori_loop(0, s // CONV_ROWS, glu, 0)

    win_rows = CONV_ROWS + CONV_PAD
    n_groups = CONV_CH // LANES

    def conv(r, carry):
        base = pl.multiple_of(r * CONV_ROWS, CONV_ROWS)
        accs = []
        for c in range(n_groups):
            lanes = slice(c * LANES, (c + 1) * LANES)
            win = u_ref[pl.ds(base, win_rows), lanes]
            acc = jnp.zeros((CONV_ROWS, LANES), F32) + cb_ref[:, lanes]
            for b in range(SUBLANES):
                shifted = pltpu.roll(win, b, 0) if b else win
                for a in range(-(-CONV_WIDTH // SUBLANES)):
                    d = SUBLANES * a + b
                    if d < CONV_WIDTH:
                        k0 = CONV_PAD - SUBLANES * a
                        tap = CONV_WIDTH - 1 - d
                        acc = acc + shifted[k0:k0 + CONV_ROWS, :] * w_ref[tap:tap + 1, lanes]
            accs.append(acc)
        mu = sum(jnp.sum(acc, axis=-1, keepdims=True) for acc in accs) / float(CONV_CH)
        cens = [acc - mu for acc in accs]
        var = sum(jnp.sum(cen * cen, axis=-1, keepdims=True) for cen in cens) / float(CONV_CH)
        inv = lax.rsqrt(var + 1e-5)
        for c, cen in enumerate(cens):
            lanes = slice(c * LANES, (c + 1) * LANES)
            y = cen * inv * lg_ref[:, lanes] + lb_ref[:, lanes]
            o_ref[0, pl.ds(base, CONV_ROWS), lanes] = (y / (1.0 + jnp.exp(-y))).astype(o_ref.dtype)
        return carry

    lax.fori_loop(0, s // CONV_ROWS, conv, 0, unroll=CONV_UNROLL)


def _conv(proj, w, cb, lg, lb):
    b, s, _ = proj.shape
    vec = lambda v: v.reshape(1, CONV_CH)
    return pl.pallas_call(
        _conv_kernel,
        out_shape=jax.ShapeDtypeStruct((b, s, CONV_CH), BF16),
        grid=(b,),
        in_specs=[
            pl.BlockSpec((1, s, CONV_CH), lambda i: (i, 0, 0)),
            pl.BlockSpec((1, s, CONV_CH), lambda i: (i, 0, 1)),
            pl.BlockSpec((CONV_WIDTH, CONV_CH), lambda i: (0, 0)),
            pl.BlockSpec((1, CONV_CH), lambda i: (0, 0)),
            pl.BlockSpec((1, CONV_CH), lambda i: (0, 0)),
            pl.BlockSpec((1, CONV_CH), lambda i: (0, 0)),
        ],
        out_specs=pl.BlockSpec((1, s, CONV_CH), lambda i: (i, 0, 0)),
        scratch_shapes=[pltpu.VMEM((s + CONV_PAD, CONV_CH), F32)],
        compiler_params=_params(("parallel",)),
        name="conv",
    )(proj, proj, w, vec(cb), vec(lg), vec(lb))


def _tile_rows(j):
    return slice(j * ATT_TILE, (j + 1) * ATT_TILE)


def _lane_fold(op, x):
    out = x[:, :LANES]
    for c in range(1, x.shape[1] // LANES):
        out = op(out, x[:, c * LANES:(c + 1) * LANES])
    return out


def _group_mean_matrix(width, group):
    r = lax.broadcasted_iota(jnp.int32, (width, width), 0) // group
    c = lax.broadcasted_iota(jnp.int32, (width, width), 1) // group
    return jnp.where(r == c, 1.0 / group, 0.0).astype(BF16)


def _group_rms(x, gain, mean_mat, eps=1e-6):
    ms = _dot((x * x).astype(BF16), mean_mat)
    return x * lax.rsqrt(ms + eps) * gain


def _ones_columns(v_ref, vo_ref):
    vo_ref[:, :HEAD_DIM] = v_ref[...]
    vo_ref[:, HEAD_DIM:] = jnp.ones((v_ref.shape[0], HEAD_DIM), BF16)


def _logits_pass(i, q, k_ref, strip_ref, sel_add, s_ref):
    maps = q.shape[0] // ATT_TILE
    shape = (q.shape[0], ATT_TILE)
    mask = lax.broadcasted_iota(jnp.int32, shape, 0) % ATT_TILE >= lax.broadcasted_iota(jnp.int32, shape, 1)
    rep = lambda x: x if maps == 1 else jnp.concatenate([x] * maps, axis=0)
    m_acc = None
    for j in range(i + 1):
        s = _dot_nt(q, k_ref[_tile_rows(j), :])
        if j == i:
            s = jnp.where(mask, s + rep(strip_ref[0, :, ATT_TILE:]), NEG_INF)
        elif j == i - 1:
            s = s + rep(strip_ref[0, :, :ATT_TILE])
        if sel_add is not None and j < i:
            s = s + sel_add[:, j:j + 1]
        s_ref[:, _tile_rows(j)] = s
        t = _lane_fold(jnp.maximum, s)
        m_acc = t if m_acc is None else jnp.maximum(m_acc, t)
    return jnp.max(m_acc, axis=-1, keepdims=True)


def _softmax_pass(i, m, vo_ref, s_ref, p_ref):
    for j in range(i + 1):
        p_ref[:, _tile_rows(j)] = jnp.exp2(s_ref[:, _tile_rows(j)] - m).astype(BF16)
    n = (i + 1) * ATT_TILE
    out = _dot(p_ref[:, :n], vo_ref[:n, :])
    return out[:, :HEAD_DIM], out[:, HEAD_DIM:]


def _moba_kernel(q_ref, k_ref, v_ref, qg_ref, kg_ref, strip_ref, o_ref, kn_ref, vo_ref, kmean_ref, s_ref, p_ref):
    nb = k_ref.shape[1] // MOBA_BLOCK
    mean_mat = _group_mean_matrix(HEAD_DIM, HEAD_DIM)
    _ones_columns(v_ref.at[0], vo_ref)
    for j in range(nb):
        rows = _tile_rows(j)
        kn = _group_rms(k_ref[0, rows, :].astype(F32), kg_ref[...], mean_mat)
        kn_ref[rows, :] = kn.astype(BF16)
        kmean_ref[j:j + 1, :] = jnp.sum(kn, axis=0, keepdims=True) / float(MOBA_BLOCK)

    eye = (lax.broadcasted_iota(jnp.int32, (MOBA_BLOCK, MOBA_BLOCK), 0)
           == lax.broadcasted_iota(jnp.int32, (MOBA_BLOCK, MOBA_BLOCK), 1)).astype(F32).astype(BF16)
    n_sel = min(MOBA_TOPK, nb)

    def logits(i):
        qn = _group_rms(q_ref[0, _tile_rows(i), :].astype(F32), qg_ref[...], mean_mat)
        sel_add = None
        if i > n_sel:
            gate = _dot_nt(kmean_ref[...], qn, precision=lax.Precision.HIGHEST)
            blk = lax.broadcasted_iota(jnp.int32, gate.shape, 0)
            gate = jnp.where(blk < i, gate, TAKEN)
            sel = jnp.zeros(gate.shape, F32)
            for _ in range(n_sel):
                top = jnp.max(gate, axis=0, keepdims=True)
                first = jnp.min(jnp.where(gate == top, blk, nb), axis=0, keepdims=True)
                pick = blk == first
                sel = jnp.where(pick, 1.0, sel)
                gate = jnp.where(pick, TAKEN, gate)
            sel_t = _dot_nt(eye, sel.astype(BF16))
            sel_add = (sel_t - 1.0) * -NEG_INF
        q = (qn * (HEAD_DIM ** -0.5 * LOG2E)).astype(BF16)
        return _logits_pass(i, q, kn_ref, strip_ref, sel_add, s_ref.at[i % 2])

    m_next = logits(0)
    for i in range(nb):
        m = m_next
        if i + 1 < nb:
            m_next = logits(i + 1)
        acc, l = _softmax_pass(i, m, vo_ref, s_ref.at[i % 2], p_ref.at[i % 2])
        o_ref[0, _tile_rows(i), :] = (acc / l).astype(o_ref.dtype)


def _moba(proj, strips, q_gain, k_gain):
    b, s, _ = proj.shape
    nb = s // MOBA_BLOCK
    return pl.pallas_call(
        _moba_kernel,
        out_shape=jax.ShapeDtypeStruct((b, s, MOBA_HEADS * HEAD_DIM), BF16),
        grid=(b, MOBA_HEADS),
        in_specs=[
            pl.BlockSpec((1, s, HEAD_DIM), lambda bi, h: (bi, 0, COL_MOBA_Q + h)),
            pl.BlockSpec((1, s, HEAD_DIM), lambda bi, h: (bi, 0, COL_MOBA_K + h)),
            pl.BlockSpec((1, s, HEAD_DIM), lambda bi, h: (bi, 0, COL_MOBA_V + h)),
            pl.BlockSpec((1, HEAD_DIM), lambda bi, h: (0, 0)),
            pl.BlockSpec((1, HEAD_DIM), lambda bi, h: (0, 0)),
            pl.BlockSpec((1, ATT_TILE, 2 * ATT_TILE), lambda bi, h: (h, 0, 0)),
        ],
        out_specs=pl.BlockSpec((1, s, HEAD_DIM), lambda bi, h: (bi, 0, h)),
        scratch_shapes=[
            pltpu.VMEM((s, HEAD_DIM), BF16),
            pltpu.VMEM((s, 2 * HEAD_DIM), BF16),
            pltpu.VMEM((nb, HEAD_DIM), F32),
            pltpu.VMEM((2, ATT_TILE, s), F32),
            pltpu.VMEM((2, ATT_TILE, s), BF16),
        ],
        compiler_params=_params(("parallel", "parallel")),
        name="moba",
    )(proj, proj, proj, q_gain.reshape(1, HEAD_DIM), k_gain.reshape(1, HEAD_DIM), strips)


DIFF_MAPS = 2


def _diff_kernel(lam_init, q_ref, k_ref, v_ref, qg_ref, kg_ref, lam_ref, sg_ref, strip_ref, o_ref,
                 kn_ref, vo_ref, s_ref, p_ref):
    nt = k_ref.shape[1] // ATT_TILE
    mean_mat = _group_mean_matrix(HEAD_DIM, DIFF_QK_DIM)
    _ones_columns(v_ref.at[0], vo_ref)
    for j in range(nt):
        rows = _tile_rows(j)
        kn_ref[rows, :] = _group_rms(k_ref[0, rows, :].astype(F32), kg_ref[...], mean_mat).astype(BF16)

    lp = lam_ref[...]
    lam = (jnp.exp(jnp.sum(lp[0:1] * lp[1:2], axis=-1, keepdims=True))
           - jnp.exp(jnp.sum(lp[2:3] * lp[3:4], axis=-1, keepdims=True)) + lam_init)
    scale = DIFF_QK_DIM ** -0.5 * LOG2E
    lane_map = lax.broadcasted_iota(jnp.int32, (ATT_TILE, HEAD_DIM), 1) // DIFF_QK_DIM

    def logits(i):
        qn = _group_rms(q_ref[0, _tile_rows(i), :].astype(F32), qg_ref[...], mean_mat) * scale
        q = jnp.concatenate([jnp.where(lane_map == c, qn, 0.0) for c in range(DIFF_MAPS)], axis=0).astype(BF16)
        return _logits_pass(i, q, kn_ref, strip_ref, None, s_ref.at[i % 2])

    m_next = logits(0)
    for i in range(nt):
        m = m_next
        if i + 1 < nt:
            m_next = logits(i + 1)
        acc, l = _softmax_pass(i, m, vo_ref, s_ref.at[i % 2], p_ref.at[i % 2])
        a = acc / l
        o = a[:ATT_TILE] - lam * a[ATT_TILE:]
        o_ref[0, _tile_rows(i), :] = (_rms(o, sg_ref[...]) * (1.0 - lam_init)).astype(o_ref.dtype)


def _diff(proj, strips, q_gain, k_gain, lam_params, subln, lam_init):
    b, s, _ = proj.shape
    both = lambda g: jnp.tile(g, DIFF_MAPS).reshape(1, HEAD_DIM)
    return pl.pallas_call(
        functools.partial(_diff_kernel, lam_init),
        out_shape=jax.ShapeDtypeStruct((b, s, DIFF_HEADS * HEAD_DIM), BF16),
        grid=(b, DIFF_HEADS),
        in_specs=[
            pl.BlockSpec((1, s, HEAD_DIM), lambda bi, h: (bi, 0, COL_DIFF_Q + h)),
            pl.BlockSpec((1, s, HEAD_DIM), lambda bi, h: (bi, 0, COL_DIFF_K + h)),
            pl.BlockSpec((1, s, HEAD_DIM), lambda bi, h: (bi, 0, COL_DIFF_V + h)),
            pl.BlockSpec((1, HEAD_DIM), lambda bi, h: (0, 0)),
            pl.BlockSpec((1, HEAD_DIM), lambda bi, h: (0, 0)),
            pl.BlockSpec((4, DIFF_QK_DIM), lambda bi, h: (0, 0)),
            pl.BlockSpec((1, HEAD_DIM), lambda bi, h: (0, 0)),
            pl.BlockSpec((1, ATT_TILE, 2 * ATT_TILE), lambda bi, h: (MOBA_HEADS + h, 0, 0)),
        ],
        out_specs=pl.BlockSpec((1, s, HEAD_DIM), lambda bi, h: (bi, 0, h)),
        scratch_shapes=[
            pltpu.VMEM((s, HEAD_DIM), BF16),
            pltpu.VMEM((s, 2 * HEAD_DIM), BF16),
            pltpu.VMEM((2, DIFF_MAPS * ATT_TILE, s), F32),
            pltpu.VMEM((2, DIFF_MAPS * ATT_TILE, s), BF16),
        ],
        compiler_params=_params(("parallel", "parallel")),
        name="diff",
    )(proj, proj, proj, both(q_gain), both(k_gain), lam_params, subln.reshape(1, HEAD_DIM), strips)


def _memkv_kernel(m_ref, g_ref, w_ref, kg_ref, k_ref, v_ref):
    hm = _rms(m_ref[0], g_ref[...]).astype(BF16)
    kv = _dot(hm, w_ref[...])
    width = MEM_HEADS * HEAD_DIM
    for h in range(MEM_HEADS):
        cols = slice(h * HEAD_DIM, (h + 1) * HEAD_DIM)
        k_ref[0, :, cols] = _rms(kv[:, cols], kg_ref[...]).astype(BF16)
    v_ref[0] = kv[:, width:].astype(BF16)


def _memkv(mem, gain, wkv, k_gain, l):
    b, m, d = mem.shape
    width = MEM_HEADS * HEAD_DIM
    out = jax.ShapeDtypeStruct((b, m, width), BF16)
    return pl.pallas_call(
        _memkv_kernel,
        out_shape=(out, out),
        grid=(b,),
        in_specs=[
            pl.BlockSpec((1, m, d), lambda i: (i, 0, 0)),
            pl.BlockSpec((1, d), lambda i: (0, 0)),
            pl.BlockSpec((None, d, 2 * width), lambda i: (l, 0, 0)),
            pl.BlockSpec((1, HEAD_DIM), lambda i: (0, 0)),
        ],
        out_specs=(pl.BlockSpec((1, m, width), lambda i: (i, 0, 0)), pl.BlockSpec((1, m, width), lambda i: (i, 0, 0))),
        compiler_params=_params(("parallel",)),
        name="memkv",
    )(mem, gain.reshape(1, d), wkv, k_gain.reshape(1, HEAD_DIM))


def _mixout_kernel(x_ref, yc_ref, ym_ref, yd_ref, w_ref, g_ref, wq_ref, qg_ref, k_ref, v_ref, wo_ref, o_ref, ob_ref):
    c0 = CONV_CH
    c1 = c0 + MOBA_HEADS * HEAD_DIM
    x1 = (x_ref[0] + _dot(yc_ref[0], w_ref[0:c0, :]) + _dot(ym_ref[0], w_ref[c0:c1, :])
          + _dot(yd_ref[0], w_ref[c1:, :]))
    q = _dot(_rms(x1, g_ref[...]).astype(BF16), wq_ref[...])
    mean_mat = _group_mean_matrix(HEAD_DIM, HEAD_DIM)
    for h in range(MEM_HEADS):
        cols = slice(h * HEAD_DIM, (h + 1) * HEAD_DIM)
        qh = (_group_rms(q[:, cols], qg_ref[...], mean_mat) * (HEAD_DIM ** -0.5 * LOG2E)).astype(BF16)
        s = _dot_nt(qh, k_ref[0, :, cols])
        p = jnp.exp2(s - jnp.max(s, axis=-1, keepdims=True))
        oh = _dot(p.astype(BF16), v_ref[0, :, cols]) / jnp.sum(p, axis=-1, keepdims=True)
        ob_ref[:, cols] = oh.astype(BF16)
    o_ref[0] = x1 + _dot(ob_ref[...], wo_ref[...])


def _mixout(x, yc, ym, yd, w, gain, wq, q_gain, kn, v, wo, l, tm=512):
    b, s, d = x.shape
    m = kn.shape[1]
    width = MEM_HEADS * HEAD_DIM
    once = pl.Buffered(1)
    rows = lambda a: pl.BlockSpec((1, tm, a.shape[2]), lambda bi, i: (bi, i, 0))
    return pl.pallas_call(
        _mixout_kernel,
        out_shape=jax.ShapeDtypeStruct((b, s, d), F32),
        grid=(b, s // tm),
        in_specs=[
            rows(x), rows(yc), rows(ym), rows(yd),
            pl.BlockSpec((None, w.shape[1], d), lambda bi, i: (l, 0, 0), pipeline_mode=once),
            pl.BlockSpec((1, d), lambda bi, i: (0, 0)),
            pl.BlockSpec((None, d, width), lambda bi, i: (l, 0, 0), pipeline_mode=once),
            pl.BlockSpec((1, HEAD_DIM), lambda bi, i: (0, 0)),
            pl.BlockSpec((1, m, width), lambda bi, i: (bi, 0, 0)),
            pl.BlockSpec((1, m, width), lambda bi, i: (bi, 0, 0)),
            pl.BlockSpec((None, width, d), lambda bi, i: (l, 0, 0), pipeline_mode=once),
        ],
        out_specs=rows(x),
        scratch_shapes=[pltpu.VMEM((tm, width), BF16)],
        compiler_params=_params(("parallel", "parallel")),
        name="mixout",
    )(x, yc, ym, yd, w, gain.reshape(1, d), wq, q_gain.reshape(1, HEAD_DIM), kn, v, wo)


CAST_BLOCK_BYTES = 6 * 1024 * 1024


def _cast_kernel(w_ref, o_ref):
    o_ref[...] = w_ref[...].astype(o_ref.dtype)


def _cast_bf16(w, layer=None):
    depth, k, n = w.shape
    first, count = (0, depth) if layer is None else (layer, 1)
    bk = max(r for r in range(BF16_SUBLANES, k + 1, BF16_SUBLANES) if k % r == 0 and r * n * 4 <= CAST_BLOCK_BYTES)
    return pl.pallas_call(
        _cast_kernel,
        out_shape=jax.ShapeDtypeStruct((count, k, n), BF16),
        grid=(count, k // bk),
        in_specs=[pl.BlockSpec((None, bk, n), lambda l, i: (first + l, i, 0))],
        out_specs=pl.BlockSpec((None, bk, n), lambda l, i: (l, i, 0)),
        compiler_params=_params(("parallel", "parallel")),
        name="cast_bf16",
    )(w)


def kernel(x, mem, rel_bias, ffn1_norm, ffn1_w_gate, ffn1_w_up, ffn1_w_down, mix_norm, w_in, conv_w, conv_b, conv_ln_g, conv_ln_b, moba_q_norm, moba_k_norm, diff_q_norm, diff_k_norm, diff_lambda, diff_subln, w_out, mem_norm_x, mem_norm_m, mem_wq, mem_wkv, mem_q_norm, mem_k_norm, mem_wo, ffn2_norm, ffn2_w_gate, ffn2_w_up, ffn2_w_down):
    b, s, d = x.shape
    depth = w_in.shape[0]
    t = b * s
    tab_flat = rel_bias.T.reshape(-1)
    strips = _bias_strips(tab_flat)
    ffn1_f32 = (ffn1_w_gate, ffn1_w_up, ffn1_w_down)
    ffn2_f32 = (ffn2_w_gate, ffn2_w_up, ffn2_w_down)
    ffn_w, ffn_tf = ffn1_f32, FFN_BLOCK_F32
    w_out, mem_wq, mem_wkv, mem_wo = (_cast_bf16(w) for w in (w_out, mem_wq, mem_wkv, mem_wo))
    xt = x.reshape(t, d)
    for l in range(depth):
        xt, (*ffn_w, w_in_l) = _ffn(xt, ffn1_norm[l], *ffn_w, 0, side=[(w, l) for w in (*ffn2_f32, w_in)], tf=ffn_tf)
        ffn_tf = FFN_BLOCK_BF16
        proj = _proj(xt, mix_norm[l], w_in_l, 0).reshape(b, s, PROJ_WIDTH)
        y_conv = _conv(proj, conv_w[l], conv_b[l], conv_ln_g[l], conv_ln_b[l])
        y_moba = _moba(proj, strips, moba_q_norm[l], moba_k_norm[l])
        lam_init = 0.8 - 0.6 * math.exp(-0.3 * l)
        y_diff = _diff(proj, strips, diff_q_norm[l], diff_k_norm[l], diff_lambda[l], diff_subln[l], lam_init)
        kn, v = _memkv(mem, mem_norm_m[l], mem_wkv, mem_k_norm[l], l)
        xt = _mixout(xt.reshape(b, s, d), y_conv, y_moba, y_diff, w_out, mem_norm_x[l], mem_wq, mem_q_norm[l], kn, v,
                     mem_wo, l).reshape(t, d)
        nxt = [(w, l + 1) for w in ffn1_f32] if l + 1 < depth else []
        xt, ffn_w = _ffn(xt, ffn2_norm[l], *ffn_w, 0, side=nxt, tf=ffn_tf)
    return xt.reshape(b, s, d)
```

```python
import functools
import math

import jax
import jax.numpy as jnp
from jax import lax
from jax.experimental import pallas as pl
from jax.experimental.pallas import tpu as pltpu

F32 = jnp.float32
BF16 = jnp.bfloat16

D_MODEL = 2048
D_FF = 5632
CONV_CH = 512
CONV_WIDTH = 31
CONV_PAD = 32
HEAD_DIM = 128
LANES = 128
BF16_SUBLANES = 16
MOBA_HEADS = 6
MOBA_BLOCK = 256
MOBA_TOPK = 3
DIFF_HEADS = 6
DIFF_QK_DIM = 64
MEM_HEADS = 4
N_BUCKETS = 32
MAX_DISTANCE = 128
N_BIAS_HEADS = MOBA_HEADS + DIFF_HEADS
NEG_INF = -1e30
TAKEN = -3e38
PROJ_WIDTH = 5632
COL_CONV_A = 0
COL_CONV_G = CONV_CH // HEAD_DIM
COL_MOBA_Q = 2 * CONV_CH // HEAD_DIM
COL_MOBA_K = COL_MOBA_Q + MOBA_HEADS
COL_MOBA_V = COL_MOBA_K + MOBA_HEADS
COL_DIFF_Q = COL_MOBA_V + MOBA_HEADS
COL_DIFF_K = COL_DIFF_Q + DIFF_HEADS
COL_DIFF_V = COL_DIFF_K + DIFF_HEADS

ATT_TILE = 256
assert ATT_TILE == MOBA_BLOCK and ATT_TILE >= MAX_DISTANCE
LOG2E = 1.4426950408889634
VMEM_LIMIT = 56 * 1024 * 1024


def _params(semantics, vmem=VMEM_LIMIT):
    return pltpu.CompilerParams(dimension_semantics=semantics, vmem_limit_bytes=vmem)


def _rms(x, gain, eps=1e-6):
    return x * lax.rsqrt(jnp.mean(x * x, axis=-1, keepdims=True) + eps) * gain


def _dot(a, b):
    return jnp.dot(a, b, preferred_element_type=F32)


def _dot_nt(a, b, precision=None):
    return lax.dot_general(a, b, (((1,), (1,)), ((), ())), precision=precision, preferred_element_type=F32)


def _ffn_kernel(n_side, x_hbm, g_ref, wg_ref, wu_ref, wd_ref, *refs):
    side_in, o_ref, side_out = refs[:n_side], refs[n_side], refs[n_side + 1:2 * n_side + 1]
    h_ref, x_ref, x_sem = refs[2 * n_side + 1:]
    i, k = pl.program_id(0), pl.program_id(1)
    tm = x_ref.shape[0]

    def x_copy(block):
        return pltpu.make_async_copy(x_hbm.at[pl.ds(pl.multiple_of(block * tm, tm), tm), :], x_ref, x_sem)

    @pl.when(jnp.logical_and(i == 0, k == 0))
    def _():
        x_copy(0).start()

    @pl.when(k == 0)
    def _():
        x_copy(i).wait()
        x = x_ref[...]
        h_ref[...] = _rms(x, g_ref[...]).astype(BF16)
        o_ref[...] = x

    @pl.when(jnp.logical_and(k == 1, i + 1 < pl.num_programs(0)))
    def _():
        x_copy(i + 1).start()

    h = h_ref[...]
    gate = _dot(h, wg_ref[...].astype(BF16))
    up = _dot(h, wu_ref[...].astype(BF16))
    act = (0.5 * gate / (1.0 + jnp.exp(-gate)) * up).astype(BF16)
    o_ref[...] += _dot(act, wd_ref[...].astype(BF16))
    for w_ref, wb_ref in zip(side_in, side_out):
        wb_ref[...] = w_ref[...].astype(BF16)


def _side_block(shape, gi, gk):
    k, n = shape
    if k % gi == 0 and n % gk == 0 and (k // gi) % BF16_SUBLANES == 0 and (n // gk) % LANES == 0:
        return (k // gi, n // gk), (lambda i, kk: (i, kk))
    assert k % gk == 0 and n % gi == 0 and (k // gk) % BF16_SUBLANES == 0 and (n // gi) % LANES == 0, shape
    return (k // gk, n // gi), (lambda i, kk: (kk, i))


FFN_BLOCK_BF16 = 512
FFN_BLOCK_F32 = 256


def _ffn(x, gain, wg, wu, wd, l, side=(), tm=1024, tf=FFN_BLOCK_BF16):
    t, d = x.shape
    f = wg.shape[2]
    gi, gk = t // tm, f // tf
    side_in_specs, side_out_specs, side_shapes = [], [], []
    for w, ls in side:
        blk, order = _side_block(w.shape[1:], gi, gk)
        side_in_specs.append(pl.BlockSpec((None, *blk), lambda i, k, ls=ls, order=order: (ls, *order(i, k))))
        side_out_specs.append(pl.BlockSpec((None, *blk), lambda i, k, order=order: (0, *order(i, k))))
        side_shapes.append(jax.ShapeDtypeStruct((1, *w.shape[1:]), BF16))
    out, *cast = pl.pallas_call(
        functools.partial(_ffn_kernel, len(side)),
        out_shape=[jax.ShapeDtypeStruct((t, d), F32), *side_shapes],
        grid=(gi, gk),
        in_specs=[
            pl.BlockSpec(memory_space=pl.ANY),
            pl.BlockSpec((1, d), lambda i, k: (0, 0)),
            pl.BlockSpec((None, d, tf), lambda i, k: (l, 0, k)),
            pl.BlockSpec((None, d, tf), lambda i, k: (l, 0, k)),
            pl.BlockSpec((None, tf, d), lambda i, k: (l, k, 0)),
            *side_in_specs,
        ],
        out_specs=[pl.BlockSpec((tm, d), lambda i, k: (i, 0)), *side_out_specs],
        scratch_shapes=[pltpu.VMEM((tm, d), BF16), pltpu.VMEM((tm, d), F32), pltpu.SemaphoreType.DMA(())],
        compiler_params=_params(("arbitrary", "arbitrary")),
        name="ffn",
    )(x, gain.reshape(1, d), wg, wu, wd, *(w for w, _ in side))
    return out, cast


def _proj_kernel(x_ref, g_ref, w_ref, o_ref, h_ref):
    @pl.when(pl.program_id(1) == 0)
    def _():
        h_ref[...] = _rms(x_ref[...], g_ref[...]).astype(BF16)

    o_ref[...] = _dot(h_ref[...], w_ref[...]).astype(o_ref.dtype)


def _proj(x, gain, w, l, tm=2048, tn=512):
    t, d = x.shape
    n = w.shape[2]
    return pl.pallas_call(
        _proj_kernel,
        out_shape=jax.ShapeDtypeStruct((t, n), BF16),
        grid=(t // tm, n // tn),
        in_specs=[
            pl.BlockSpec((tm, d), lambda i, j: (i, 0)),
            pl.BlockSpec((1, d), lambda i, j: (0, 0)),
            pl.BlockSpec((None, d, tn), lambda i, j: (l, 0, j)),
        ],
        out_specs=pl.BlockSpec((tm, tn), lambda i, j: (i, j)),
        scratch_shapes=[pltpu.VMEM((tm, d), BF16)],
        compiler_params=_params(("parallel", "arbitrary")),
        name="proj",
    )(x, gain.reshape(1, d), w)


def _bias_kernel(tab_ref, o_ref):
    h = pl.program_id(0)
    far = tab_ref[h * N_BUCKETS + N_BUCKETS - 1]
    shape = (ATT_TILE, 2 * ATT_TILE)
    dist = ATT_TILE + lax.broadcasted_iota(jnp.int32, shape, 0) - lax.broadcasted_iota(jnp.int32, shape, 1)
    dist = jnp.maximum(dist, 0)
    max_exact = N_BUCKETS // 2
    log_ratio = jnp.log(jnp.maximum(dist, 1).astype(F32) / max_exact) / math.log(MAX_DISTANCE / max_exact)
    large = jnp.minimum(max_exact + (log_ratio * (N_BUCKETS - max_exact)).astype(jnp.int32), N_BUCKETS - 1)
    bucket = jnp.where(dist < max_exact, dist, large)
    out = jnp.zeros(shape, F32)
    for b in range(N_BUCKETS):
        out = jnp.where(bucket == b, (tab_ref[h * N_BUCKETS + b] - far) * LOG2E, out)
    o_ref[0] = out


def _bias_strips(tab_flat):
    return pl.pallas_call(
        _bias_kernel,
        out_shape=jax.ShapeDtypeStruct((N_BIAS_HEADS, ATT_TILE, 2 * ATT_TILE), F32),
        grid=(N_BIAS_HEADS,),
        in_specs=[pl.BlockSpec(memory_space=pltpu.SMEM)],
        out_specs=pl.BlockSpec((1, ATT_TILE, 2 * ATT_TILE), lambda h: (h, 0, 0)),
        compiler_params=_params(("arbitrary",)),
        name="bias_strips",
    )(tab_flat)


CONV_ROWS = 32
CONV_UNROLL = 4
SUBLANES = 8
assert CONV_PAD % SUBLANES == 0 and CONV_PAD >= SUBLANES * -(-CONV_WIDTH // SUBLANES)


def _conv_kernel(n_side, a_ref, g_ref, w_ref, cb_ref, lg_ref, lb_ref, *refs):
    side_in, o_ref, side_out, u_ref = refs[:n_side], refs[n_side], refs[n_side + 1:2 * n_side + 1], refs[-1]
    for wf_ref, wb_ref in zip(side_in, side_out):
        wb_ref[...] = wf_ref[...].astype(BF16)
    s = a_ref.shape[1]
    u_ref[0:CONV_PAD, :] = jnp.zeros((CONV_PAD, CONV_CH), F32)

    def glu(r, carry):
        rows = pl.ds(pl.multiple_of(r * CONV_ROWS, CONV_ROWS), CONV_ROWS)
        a = a_ref[0, rows, :].astype(F32)
        g = g_ref[0, rows, :].astype(F32)
        u_ref[pl.ds(pl.multiple_of(CONV_PAD + r * CONV_ROWS, CONV_ROWS), CONV_ROWS), :] = a / (1.0 + jnp.exp(-g))
        return carry

    lax.fori_loop(0, s // CONV_ROWS, glu, 0)

    win_rows = CONV_ROWS + CONV_PAD
    n_groups = CONV_CH // LANES

    def conv(r, carry):
        base = pl.multiple_of(r * CONV_ROWS, CONV_ROWS)
        accs = []
        for c in range(n_groups):
            lanes = slice(c * LANES, (c + 1) * LANES)
            win = u_ref[pl.ds(base, win_rows), lanes]
            acc = jnp.zeros((CONV_ROWS, LANES), F32) + cb_ref[:, lanes]
            for b in range(SUBLANES):
                shifted = pltpu.roll(win, b, 0) if b else win
                for a in range(-(-CONV_WIDTH // SUBLANES)):
                    d = SUBLANES * a + b
                    if d < CONV_WIDTH:
                        k0 = CONV_PAD - SUBLANES * a
                        tap = CONV_WIDTH - 1 - d
                        acc = acc + shifted[k0:k0 + CONV_ROWS, :] * w_ref[tap:tap + 1, lanes]
            accs.append(acc)
        mu = sum(jnp.sum(acc, axis=-1, keepdims=True) for acc in accs) / float(CONV_CH)
        cens = [acc - mu for acc in accs]
        var = sum(jnp.sum(cen * cen, axis=-1, keepdims=True) for cen in cens) / float(CONV_CH)
        inv = lax.rsqrt(var + 1e-5)
        for c, cen in enumerate(cens):
            lanes = slice(c * LANES, (c + 1) * LANES)
            y = cen * inv * lg_ref[:, lanes] + lb_ref[:, lanes]
            o_ref[0, pl.ds(base, CONV_ROWS), lanes] = (y / (1.0 + jnp.exp(-y))).astype(o_ref.dtype)
        return carry

    lax.fori_loop(0, s // CONV_ROWS, conv, 0, unroll=CONV_UNROLL)


def _conv(proj, w, cb, lg, lb, side=()):
    b, s, _ = proj.shape
    vec = lambda v: v.reshape(1, CONV_CH)
    side_in_specs, side_out_specs, side_shapes = [], [], []
    for wf, ls in side:
        k, n = wf.shape[1:]
        assert k % b == 0 and (k // b) % BF16_SUBLANES == 0, wf.shape
        side_in_specs.append(pl.BlockSpec((None, k // b, n), lambda i, ls=ls: (ls, i, 0)))
        side_out_specs.append(pl.BlockSpec((None, k // b, n), lambda i: (0, i, 0)))
        side_shapes.append(jax.ShapeDtypeStruct((1, k, n), BF16))
    out, *cast = pl.pallas_call(
        functools.partial(_conv_kernel, len(side)),
        out_shape=[jax.ShapeDtypeStruct((b, s, CONV_CH), BF16), *side_shapes],
        grid=(b,),
        in_specs=[
            pl.BlockSpec((1, s, CONV_CH), lambda i: (i, 0, 0)),
            pl.BlockSpec((1, s, CONV_CH), lambda i: (i, 0, 1)),
            pl.BlockSpec((CONV_WIDTH, CONV_CH), lambda i: (0, 0)),
            pl.BlockSpec((1, CONV_CH), lambda i: (0, 0)),
            pl.BlockSpec((1, CONV_CH), lambda i: (0, 0)),
            pl.BlockSpec((1, CONV_CH), lambda i: (0, 0)),
            *side_in_specs,
        ],
        out_specs=[pl.BlockSpec((1, s, CONV_CH), lambda i: (i, 0, 0)), *side_out_specs],
        scratch_shapes=[pltpu.VMEM((s + CONV_PAD, CONV_CH), F32)],
        compiler_params=_params(("parallel",)),
        name="conv",
    )(proj, proj, w, vec(cb), vec(lg), vec(lb), *(wf for wf, _ in side))
    return out, cast


def _tile_rows(j):
    return slice(j * ATT_TILE, (j + 1) * ATT_TILE)


def _lane_fold(op, x):
    out = x[:, :LANES]
    for c in range(1, x.shape[1] // LANES):
        out = op(out, x[:, c * LANES:(c + 1) * LANES])
    return out


def _group_mean_matrix(width, group):
    r = lax.broadcasted_iota(jnp.int32, (width, width), 0) // group
    c = lax.broadcasted_iota(jnp.int32, (width, width), 1) // group
    return jnp.where(r == c, 1.0 / group, 0.0).astype(BF16)


def _group_rms(x, gain, mean_mat, eps=1e-6):
    ms = _dot((x * x).astype(BF16), mean_mat)
    return x * lax.rsqrt(ms + eps) * gain


def _ones_columns(v_ref, vo_ref):
    vo_ref[:, :HEAD_DIM] = v_ref[...]
    vo_ref[:, HEAD_DIM:] = jnp.ones((v_ref.shape[0], HEAD_DIM), BF16)


def _logits_pass(i, q, k_ref, strip_ref, sel_add, s_ref):
    maps = q.shape[0] // ATT_TILE
    shape = (q.shape[0], ATT_TILE)
    mask = lax.broadcasted_iota(jnp.int32, shape, 0) % ATT_TILE >= lax.broadcasted_iota(jnp.int32, shape, 1)
    rep = lambda x: x if maps == 1 else jnp.concatenate([x] * maps, axis=0)
    m_acc = None
    for j in range(i + 1):
        s = _dot_nt(q, k_ref[_tile_rows(j), :])
        if j == i:
            s = jnp.where(mask, s + rep(strip_ref[0, :, ATT_TILE:]), NEG_INF)
        elif j == i - 1:
            s = s + rep(strip_ref[0, :, :ATT_TILE])
        if sel_add is not None and j < i:
            s = s + sel_add[:, j:j + 1]
        s_ref[:, _tile_rows(j)] = s
        t = _lane_fold(jnp.maximum, s)
        m_acc = t if m_acc is None else jnp.maximum(m_acc, t)
    return jnp.max(m_acc, axis=-1, keepdims=True)


def _softmax_pass(i, m, vo_ref, s_ref, p_ref):
    for j in range(i + 1):
        p_ref[:, _tile_rows(j)] = jnp.exp2(s_ref[:, _tile_rows(j)] - m).astype(BF16)
    n = (i + 1) * ATT_TILE
    out = _dot(p_ref[:, :n], vo_ref[:n, :])
    return out[:, :HEAD_DIM], out[:, HEAD_DIM:]


def _moba_kernel(q_ref, k_ref, v_ref, qg_ref, kg_ref, strip_ref, o_ref, kn_ref, vo_ref, kmean_ref, s_ref, p_ref):
    nb = k_ref.shape[1] // MOBA_BLOCK
    mean_mat = _group_mean_matrix(HEAD_DIM, HEAD_DIM)
    _ones_columns(v_ref.at[0], vo_ref)
    for j in range(nb):
        rows = _tile_rows(j)
        kn = _group_rms(k_ref[0, rows, :].astype(F32), kg_ref[...], mean_mat)
        kn_ref[rows, :] = kn.astype(BF16)
        kmean_ref[j:j + 1, :] = jnp.sum(kn, axis=0, keepdims=True) / float(MOBA_BLOCK)

    eye = (lax.broadcasted_iota(jnp.int32, (MOBA_BLOCK, MOBA_BLOCK), 0)
           == lax.broadcasted_iota(jnp.int32, (MOBA_BLOCK, MOBA_BLOCK), 1)).astype(F32).astype(BF16)
    n_sel = min(MOBA_TOPK, nb)

    def logits(i):
        qn = _group_rms(q_ref[0, _tile_rows(i), :].astype(F32), qg_ref[...], mean_mat)
        sel_add = None
        if i > n_sel:
            gate = _dot_nt(kmean_ref[...], qn, precision=lax.Precision.HIGHEST)
            blk = lax.broadcasted_iota(jnp.int32, gate.shape, 0)
            gate = jnp.where(blk < i, gate, TAKEN)
            sel = jnp.zeros(gate.shape, F32)
            for _ in range(n_sel):
                top = jnp.max(gate, axis=0, keepdims=True)
                first = jnp.min(jnp.where(gate == top, blk, nb), axis=0, keepdims=True)
                pick = blk == first
                sel = jnp.where(pick, 1.0, sel)
                gate = jnp.where(pick, TAKEN, gate)
            sel_t = _dot_nt(eye, sel.astype(BF16))
            sel_add = (sel_t - 1.0) * -NEG_INF
        q = (qn * (HEAD_DIM ** -0.5 * LOG2E)).astype(BF16)
        return _logits_pass(i, q, kn_ref, strip_ref, sel_add, s_ref.at[i % 2])

    m_next = logits(0)
    for i in range(nb):
        m = m_next
        if i + 1 < nb:
            m_next = logits(i + 1)
        acc, l = _softmax_pass(i, m, vo_ref, s_ref.at[i % 2], p_ref.at[i % 2])
        o_ref[0, _tile_rows(i), :] = (acc / l).astype(o_ref.dtype)


def _moba(proj, strips, q_gain, k_gain):
    b, s, _ = proj.shape
    nb = s // MOBA_BLOCK
    return pl.pallas_call(
        _moba_kernel,
        out_shape=jax.ShapeDtypeStruct((b, s, MOBA_HEADS * HEAD_DIM), BF16),
        grid=(b, MOBA_HEADS),
        in_specs=[
            pl.BlockSpec((1, s, HEAD_DIM), lambda bi, h: (bi, 0, COL_MOBA_Q + h)),
            pl.BlockSpec((1, s, HEAD_DIM), lambda bi, h: (bi, 0, COL_MOBA_K + h)),
            pl.BlockSpec((1, s, HEAD_DIM), lambda bi, h: (bi, 0, COL_MOBA_V + h)),
            pl.BlockSpec((1, HEAD_DIM), lambda bi, h: (0, 0)),
            pl.BlockSpec((1, HEAD_DIM), lambda bi, h: (0, 0)),
            pl.BlockSpec((1, ATT_TILE, 2 * ATT_TILE), lambda bi, h: (h, 0, 0)),
        ],
        out_specs=pl.BlockSpec((1, s, HEAD_DIM), lambda bi, h: (bi, 0, h)),
        scratch_shapes=[
            pltpu.VMEM((s, HEAD_DIM), BF16),
            pltpu.VMEM((s, 2 * HEAD_DIM), BF16),
            pltpu.VMEM((nb, HEAD_DIM), F32),
            pltpu.VMEM((2, ATT_TILE, s), F32),
            pltpu.VMEM((2, ATT_TILE, s), BF16),
        ],
        compiler_params=_params(("parallel", "parallel")),
        name="moba",
    )(proj, proj, proj, q_gain.reshape(1, HEAD_DIM), k_gain.reshape(1, HEAD_DIM), strips)


DIFF_MAPS = 2


def _diff_kernel(lam_init, q_ref, k_ref, v_ref, qg_ref, kg_ref, lam_ref, sg_ref, strip_ref, o_ref,
                 kn_ref, vo_ref, s_ref, p_ref):
    nt = k_ref.shape[1] // ATT_TILE
    mean_mat = _group_mean_matrix(HEAD_DIM, DIFF_QK_DIM)
    _ones_columns(v_ref.at[0], vo_ref)
    for j in range(nt):
        rows = _tile_rows(j)
        kn_ref[rows, :] = _group_rms(k_ref[0, rows, :].astype(F32), kg_ref[...], mean_mat).astype(BF16)

    lp = lam_ref[...]
    lam = (jnp.exp(jnp.sum(lp[0:1] * lp[1:2], axis=-1, keepdims=True))
           - jnp.exp(jnp.sum(lp[2:3] * lp[3:4], axis=-1, keepdims=True)) + lam_init)
    scale = DIFF_QK_DIM ** -0.5 * LOG2E
    lane_map = lax.broadcasted_iota(jnp.int32, (ATT_TILE, HEAD_DIM), 1) // DIFF_QK_DIM

    def logits(i):
        qn = _group_rms(q_ref[0, _tile_rows(i), :].astype(F32), qg_ref[...], mean_mat) * scale
        q = jnp.concatenate([jnp.where(lane_map == c, qn, 0.0) for c in range(DIFF_MAPS)], axis=0).astype(BF16)
        return _logits_pass(i, q, kn_ref, strip_ref, None, s_ref.at[i % 2])

    m_next = logits(0)
    for i in range(nt):
        m = m_next
        if i + 1 < nt:
            m_next = logits(i + 1)
        acc, l = _softmax_pass(i, m, vo_ref, s_ref.at[i % 2], p_ref.at[i % 2])
        a = acc / l
        o = a[:ATT_TILE] - lam * a[ATT_TILE:]
        o_ref[0, _tile_rows(i), :] = (_rms(o, sg_ref[...]) * (1.0 - lam_init)).astype(o_ref.dtype)


def _diff(proj, strips, q_gain, k_gain, lam_params, subln, lam_init):
    b, s, _ = proj.shape
    both = lambda g: jnp.tile(g, DIFF_MAPS).reshape(1, HEAD_DIM)
    return pl.pallas_call(
        functools.partial(_diff_kernel, lam_init),
        out_shape=jax.ShapeDtypeStruct((b, s, DIFF_HEADS * HEAD_DIM), BF16),
        grid=(b, DIFF_HEADS),
        in_specs=[
            pl.BlockSpec((1, s, HEAD_DIM), lambda bi, h: (bi, 0, COL_DIFF_Q + h)),
            pl.BlockSpec((1, s, HEAD_DIM), lambda bi, h: (bi, 0, COL_DIFF_K + h)),
            pl.BlockSpec((1, s, HEAD_DIM), lambda bi, h: (bi, 0, COL_DIFF_V + h)),
            pl.BlockSpec((1, HEAD_DIM), lambda bi, h: (0, 0)),
            pl.BlockSpec((1, HEAD_DIM), lambda bi, h: (0, 0)),
            pl.BlockSpec((4, DIFF_QK_DIM), lambda bi, h: (0, 0)),
            pl.BlockSpec((1, HEAD_DIM), lambda bi, h: (0, 0)),
            pl.BlockSpec((1, ATT_TILE, 2 * ATT_TILE), lambda bi, h: (MOBA_HEADS + h, 0, 0)),
        ],
        out_specs=pl.BlockSpec((1, s, HEAD_DIM), lambda bi, h: (bi, 0, h)),
        scratch_shapes=[
            pltpu.VMEM((s, HEAD_DIM), BF16),
            pltpu.VMEM((s, 2 * HEAD_DIM), BF16),
            pltpu.VMEM((2, DIFF_MAPS * ATT_TILE, s), F32),
            pltpu.VMEM((2, DIFF_MAPS * ATT_TILE, s), BF16),
        ],
        compiler_params=_params(("parallel", "parallel")),
        name="diff",
    )(proj, proj, proj, both(q_gain), both(k_gain), lam_params, subln.reshape(1, HEAD_DIM), strips)


def _memkv_kernel(m_ref, g_ref, w_ref, kg_ref, k_ref, v_ref):
    hm = _rms(m_ref[0], g_ref[...]).astype(BF16)
    kv = _dot(hm, w_ref[...])
    width = MEM_HEADS * HEAD_DIM
    for h in range(MEM_HEADS):
        cols = slice(h * HEAD_DIM, (h + 1) * HEAD_DIM)
        k_ref[0, :, cols] = _rms(kv[:, cols], kg_ref[...]).astype(BF16)
    v_ref[0] = kv[:, width:].astype(BF16)


def _memkv(mem, gain, wkv, k_gain, l):
    b, m, d = mem.shape
    width = MEM_HEADS * HEAD_DIM
    out = jax.ShapeDtypeStruct((b, m, width), BF16)
    return pl.pallas_call(
        _memkv_kernel,
        out_shape=(out, out),
        grid=(b,),
        in_specs=[
            pl.BlockSpec((1, m, d), lambda i: (i, 0, 0)),
            pl.BlockSpec((1, d), lambda i: (0, 0)),
            pl.BlockSpec((None, d, 2 * width), lambda i: (l, 0, 0)),
            pl.BlockSpec((1, HEAD_DIM), lambda i: (0, 0)),
        ],
        out_specs=(pl.BlockSpec((1, m, width), lambda i: (i, 0, 0)), pl.BlockSpec((1, m, width), lambda i: (i, 0, 0))),
        compiler_params=_params(("parallel",)),
        name="memkv",
    )(mem, gain.reshape(1, d), wkv, k_gain.reshape(1, HEAD_DIM))


def _mixout_kernel(x_ref, yc_ref, ym_ref, yd_ref, w_ref, g_ref, wq_ref, qg_ref, k_ref, v_ref, wo_ref, o_ref, ob_ref):
    c0 = CONV_CH
    c1 = c0 + MOBA_HEADS * HEAD_DIM
    x1 = (x_ref[0] + _dot(yc_ref[0], w_ref[0:c0, :]) + _dot(ym_ref[0], w_ref[c0:c1, :])
          + _dot(yd_ref[0], w_ref[c1:, :]))
    q = _dot(_rms(x1, g_ref[...]).astype(BF16), wq_ref[...])
    mean_mat = _group_mean_matrix(HEAD_DIM, HEAD_DIM)
    for h in range(MEM_HEADS):
        cols = slice(h * HEAD_DIM, (h + 1) * HEAD_DIM)
        qh = (_group_rms(q[:, cols], qg_ref[...], mean_mat) * (HEAD_DIM ** -0.5 * LOG2E)).astype(BF16)
        s = _dot_nt(qh, k_ref[0, :, cols])
        p = jnp.exp2(s - jnp.max(s, axis=-1, keepdims=True))
        oh = _dot(p.astype(BF16), v_ref[0, :, cols]) / jnp.sum(p, axis=-1, keepdims=True)
        ob_ref[:, cols] = oh.astype(BF16)
    o_ref[0] = x1 + _dot(ob_ref[...], wo_ref[...])


def _mixout(x, yc, ym, yd, w, gain, wq, q_gain, kn, v, wo, l, tm=512):
    b, s, d = x.shape
    m = kn.shape[1]
    width = MEM_HEADS * HEAD_DIM
    once = pl.Buffered(1)
    rows = lambda a: pl.BlockSpec((1, tm, a.shape[2]), lambda bi, i: (bi, i, 0))
    return pl.pallas_call(
        _mixout_kernel,
        out_shape=jax.ShapeDtypeStruct((b, s, d), F32),
        grid=(b, s // tm),
        in_specs=[
            rows(x), rows(yc), rows(ym), rows(yd),
            pl.BlockSpec((None, w.shape[1], d), lambda bi, i: (l, 0, 0), pipeline_mode=once),
            pl.BlockSpec((1, d), lambda bi, i: (0, 0)),
            pl.BlockSpec((None, d, width), lambda bi, i: (l, 0, 0), pipeline_mode=once),
            pl.BlockSpec((1, HEAD_DIM), lambda bi, i: (0, 0)),
            pl.BlockSpec((1, m, width), lambda bi, i: (bi, 0, 0)),
            pl.BlockSpec((1, m, width), lambda bi, i: (bi, 0, 0)),
            pl.BlockSpec((None, width, d), lambda bi, i: (l, 0, 0), pipeline_mode=once),
        ],
        out_specs=rows(x),
        scratch_shapes=[pltpu.VMEM((tm, width), BF16)],
        compiler_params=_params(("parallel", "parallel")),
        name="mixout",
    )(x, yc, ym, yd, w, gain.reshape(1, d), wq, q_gain.reshape(1, HEAD_DIM), kn, v, wo)


def kernel(x, mem, rel_bias, ffn1_norm, ffn1_w_gate, ffn1_w_up, ffn1_w_down, mix_norm, w_in, conv_w, conv_b, conv_ln_g, conv_ln_b, moba_q_norm, moba_k_norm, diff_q_norm, diff_k_norm, diff_lambda, diff_subln, w_out, mem_norm_x, mem_norm_m, mem_wq, mem_wkv, mem_q_norm, mem_k_norm, mem_wo, ffn2_norm, ffn2_w_gate, ffn2_w_up, ffn2_w_down):
    b, s, d = x.shape
    depth = w_in.shape[0]
    t = b * s
    tab_flat = rel_bias.T.reshape(-1)
    strips = _bias_strips(tab_flat)
    ffn1_f32 = (ffn1_w_gate, ffn1_w_up, ffn1_w_down)
    ffn2_f32 = (ffn2_w_gate, ffn2_w_up, ffn2_w_down)
    ffn_w, ffn_tf = ffn1_f32, FFN_BLOCK_F32
    xt = x.reshape(t, d)
    for l in range(depth):
        xt, (*ffn_w, w_in_l) = _ffn(xt, ffn1_norm[l], *ffn_w, 0, side=[(w, l) for w in (*ffn2_f32, w_in)], tf=ffn_tf)
        ffn_tf = FFN_BLOCK_BF16
        proj = _proj(xt, mix_norm[l], w_in_l, 0).reshape(b, s, PROJ_WIDTH)
        y_conv, (w_out_l, wq_l, wkv_l, wo_l) = _conv(proj, conv_w[l], conv_b[l], conv_ln_g[l], conv_ln_b[l],
                                                    side=[(w, l) for w in (w_out, mem_wq, mem_wkv, mem_wo)])
        y_moba = _moba(proj, strips, moba_q_norm[l], moba_k_norm[l])
        lam_init = 0.8 - 0.6 * math.exp(-0.3 * l)
        y_diff = _diff(proj, strips, diff_q_norm[l], diff_k_norm[l], diff_lambda[l], diff_subln[l], lam_init)
        kn, v = _memkv(mem, mem_norm_m[l], wkv_l, mem_k_norm[l], 0)
        xt = _mixout(xt.reshape(b, s, d), y_conv, y_moba, y_diff, w_out_l, mem_norm_x[l], wq_l, mem_q_norm[l], kn, v,
                     wo_l, 0).reshape(t, d)
        nxt = [(w, l + 1) for w in ffn1_f32] if l + 1 < depth else []
        xt, ffn_w = _ffn(xt, ffn2_norm[l], *ffn_w, 0, side=nxt, tf=ffn_tf)
    return xt.reshape(b, s, d)
```

```python
import functools
import math

import jax
import jax.numpy as jnp
from jax import lax
from jax.experimental import pallas as pl
from jax.experimental.pallas import tpu as pltpu

F32 = jnp.float32
BF16 = jnp.bfloat16

D_MODEL = 2048
D_FF = 5632
CONV_CH = 512
CONV_WIDTH = 31
CONV_PAD = 32
HEAD_DIM = 128
LANES = 128
BF16_SUBLANES = 16
MOBA_HEADS = 6
MOBA_BLOCK = 256
MOBA_TOPK = 3
DIFF_HEADS = 6
DIFF_QK_DIM = 64
MEM_HEADS = 4
N_BUCKETS = 32
MAX_DISTANCE = 128
N_BIAS_HEADS = MOBA_HEADS + DIFF_HEADS
NEG_INF = -1e30
TAKEN = -3e38
PROJ_WIDTH = 5632
PROJ_BLOCK = 512
COL_CONV_A = 0
COL_CONV_G = CONV_CH // HEAD_DIM
COL_MOBA_Q = 2 * CONV_CH // HEAD_DIM
COL_MOBA_K = COL_MOBA_Q + MOBA_HEADS
COL_MOBA_V = COL_MOBA_K + MOBA_HEADS
COL_DIFF_Q = COL_MOBA_V + MOBA_HEADS
COL_DIFF_K = COL_DIFF_Q + DIFF_HEADS
COL_DIFF_V = COL_DIFF_K + DIFF_HEADS

ATT_TILE = 256
assert ATT_TILE == MOBA_BLOCK and ATT_TILE >= MAX_DISTANCE
LOG2E = 1.4426950408889634
VMEM_LIMIT = 56 * 1024 * 1024


def _params(semantics, vmem=VMEM_LIMIT):
    return pltpu.CompilerParams(dimension_semantics=semantics, vmem_limit_bytes=vmem)


def _rms(x, gain, eps=1e-6):
    return x * lax.rsqrt(jnp.mean(x * x, axis=-1, keepdims=True) + eps) * gain


def _dot(a, b):
    return jnp.dot(a, b, preferred_element_type=F32)


def _dot_nt(a, b, precision=None):
    return lax.dot_general(a, b, (((1,), (1,)), ((), ())), precision=precision, preferred_element_type=F32)


def _ffn_kernel(n_side, x_hbm, g_ref, wg_ref, wu_ref, wd_ref, *refs):
    side_in, o_ref, side_out = refs[:n_side], refs[n_side], refs[n_side + 1:2 * n_side + 1]
    h_ref, x_ref, x_sem = refs[2 * n_side + 1:]
    i, k = pl.program_id(0), pl.program_id(1)
    tm = x_ref.shape[0]

    def x_copy(block):
        return pltpu.make_async_copy(x_hbm.at[pl.ds(pl.multiple_of(block * tm, tm), tm), :], x_ref, x_sem)

    @pl.when(jnp.logical_and(i == 0, k == 0))
    def _():
        x_copy(0).start()

    @pl.when(k == 0)
    def _():
        x_copy(i).wait()
        x = x_ref[...]
        h_ref[...] = _rms(x, g_ref[...]).astype(BF16)
        o_ref[...] = x

    @pl.when(jnp.logical_and(k == 1, i + 1 < pl.num_programs(0)))
    def _():
        x_copy(i + 1).start()

    h = h_ref[...]
    gate = _dot(h, wg_ref[...].astype(BF16))
    up = _dot(h, wu_ref[...].astype(BF16))
    act = (0.5 * gate / (1.0 + jnp.exp(-gate)) * up).astype(BF16)
    o_ref[...] += _dot(act, wd_ref[...].astype(BF16))
    for w_ref, wb_ref in zip(side_in, side_out):
        wb_ref[...] = w_ref[...].astype(BF16)


def _side_block(shape, gi, gk):
    k, n = shape
    if k % gi == 0 and n % gk == 0 and (k // gi) % BF16_SUBLANES == 0 and (n // gk) % LANES == 0:
        return (k // gi, n // gk), (lambda i, kk: (i, kk))
    assert k % gk == 0 and n % gi == 0 and (k // gk) % BF16_SUBLANES == 0 and (n // gi) % LANES == 0, shape
    return (k // gk, n // gi), (lambda i, kk: (kk, i))


def _col_blocked_spec(blk, order, width):
    per = width // blk[1]
    assert width % blk[1] == 0

    def index(i, kk):
        r, c = order(i, kk)
        return (c // per, r, c % per)

    return pl.BlockSpec((None, *blk), index)


FFN_BLOCK_BF16 = 512
FFN_BLOCK_F32 = 256


def _ffn(x, gain, wg, wu, wd, l, side=(), col_blocked=False, tm=1024, tf=FFN_BLOCK_BF16):
    t, d = x.shape
    f = wd.shape[1]
    gi, gk = t // tm, f // tf
    side_in_specs, side_out_specs, side_shapes = [], [], []
    for w, ls, width in side:
        kn = w.shape[1:]
        blk, order = _side_block(kn, gi, gk)
        side_in_specs.append(pl.BlockSpec((None, *blk), lambda i, k, ls=ls, order=order: (ls, *order(i, k))))
        if width is None:
            side_out_specs.append(pl.BlockSpec((None, *blk), lambda i, k, order=order: (0, *order(i, k))))
            side_shapes.append(jax.ShapeDtypeStruct((1, *kn), BF16))
        else:
            side_out_specs.append(_col_blocked_spec(blk, order, width))
            side_shapes.append(jax.ShapeDtypeStruct((kn[1] // width, kn[0], width), BF16))
    if col_blocked:
        up_spec = pl.BlockSpec((None, d, tf), lambda i, k: (k, 0, 0))
    else:
        up_spec = pl.BlockSpec((None, d, tf), lambda i, k: (l, 0, k))
    out, *cast = pl.pallas_call(
        functools.partial(_ffn_kernel, len(side)),
        out_shape=[jax.ShapeDtypeStruct((t, d), F32), *side_shapes],
        grid=(gi, gk),
        in_specs=[
            pl.BlockSpec(memory_space=pl.ANY),
            pl.BlockSpec((1, d), lambda i, k: (0, 0)),
            up_spec,
            up_spec,
            pl.BlockSpec((None, tf, d), lambda i, k: (l, k, 0)),
            *side_in_specs,
        ],
        out_specs=[pl.BlockSpec((tm, d), lambda i, k: (i, 0)), *side_out_specs],
        scratch_shapes=[pltpu.VMEM((tm, d), BF16), pltpu.VMEM((tm, d), F32), pltpu.SemaphoreType.DMA(())],
        compiler_params=_params(("arbitrary", "arbitrary")),
        name="ffn",
    )(x, gain.reshape(1, d), wg, wu, wd, *(w for w, _, _ in side))
    return out, cast


def _proj_kernel(x_ref, g_ref, w_ref, o_ref, h_ref):
    @pl.when(pl.program_id(1) == 0)
    def _():
        h_ref[...] = _rms(x_ref[...], g_ref[...]).astype(BF16)

    o_ref[...] = _dot(h_ref[...], w_ref[...]).astype(o_ref.dtype)


def _proj(x, gain, w, tm=2048):
    t, d = x.shape
    nj, _, tn = w.shape
    return pl.pallas_call(
        _proj_kernel,
        out_shape=jax.ShapeDtypeStruct((nj, t, tn), BF16),
        grid=(t // tm, nj),
        in_specs=[
            pl.BlockSpec((tm, d), lambda i, j: (i, 0)),
            pl.BlockSpec((1, d), lambda i, j: (0, 0)),
            pl.BlockSpec((None, d, tn), lambda i, j: (j, 0, 0)),
        ],
        out_specs=pl.BlockSpec((None, tm, tn), lambda i, j: (j, i, 0)),
        scratch_shapes=[pltpu.VMEM((tm, d), BF16)],
        compiler_params=_params(("parallel", "arbitrary")),
        name="proj",
    )(x, gain.reshape(1, d), w)


def _bias_kernel(tab_ref, o_ref):
    h = pl.program_id(0)
    far = tab_ref[h * N_BUCKETS + N_BUCKETS - 1]
    shape = (ATT_TILE, 2 * ATT_TILE)
    dist = ATT_TILE + lax.broadcasted_iota(jnp.int32, shape, 0) - lax.broadcasted_iota(jnp.int32, shape, 1)
    dist = jnp.maximum(dist, 0)
    max_exact = N_BUCKETS // 2
    log_ratio = jnp.log(jnp.maximum(dist, 1).astype(F32) / max_exact) / math.log(MAX_DISTANCE / max_exact)
    large = jnp.minimum(max_exact + (log_ratio * (N_BUCKETS - max_exact)).astype(jnp.int32), N_BUCKETS - 1)
    bucket = jnp.where(dist < max_exact, dist, large)
    out = jnp.zeros(shape, F32)
    for b in range(N_BUCKETS):
        out = jnp.where(bucket == b, (tab_ref[h * N_BUCKETS + b] - far) * LOG2E, out)
    o_ref[0] = out


def _bias_strips(tab_flat):
    return pl.pallas_call(
        _bias_kernel,
        out_shape=jax.ShapeDtypeStruct((N_BIAS_HEADS, ATT_TILE, 2 * ATT_TILE), F32),
        grid=(N_BIAS_HEADS,),
        in_specs=[pl.BlockSpec(memory_space=pltpu.SMEM)],
        out_specs=pl.BlockSpec((1, ATT_TILE, 2 * ATT_TILE), lambda h: (h, 0, 0)),
        compiler_params=_params(("arbitrary",)),
        name="bias_strips",
    )(tab_flat)


CONV_ROWS = 32
CONV_UNROLL = 4
SUBLANES = 8
assert CONV_PAD % SUBLANES == 0 and CONV_PAD >= SUBLANES * -(-CONV_WIDTH // SUBLANES)


def _conv_kernel(n_side, a_ref, g_ref, w_ref, cb_ref, lg_ref, lb_ref, *refs):
    side_in, o_ref, side_out, u_ref = refs[:n_side], refs[n_side], refs[n_side + 1:2 * n_side + 1], refs[-1]
    for wf_ref, wb_ref in zip(side_in, side_out):
        wb_ref[...] = wf_ref[...].astype(BF16)
    s = a_ref.shape[1]
    u_ref[0:CONV_PAD, :] = jnp.zeros((CONV_PAD, CONV_CH), F32)

    def glu(r, carry):
        rows = pl.ds(pl.multiple_of(r * CONV_ROWS, CONV_ROWS), CONV_ROWS)
        a = a_ref[0, rows, :].astype(F32)
        g = g_ref[0, rows, :].astype(F32)
        u_ref[pl.ds(pl.multiple_of(CONV_PAD + r * CONV_ROWS, CONV_ROWS), CONV_ROWS), :] = a / (1.0 + jnp.exp(-g))
        return carry

    lax.fori_loop(0, s // CONV_ROWS, glu, 0)

    win_rows = CONV_ROWS + CONV_PAD
    n_groups = CONV_CH // LANES

    def conv(r, carry):
        base = pl.multiple_of(r * CONV_ROWS, CONV_ROWS)
        accs = []
        for c in range(n_groups):
            lanes = slice(c * LANES, (c + 1) * LANES)
            win = u_ref[pl.ds(base, win_rows), lanes]
            acc = jnp.zeros((CONV_ROWS, LANES), F32) + cb_ref[:, lanes]
            for b in range(SUBLANES):
                shifted = pltpu.roll(win, b, 0) if b else win
                for a in range(-(-CONV_WIDTH // SUBLANES)):
                    d = SUBLANES * a + b
                    if d < CONV_WIDTH:
                        k0 = CONV_PAD - SUBLANES * a
                        tap = CONV_WIDTH - 1 - d
                        acc = acc + shifted[k0:k0 + CONV_ROWS, :] * w_ref[tap:tap + 1, lanes]
            accs.append(acc)
        mu = sum(jnp.sum(acc, axis=-1, keepdims=True) for acc in accs) / float(CONV_CH)
        cens = [acc - mu for acc in accs]
        var = sum(jnp.sum(cen * cen, axis=-1, keepdims=True) for cen in cens) / float(CONV_CH)
        inv = lax.rsqrt(var + 1e-5)
        for c, cen in enumerate(cens):
            lanes = slice(c * LANES, (c + 1) * LANES)
            y = cen * inv * lg_ref[:, lanes] + lb_ref[:, lanes]
            o_ref[0, pl.ds(base, CONV_ROWS), lanes] = (y / (1.0 + jnp.exp(-y))).astype(o_ref.dtype)
        return carry

    lax.fori_loop(0, s // CONV_ROWS, conv, 0, unroll=CONV_UNROLL)


def _proj_cols(col, width, s):
    per = PROJ_BLOCK // HEAD_DIM
    assert PROJ_BLOCK % width == 0 and width % HEAD_DIM == 0

    def index(bi, *rest):
        c = col(*rest) if callable(col) else col
        return (c // per, bi, 0, (c % per) * HEAD_DIM // width)

    return pl.BlockSpec((None, 1, s, width), index)


def _conv(proj, w, cb, lg, lb, side=()):
    _, b, s, _ = proj.shape
    vec = lambda v: v.reshape(1, CONV_CH)
    side_in_specs, side_out_specs, side_shapes = [], [], []
    for wf, ls in side:
        k, n = wf.shape[1:]
        assert k % b == 0 and (k // b) % BF16_SUBLANES == 0, wf.shape
        side_in_specs.append(pl.BlockSpec((None, k // b, n), lambda i, ls=ls: (ls, i, 0)))
        side_out_specs.append(pl.BlockSpec((None, k // b, n), lambda i: (0, i, 0)))
        side_shapes.append(jax.ShapeDtypeStruct((1, k, n), BF16))
    out, *cast = pl.pallas_call(
        functools.partial(_conv_kernel, len(side)),
        out_shape=[jax.ShapeDtypeStruct((b, s, CONV_CH), BF16), *side_shapes],
        grid=(b,),
        in_specs=[
            _proj_cols(COL_CONV_A, CONV_CH, s),
            _proj_cols(COL_CONV_G, CONV_CH, s),
            pl.BlockSpec((CONV_WIDTH, CONV_CH), lambda i: (0, 0)),
            pl.BlockSpec((1, CONV_CH), lambda i: (0, 0)),
            pl.BlockSpec((1, CONV_CH), lambda i: (0, 0)),
            pl.BlockSpec((1, CONV_CH), lambda i: (0, 0)),
            *side_in_specs,
        ],
        out_specs=[pl.BlockSpec((1, s, CONV_CH), lambda i: (i, 0, 0)), *side_out_specs],
        scratch_shapes=[pltpu.VMEM((s + CONV_PAD, CONV_CH), F32)],
        compiler_params=_params(("parallel",)),
        name="conv",
    )(proj, proj, w, vec(cb), vec(lg), vec(lb), *(wf for wf, _ in side))
    return out, cast


def _tile_rows(j):
    return slice(j * ATT_TILE, (j + 1) * ATT_TILE)


def _lane_fold(op, x):
    out = x[:, :LANES]
    for c in range(1, x.shape[1] // LANES):
        out = op(out, x[:, c * LANES:(c + 1) * LANES])
    return out


def _group_mean_matrix(width, group):
    r = lax.broadcasted_iota(jnp.int32, (width, width), 0) // group
    c = lax.broadcasted_iota(jnp.int32, (width, width), 1) // group
    return jnp.where(r == c, 1.0 / group, 0.0).astype(BF16)


def _group_rms(x, gain, mean_mat, eps=1e-6):
    ms = _dot((x * x).astype(BF16), mean_mat)
    return x * lax.rsqrt(ms + eps) * gain


def _ones_columns(v_ref, vo_ref):
    vo_ref[:, :HEAD_DIM] = v_ref[...]
    vo_ref[:, HEAD_DIM:] = jnp.ones((v_ref.shape[0], HEAD_DIM), BF16)


def _logits_pass(i, q, k_ref, strip_ref, sel_add, s_ref):
    maps = q.shape[0] // ATT_TILE
    shape = (q.shape[0], ATT_TILE)
    mask = lax.broadcasted_iota(jnp.int32, shape, 0) % ATT_TILE >= lax.broadcasted_iota(jnp.int32, shape, 1)
    rep = lambda x: x if maps == 1 else jnp.concatenate([x] * maps, axis=0)
    m_acc = None
    for j in range(i + 1):
        s = _dot_nt(q, k_ref[_tile_rows(j), :])
        if j == i:
            s = jnp.where(mask, s + rep(strip_ref[0, :, ATT_TILE:]), NEG_INF)
        elif j == i - 1:
            s = s + rep(strip_ref[0, :, :ATT_TILE])
        if sel_add is not None and j < i:
            s = s + sel_add[:, j:j + 1]
        s_ref[:, _tile_rows(j)] = s
        t = _lane_fold(jnp.maximum, s)
        m_acc = t if m_acc is None else jnp.maximum(m_acc, t)
    return jnp.max(m_acc, axis=-1, keepdims=True)


def _softmax_pass(i, m, vo_ref, s_ref, p_ref):
    for j in range(i + 1):
        p_ref[:, _tile_rows(j)] = jnp.exp2(s_ref[:, _tile_rows(j)] - m).astype(BF16)
    n = (i + 1) * ATT_TILE
    out = _dot(p_ref[:, :n], vo_ref[:n, :])
    return out[:, :HEAD_DIM], out[:, HEAD_DIM:]


def _moba_kernel(q_ref, k_ref, v_ref, qg_ref, kg_ref, strip_ref, o_ref, kn_ref, vo_ref, kmean_ref, s_ref, p_ref):
    nb = k_ref.shape[1] // MOBA_BLOCK
    mean_mat = _group_mean_matrix(HEAD_DIM, HEAD_DIM)
    _ones_columns(v_ref.at[0], vo_ref)
    for j in range(nb):
        rows = _tile_rows(j)
        kn = _group_rms(k_ref[0, rows, :].astype(F32), kg_ref[...], mean_mat)
        kn_ref[rows, :] = kn.astype(BF16)
        kmean_ref[j:j + 1, :] = jnp.sum(kn, axis=0, keepdims=True) / float(MOBA_BLOCK)

    eye = (lax.broadcasted_iota(jnp.int32, (MOBA_BLOCK, MOBA_BLOCK), 0)
           == lax.broadcasted_iota(jnp.int32, (MOBA_BLOCK, MOBA_BLOCK), 1)).astype(F32).astype(BF16)
    n_sel = min(MOBA_TOPK, nb)

    def logits(i):
        qn = _group_rms(q_ref[0, _tile_rows(i), :].astype(F32), qg_ref[...], mean_mat)
        sel_add = None
        if i > n_sel:
            gate = _dot_nt(kmean_ref[...], qn, precision=lax.Precision.HIGHEST)
            blk = lax.broadcasted_iota(jnp.int32, gate.shape, 0)
            gate = jnp.where(blk < i, gate, TAKEN)
            sel = jnp.zeros(gate.shape, F32)
            for _ in range(n_sel):
                top = jnp.max(gate, axis=0, keepdims=True)
                first = jnp.min(jnp.where(gate == top, blk, nb), axis=0, keepdims=True)
                pick = blk == first
                sel = jnp.where(pick, 1.0, sel)
                gate = jnp.where(pick, TAKEN, gate)
            sel_t = _dot_nt(eye, sel.astype(BF16))
            sel_add = (sel_t - 1.0) * -NEG_INF
        q = (qn * (HEAD_DIM ** -0.5 * LOG2E)).astype(BF16)
        return _logits_pass(i, q, kn_ref, strip_ref, sel_add, s_ref.at[i % 2])

    m_next = logits(0)
    for i in range(nb):
        m = m_next
        if i + 1 < nb:
            m_next = logits(i + 1)
        acc, l = _softmax_pass(i, m, vo_ref, s_ref.at[i % 2], p_ref.at[i % 2])
        o_ref[0, _tile_rows(i), :] = (acc / l).astype(o_ref.dtype)


def _moba(proj, strips, q_gain, k_gain):
    _, b, s, _ = proj.shape
    nb = s // MOBA_BLOCK
    return pl.pallas_call(
        _moba_kernel,
        out_shape=jax.ShapeDtypeStruct((b, s, MOBA_HEADS * HEAD_DIM), BF16),
        grid=(b, MOBA_HEADS),
        in_specs=[
            _proj_cols(lambda h: COL_MOBA_Q + h, HEAD_DIM, s),
            _proj_cols(lambda h: COL_MOBA_K + h, HEAD_DIM, s),
            _proj_cols(lambda h: COL_MOBA_V + h, HEAD_DIM, s),
            pl.BlockSpec((1, HEAD_DIM), lambda bi, h: (0, 0)),
            pl.BlockSpec((1, HEAD_DIM), lambda bi, h: (0, 0)),
            pl.BlockSpec((1, ATT_TILE, 2 * ATT_TILE), lambda bi, h: (h, 0, 0)),
        ],
        out_specs=pl.BlockSpec((1, s, HEAD_DIM), lambda bi, h: (bi, 0, h)),
        scratch_shapes=[
            pltpu.VMEM((s, HEAD_DIM), BF16),
            pltpu.VMEM((s, 2 * HEAD_DIM), BF16),
            pltpu.VMEM((nb, HEAD_DIM), F32),
            pltpu.VMEM((2, ATT_TILE, s), F32),
            pltpu.VMEM((2, ATT_TILE, s), BF16),
        ],
        compiler_params=_params(("parallel", "parallel")),
        name="moba",
    )(proj, proj, proj, q_gain.reshape(1, HEAD_DIM), k_gain.reshape(1, HEAD_DIM), strips)


DIFF_MAPS = 2


def _diff_kernel(lam_init, q_ref, k_ref, v_ref, qg_ref, kg_ref, lam_ref, sg_ref, strip_ref, o_ref,
                 kn_ref, vo_ref, s_ref, p_ref):
    nt = k_ref.shape[1] // ATT_TILE
    mean_mat = _group_mean_matrix(HEAD_DIM, DIFF_QK_DIM)
    _ones_columns(v_ref.at[0], vo_ref)
    for j in range(nt):
        rows = _tile_rows(j)
        kn_ref[rows, :] = _group_rms(k_ref[0, rows, :].astype(F32), kg_ref[...], mean_mat).astype(BF16)

    lp = lam_ref[...]
    lam = (jnp.exp(jnp.sum(lp[0:1] * lp[1:2], axis=-1, keepdims=True))
           - jnp.exp(jnp.sum(lp[2:3] * lp[3:4], axis=-1, keepdims=True)) + lam_init)
    scale = DIFF_QK_DIM ** -0.5 * LOG2E
    lane_map = lax.broadcasted_iota(jnp.int32, (ATT_TILE, HEAD_DIM), 1) // DIFF_QK_DIM

    def logits(i):
        qn = _group_rms(q_ref[0, _tile_rows(i), :].astype(F32), qg_ref[...], mean_mat) * scale
        q = jnp.concatenate([jnp.where(lane_map == c, qn, 0.0) for c in range(DIFF_MAPS)], axis=0).astype(BF16)
        return _logits_pass(i, q, kn_ref, strip_ref, None, s_ref.at[i % 2])

    m_next = logits(0)
    for i in range(nt):
        m = m_next
        if i + 1 < nt:
            m_next = logits(i + 1)
        acc, l = _softmax_pass(i, m, vo_ref, s_ref.at[i % 2], p_ref.at[i % 2])
        a = acc / l
        o = a[:ATT_TILE] - lam * a[ATT_TILE:]
        o_ref[0, _tile_rows(i), :] = (_rms(o, sg_ref[...]) * (1.0 - lam_init)).astype(o_ref.dtype)


def _diff(proj, strips, q_gain, k_gain, lam_params, subln, lam_init):
    _, b, s, _ = proj.shape
    both = lambda g: jnp.tile(g, DIFF_MAPS).reshape(1, HEAD_DIM)
    return pl.pallas_call(
        functools.partial(_diff_kernel, lam_init),
        out_shape=jax.ShapeDtypeStruct((b, s, DIFF_HEADS * HEAD_DIM), BF16),
        grid=(b, DIFF_HEADS),
        in_specs=[
            _proj_cols(lambda h: COL_DIFF_Q + h, HEAD_DIM, s),
            _proj_cols(lambda h: COL_DIFF_K + h, HEAD_DIM, s),
            _proj_cols(lambda h: COL_DIFF_V + h, HEAD_DIM, s),
            pl.BlockSpec((1, HEAD_DIM), lambda bi, h: (0, 0)),
            pl.BlockSpec((1, HEAD_DIM), lambda bi, h: (0, 0)),
            pl.BlockSpec((4, DIFF_QK_DIM), lambda bi, h: (0, 0)),
            pl.BlockSpec((1, HEAD_DIM), lambda bi, h: (0, 0)),
            pl.BlockSpec((1, ATT_TILE, 2 * ATT_TILE), lambda bi, h: (MOBA_HEADS + h, 0, 0)),
        ],
        out_specs=pl.BlockSpec((1, s, HEAD_DIM), lambda bi, h: (bi, 0, h)),
        scratch_shapes=[
            pltpu.VMEM((s, HEAD_DIM), BF16),
            pltpu.VMEM((s, 2 * HEAD_DIM), BF16),
            pltpu.VMEM((2, DIFF_MAPS * ATT_TILE, s), F32),
            pltpu.VMEM((2, DIFF_MAPS * ATT_TILE, s), BF16),
        ],
        compiler_params=_params(("parallel", "parallel")),
        name="diff",
    )(proj, proj, proj, both(q_gain), both(k_gain), lam_params, subln.reshape(1, HEAD_DIM), strips)


def _memkv_kernel(m_ref, g_ref, w_ref, kg_ref, k_ref, v_ref):
    hm = _rms(m_ref[0], g_ref[...]).astype(BF16)
    kv = _dot(hm, w_ref[...])
    width = MEM_HEADS * HEAD_DIM
    for h in range(MEM_HEADS):
        cols = slice(h * HEAD_DIM, (h + 1) * HEAD_DIM)
        k_ref[0, :, cols] = _rms(kv[:, cols], kg_ref[...]).astype(BF16)
    v_ref[0] = kv[:, width:].astype(BF16)


def _memkv(mem, gain, wkv, k_gain, l):
    b, m, d = mem.shape
    width = MEM_HEADS * HEAD_DIM
    out = jax.ShapeDtypeStruct((b, m, width), BF16)
    return pl.pallas_call(
        _memkv_kernel,
        out_shape=(out, out),
        grid=(b,),
        in_specs=[
            pl.BlockSpec((1, m, d), lambda i: (i, 0, 0)),
            pl.BlockSpec((1, d), lambda i: (0, 0)),
            pl.BlockSpec((None, d, 2 * width), lambda i: (l, 0, 0)),
            pl.BlockSpec((1, HEAD_DIM), lambda i: (0, 0)),
        ],
        out_specs=(pl.BlockSpec((1, m, width), lambda i: (i, 0, 0)), pl.BlockSpec((1, m, width), lambda i: (i, 0, 0))),
        compiler_params=_params(("parallel",)),
        name="memkv",
    )(mem, gain.reshape(1, d), wkv, k_gain.reshape(1, HEAD_DIM))


def _mixout_kernel(x_ref, yc_ref, ym_ref, yd_ref, w_ref, g_ref, wq_ref, qg_ref, k_ref, v_ref, wo_ref, o_ref, ob_ref):
    c0 = CONV_CH
    c1 = c0 + MOBA_HEADS * HEAD_DIM
    x1 = (x_ref[0] + _dot(yc_ref[0], w_ref[0:c0, :]) + _dot(ym_ref[0], w_ref[c0:c1, :])
          + _dot(yd_ref[0], w_ref[c1:, :]))
    q = _dot(_rms(x1, g_ref[...]).astype(BF16), wq_ref[...])
    mean_mat = _group_mean_matrix(HEAD_DIM, HEAD_DIM)
    for h in range(MEM_HEADS):
        cols = slice(h * HEAD_DIM, (h + 1) * HEAD_DIM)
        qh = (_group_rms(q[:, cols], qg_ref[...], mean_mat) * (HEAD_DIM ** -0.5 * LOG2E)).astype(BF16)
        s = _dot_nt(qh, k_ref[0, :, cols])
        p = jnp.exp2(s - jnp.max(s, axis=-1, keepdims=True))
        oh = _dot(p.astype(BF16), v_ref[0, :, cols]) / jnp.sum(p, axis=-1, keepdims=True)
        ob_ref[:, cols] = oh.astype(BF16)
    o_ref[0] = x1 + _dot(ob_ref[...], wo_ref[...])


def _mixout(x, yc, ym, yd, w, gain, wq, q_gain, kn, v, wo, l, tm=512):
    b, s, d = x.shape
    m = kn.shape[1]
    width = MEM_HEADS * HEAD_DIM
    once = pl.Buffered(1)
    rows = lambda a: pl.BlockSpec((1, tm, a.shape[2]), lambda bi, i: (bi, i, 0))
    return pl.pallas_call(
        _mixout_kernel,
        out_shape=jax.ShapeDtypeStruct((b, s, d), F32),
        grid=(b, s // tm),
        in_specs=[
            rows(x), rows(yc), rows(ym), rows(yd),
            pl.BlockSpec((None, w.shape[1], d), lambda bi, i: (l, 0, 0), pipeline_mode=once),
            pl.BlockSpec((1, d), lambda bi, i: (0, 0)),
            pl.BlockSpec((None, d, width), lambda bi, i: (l, 0, 0), pipeline_mode=once),
            pl.BlockSpec((1, HEAD_DIM), lambda bi, i: (0, 0)),
            pl.BlockSpec((1, m, width), lambda bi, i: (bi, 0, 0)),
            pl.BlockSpec((1, m, width), lambda bi, i: (bi, 0, 0)),
            pl.BlockSpec((None, width, d), lambda bi, i: (l, 0, 0), pipeline_mode=once),
        ],
        out_specs=rows(x),
        scratch_shapes=[pltpu.VMEM((tm, width), BF16)],
        compiler_params=_params(("parallel", "parallel")),
        name="mixout",
    )(x, yc, ym, yd, w, gain.reshape(1, d), wq, q_gain.reshape(1, HEAD_DIM), kn, v, wo)


def kernel(x, mem, rel_bias, ffn1_norm, ffn1_w_gate, ffn1_w_up, ffn1_w_down, mix_norm, w_in, conv_w, conv_b, conv_ln_g, conv_ln_b, moba_q_norm, moba_k_norm, diff_q_norm, diff_k_norm, diff_lambda, diff_subln, w_out, mem_norm_x, mem_norm_m, mem_wq, mem_wkv, mem_q_norm, mem_k_norm, mem_wo, ffn2_norm, ffn2_w_gate, ffn2_w_up, ffn2_w_down):
    b, s, d = x.shape
    depth = w_in.shape[0]
    t = b * s
    tab_flat = rel_bias.T.reshape(-1)
    strips = _bias_strips(tab_flat)
    ffn1_f32 = (ffn1_w_gate, ffn1_w_up, ffn1_w_down)
    ffn2_f32 = (ffn2_w_gate, ffn2_w_up, ffn2_w_down)
    ffn_w, ffn_tf = ffn1_f32, FFN_BLOCK_F32
    xt = x.reshape(t, d)
    for l in range(depth):
        blocked = FFN_BLOCK_BF16, FFN_BLOCK_BF16, None, PROJ_BLOCK
        xt, (*ffn_w, w_in_l) = _ffn(xt, ffn1_norm[l], *ffn_w, 0, col_blocked=l > 0, tf=ffn_tf,
                                    side=[(w, l, c) for w, c in zip((*ffn2_f32, w_in), blocked)])
        ffn_tf = FFN_BLOCK_BF16
        proj = _proj(xt, mix_norm[l], w_in_l).reshape(-1, b, s, PROJ_BLOCK)
        y_conv, (w_out_l, wq_l, wkv_l, wo_l) = _conv(proj, conv_w[l], conv_b[l], conv_ln_g[l], conv_ln_b[l],
                                                    side=[(w, l) for w in (w_out, mem_wq, mem_wkv, mem_wo)])
        y_moba = _moba(proj, strips, moba_q_norm[l], moba_k_norm[l])
        lam_init = 0.8 - 0.6 * math.exp(-0.3 * l)
        y_diff = _diff(proj, strips, diff_q_norm[l], diff_k_norm[l], diff_lambda[l], diff_subln[l], lam_init)
        kn, v = _memkv(mem, mem_norm_m[l], wkv_l, mem_k_norm[l], 0)
        xt = _mixout(xt.reshape(b, s, d), y_conv, y_moba, y_diff, w_out_l, mem_norm_x[l], wq_l, mem_q_norm[l], kn, v,
                     wo_l, 0).reshape(t, d)
        nxt = [(w, l + 1, c) for w, c in zip(ffn1_f32, blocked)] if l + 1 < depth else []
        xt, ffn_w = _ffn(xt, ffn2_norm[l], *ffn_w, 0, side=nxt, col_blocked=True, tf=ffn_tf)
    return xt.reshape(b, s, d)
```

```python
import functools
import math

import jax
import jax.numpy as jnp
from jax import lax
from jax.experimental import pallas as pl
from jax.experimental.pallas import tpu as pltpu

F32 = jnp.float32
BF16 = jnp.bfloat16

D_MODEL = 2048
D_FF = 5632
CONV_CH = 512
CONV_WIDTH = 31
CONV_PAD = 32
HEAD_DIM = 128
LANES = 128
BF16_SUBLANES = 16
MOBA_HEADS = 6
MOBA_BLOCK = 256
MOBA_TOPK = 3
DIFF_HEADS = 6
DIFF_QK_DIM = 64
MEM_HEADS = 4
N_BUCKETS = 32
MAX_DISTANCE = 128
N_BIAS_HEADS = MOBA_HEADS + DIFF_HEADS
NEG_INF = -1e30
TAKEN = -3e38
PROJ_WIDTH = 5632
COL_CONV_A = 0
COL_CONV_G = CONV_CH // HEAD_DIM
COL_MOBA_Q = 2 * CONV_CH // HEAD_DIM
COL_MOBA_K = COL_MOBA_Q + MOBA_HEADS
COL_MOBA_V = COL_MOBA_K + MOBA_HEADS
COL_DIFF_Q = COL_MOBA_V + MOBA_HEADS
COL_DIFF_K = COL_DIFF_Q + DIFF_HEADS
COL_DIFF_V = COL_DIFF_K + DIFF_HEADS

ATT_TILE = 256
assert ATT_TILE == MOBA_BLOCK and ATT_TILE >= MAX_DISTANCE
LOG2E = 1.4426950408889634
VMEM_LIMIT = 56 * 1024 * 1024


def _params(semantics, vmem=VMEM_LIMIT):
    return pltpu.CompilerParams(dimension_semantics=("arbitrary",) * len(semantics), vmem_limit_bytes=vmem)


def _rms(x, gain, eps=1e-6):
    return x * lax.rsqrt(jnp.mean(x * x, axis=-1, keepdims=True) + eps) * gain


def _dot(a, b):
    return jnp.dot(a, b, preferred_element_type=F32)


def _dot_nt(a, b, precision=None):
    return lax.dot_general(a, b, (((1,), (1,)), ((), ())), precision=precision, preferred_element_type=F32)


def _ffn_kernel(n_side, x_hbm, g_ref, wg_ref, wu_ref, wd_ref, *refs):
    side_in, o_ref, side_out = refs[:n_side], refs[n_side], refs[n_side + 1:2 * n_side + 1]
    h_ref, x_ref, x_sem = refs[2 * n_side + 1:]
    i, k = pl.program_id(0), pl.program_id(1)
    tm = x_ref.shape[0]

    def x_copy(block):
        return pltpu.make_async_copy(x_hbm.at[pl.ds(pl.multiple_of(block * tm, tm), tm), :], x_ref, x_sem)

    @pl.when(jnp.logical_and(i == 0, k == 0))
    def _():
        x_copy(0).start()

    @pl.when(k == 0)
    def _():
        x_copy(i).wait()
        x = x_ref[...]
        h_ref[...] = _rms(x, g_ref[...]).astype(BF16)
        o_ref[...] = x

    @pl.when(jnp.logical_and(k == 1, i + 1 < pl.num_programs(0)))
    def _():
        x_copy(i + 1).start()

    h = h_ref[...]
    gate = _dot(h, wg_ref[...].astype(BF16))
    up = _dot(h, wu_ref[...].astype(BF16))
    act = (0.5 * gate / (1.0 + jnp.exp(-gate)) * up).astype(BF16)
    o_ref[...] += _dot(act, wd_ref[...].astype(BF16))
    for w_ref, wb_ref in zip(side_in, side_out):
        wb_ref[...] = w_ref[...].astype(BF16)


def _side_block(shape, gi, gk):
    k, n = shape
    if k % gi == 0 and n % gk == 0 and (k // gi) % BF16_SUBLANES == 0 and (n // gk) % LANES == 0:
        return (k // gi, n // gk), (lambda i, kk: (i, kk))
    assert k % gk == 0 and n % gi == 0 and (k // gk) % BF16_SUBLANES == 0 and (n // gi) % LANES == 0, shape
    return (k // gk, n // gi), (lambda i, kk: (kk, i))


FFN_BLOCK_BF16 = 512
FFN_BLOCK_F32 = 256


def _ffn(x, gain, wg, wu, wd, l, side=(), tm=1024, tf=FFN_BLOCK_BF16):
    t, d = x.shape
    f = wg.shape[2]
    gi, gk = t // tm, f // tf
    side_in_specs, side_out_specs, side_shapes = [], [], []
    for w, ls in side:
        blk, order = _side_block(w.shape[1:], gi, gk)
        side_in_specs.append(pl.BlockSpec((None, *blk), lambda i, k, ls=ls, order=order: (ls, *order(i, k))))
        side_out_specs.append(pl.BlockSpec((None, *blk), lambda i, k, order=order: (0, *order(i, k))))
        side_shapes.append(jax.ShapeDtypeStruct((1, *w.shape[1:]), BF16))
    out, *cast = pl.pallas_call(
        functools.partial(_ffn_kernel, len(side)),
        out_shape=[jax.ShapeDtypeStruct((t, d), F32), *side_shapes],
        grid=(gi, gk),
        in_specs=[
            pl.BlockSpec(memory_space=pl.ANY),
            pl.BlockSpec((1, d), lambda i, k: (0, 0)),
            pl.BlockSpec((None, d, tf), lambda i, k: (l, 0, k)),
            pl.BlockSpec((None, d, tf), lambda i, k: (l, 0, k)),
            pl.BlockSpec((None, tf, d), lambda i, k: (l, k, 0)),
            *side_in_specs,
        ],
        out_specs=[pl.BlockSpec((tm, d), lambda i, k: (i, 0)), *side_out_specs],
        scratch_shapes=[pltpu.VMEM((tm, d), BF16), pltpu.VMEM((tm, d), F32), pltpu.SemaphoreType.DMA(())],
        compiler_params=_params(("arbitrary", "arbitrary")),
        name="ffn",
    )(x, gain.reshape(1, d), wg, wu, wd, *(w for w, _ in side))
    return out, cast


def _proj_kernel(x_ref, g_ref, w_ref, o_ref, h_ref):
    @pl.when(pl.program_id(1) == 0)
    def _():
        h_ref[...] = _rms(x_ref[...], g_ref[...]).astype(BF16)

    o_ref[...] = _dot(h_ref[...], w_ref[...]).astype(o_ref.dtype)


def _proj(x, gain, w, l, tm=2048, tn=512):
    t, d = x.shape
    n = w.shape[2]
    return pl.pallas_call(
        _proj_kernel,
        out_shape=jax.ShapeDtypeStruct((t, n), BF16),
        grid=(t // tm, n // tn),
        in_specs=[
            pl.BlockSpec((tm, d), lambda i, j: (i, 0)),
            pl.BlockSpec((1, d), lambda i, j: (0, 0)),
            pl.BlockSpec((None, d, tn), lambda i, j: (l, 0, j)),
        ],
        out_specs=pl.BlockSpec((tm, tn), lambda i, j: (i, j)),
        scratch_shapes=[pltpu.VMEM((tm, d), BF16)],
        compiler_params=_params(("parallel", "arbitrary")),
        name="proj",
    )(x, gain.reshape(1, d), w)


def _bias_kernel(tab_ref, o_ref):
    h = pl.program_id(0)
    far = tab_ref[h * N_BUCKETS + N_BUCKETS - 1]
    shape = (ATT_TILE, 2 * ATT_TILE)
    dist = ATT_TILE + lax.broadcasted_iota(jnp.int32, shape, 0) - lax.broadcasted_iota(jnp.int32, shape, 1)
    dist = jnp.maximum(dist, 0)
    max_exact = N_BUCKETS // 2
    log_ratio = jnp.log(jnp.maximum(dist, 1).astype(F32) / max_exact) / math.log(MAX_DISTANCE / max_exact)
    large = jnp.minimum(max_exact + (log_ratio * (N_BUCKETS - max_exact)).astype(jnp.int32), N_BUCKETS - 1)
    bucket = jnp.where(dist < max_exact, dist, large)
    out = jnp.zeros(shape, F32)
    for b in range(N_BUCKETS):
        out = jnp.where(bucket == b, (tab_ref[h * N_BUCKETS + b] - far) * LOG2E, out)
    o_ref[0] = out


def _bias_strips(tab_flat):
    return pl.pallas_call(
        _bias_kernel,
        out_shape=jax.ShapeDtypeStruct((N_BIAS_HEADS, ATT_TILE, 2 * ATT_TILE), F32),
        grid=(N_BIAS_HEADS,),
        in_specs=[pl.BlockSpec(memory_space=pltpu.SMEM)],
        out_specs=pl.BlockSpec((1, ATT_TILE, 2 * ATT_TILE), lambda h: (h, 0, 0)),
        compiler_params=_params(("arbitrary",)),
        name="bias_strips",
    )(tab_flat)


CONV_ROWS = 32
CONV_UNROLL = 4
SUBLANES = 8
assert CONV_PAD % SUBLANES == 0 and CONV_PAD >= SUBLANES * -(-CONV_WIDTH // SUBLANES)


def _conv_kernel(n_side, a_ref, g_ref, w_ref, cb_ref, lg_ref, lb_ref, *refs):
    side_in, o_ref, side_out, u_ref = refs[:n_side], refs[n_side], refs[n_side + 1:2 * n_side + 1], refs[-1]
    for wf_ref, wb_ref in zip(side_in, side_out):
        wb_ref[...] = wf_ref[...].astype(BF16)
    s = a_ref.shape[1]
    u_ref[0:CONV_PAD, :] = jnp.zeros((CONV_PAD, CONV_CH), F32)

    def glu(r, carry):
        rows = pl.ds(pl.multiple_of(r * CONV_ROWS, CONV_ROWS), CONV_ROWS)
        a = a_ref[0, rows, :].astype(F32)
        g = g_ref[0, rows, :].astype(F32)
        u_ref[pl.ds(pl.multiple_of(CONV_PAD + r * CONV_ROWS, CONV_ROWS), CONV_ROWS), :] = a / (1.0 + jnp.exp(-g))
        return carry

    lax.fori_loop(0, s // CONV_ROWS, glu, 0)

    win_rows = CONV_ROWS + CONV_PAD
    n_groups = CONV_CH // LANES

    def conv(r, carry):
        base = pl.multiple_of(r * CONV_ROWS, CONV_ROWS)
        accs = []
        for c in range(n_groups):
            lanes = slice(c * LANES, (c + 1) * LANES)
            win = u_ref[pl.ds(base, win_rows), lanes]
            acc = jnp.zeros((CONV_ROWS, LANES), F32) + cb_ref[:, lanes]
            for b in range(SUBLANES):
                shifted = pltpu.roll(win, b, 0) if b else win
                for a in range(-(-CONV_WIDTH // SUBLANES)):
                    d = SUBLANES * a + b
                    if d < CONV_WIDTH:
                        k0 = CONV_PAD - SUBLANES * a
                        tap = CONV_WIDTH - 1 - d
                        acc = acc + shifted[k0:k0 + CONV_ROWS, :] * w_ref[tap:tap + 1, lanes]
            accs.append(acc)
        mu = sum(jnp.sum(acc, axis=-1, keepdims=True) for acc in accs) / float(CONV_CH)
        cens = [acc - mu for acc in accs]
        var = sum(jnp.sum(cen * cen, axis=-1, keepdims=True) for cen in cens) / float(CONV_CH)
        inv = lax.rsqrt(var + 1e-5)
        for c, cen in enumerate(cens):
            lanes = slice(c * LANES, (c + 1) * LANES)
            y = cen * inv * lg_ref[:, lanes] + lb_ref[:, lanes]
            o_ref[0, pl.ds(base, CONV_ROWS), lanes] = (y / (1.0 + jnp.exp(-y))).astype(o_ref.dtype)
        return carry

    lax.fori_loop(0, s // CONV_ROWS, conv, 0, unroll=CONV_UNROLL)


def _conv(proj, w, cb, lg, lb, side=()):
    b, s, _ = proj.shape
    vec = lambda v: v.reshape(1, CONV_CH)
    side_in_specs, side_out_specs, side_shapes = [], [], []
    for wf, ls in side:
        k, n = wf.shape[1:]
        assert k % b == 0 and (k // b) % BF16_SUBLANES == 0, wf.shape
        side_in_specs.append(pl.BlockSpec((None, k // b, n), lambda i, ls=ls: (ls, i, 0)))
        side_out_specs.append(pl.BlockSpec((None, k // b, n), lambda i: (0, i, 0)))
        side_shapes.append(jax.ShapeDtypeStruct((1, k, n), BF16))
    out, *cast = pl.pallas_call(
        functools.partial(_conv_kernel, len(side)),
        out_shape=[jax.ShapeDtypeStruct((b, s, CONV_CH), BF16), *side_shapes],
        grid=(b,),
        in_specs=[
            pl.BlockSpec((1, s, CONV_CH), lambda i: (i, 0, 0)),
            pl.BlockSpec((1, s, CONV_CH), lambda i: (i, 0, 1)),
            pl.BlockSpec((CONV_WIDTH, CONV_CH), lambda i: (0, 0)),
            pl.BlockSpec((1, CONV_CH), lambda i: (0, 0)),
            pl.BlockSpec((1, CONV_CH), lambda i: (0, 0)),
            pl.BlockSpec((1, CONV_CH), lambda i: (0, 0)),
            *side_in_specs,
        ],
        out_specs=[pl.BlockSpec((1, s, CONV_CH), lambda i: (i, 0, 0)), *side_out_specs],
        scratch_shapes=[pltpu.VMEM((s + CONV_PAD, CONV_CH), F32)],
        compiler_params=_params(("parallel",)),
        name="conv",
    )(proj, proj, w, vec(cb), vec(lg), vec(lb), *(wf for wf, _ in side))
    return out, cast


def _tile_rows(j):
    return slice(j * ATT_TILE, (j + 1) * ATT_TILE)


def _lane_fold(op, x):
    out = x[:, :LANES]
    for c in range(1, x.shape[1] // LANES):
        out = op(out, x[:, c * LANES:(c + 1) * LANES])
    return out


def _group_mean_matrix(width, group):
    r = lax.broadcasted_iota(jnp.int32, (width, width), 0) // group
    c = lax.broadcasted_iota(jnp.int32, (width, width), 1) // group
    return jnp.where(r == c, 1.0 / group, 0.0).astype(BF16)


def _group_rms(x, gain, mean_mat, eps=1e-6):
    ms = _dot((x * x).astype(BF16), mean_mat)
    return x * lax.rsqrt(ms + eps) * gain


def _ones_columns(v_ref, vo_ref):
    vo_ref[:, :HEAD_DIM] = v_ref[...]
    vo_ref[:, HEAD_DIM:] = jnp.ones((v_ref.shape[0], HEAD_DIM), BF16)


def _logits_pass(i, q, k_ref, strip_ref, sel_add, s_ref):
    maps = q.shape[0] // ATT_TILE
    shape = (q.shape[0], ATT_TILE)
    mask = lax.broadcasted_iota(jnp.int32, shape, 0) % ATT_TILE >= lax.broadcasted_iota(jnp.int32, shape, 1)
    rep = lambda x: x if maps == 1 else jnp.concatenate([x] * maps, axis=0)
    m_acc = None
    for j in range(i + 1):
        s = _dot_nt(q, k_ref[_tile_rows(j), :])
        if j == i:
            s = jnp.where(mask, s + rep(strip_ref[0, :, ATT_TILE:]), NEG_INF)
        elif j == i - 1:
            s = s + rep(strip_ref[0, :, :ATT_TILE])
        if sel_add is not None and j < i:
            s = s + sel_add[:, j:j + 1]
        s_ref[:, _tile_rows(j)] = s
        t = _lane_fold(jnp.maximum, s)
        m_acc = t if m_acc is None else jnp.maximum(m_acc, t)
    return jnp.max(m_acc, axis=-1, keepdims=True)


def _softmax_pass(i, m, vo_ref, s_ref, p_ref):
    for j in range(i + 1):
        p_ref[:, _tile_rows(j)] = jnp.exp2(s_ref[:, _tile_rows(j)] - m).astype(BF16)
    n = (i + 1) * ATT_TILE
    out = _dot(p_ref[:, :n], vo_ref[:n, :])
    return out[:, :HEAD_DIM], out[:, HEAD_DIM:]


def _moba_kernel(q_ref, k_ref, v_ref, qg_ref, kg_ref, strip_ref, o_ref, kn_ref, vo_ref, kmean_ref, s_ref, p_ref):
    nb = k_ref.shape[1] // MOBA_BLOCK
    mean_mat = _group_mean_matrix(HEAD_DIM, HEAD_DIM)
    _ones_columns(v_ref.at[0], vo_ref)
    for j in range(nb):
        rows = _tile_rows(j)
        kn = _group_rms(k_ref[0, rows, :].astype(F32), kg_ref[...], mean_mat)
        kn_ref[rows, :] = kn.astype(BF16)
        kmean_ref[j:j + 1, :] = jnp.sum(kn, axis=0, keepdims=True) / float(MOBA_BLOCK)

    eye = (lax.broadcasted_iota(jnp.int32, (MOBA_BLOCK, MOBA_BLOCK), 0)
           == lax.broadcasted_iota(jnp.int32, (MOBA_BLOCK, MOBA_BLOCK), 1)).astype(F32).astype(BF16)
    n_sel = min(MOBA_TOPK, nb)

    def logits(i):
        qn = _group_rms(q_ref[0, _tile_rows(i), :].astype(F32), qg_ref[...], mean_mat)
        sel_add = None
        if i > n_sel:
            gate = _dot_nt(kmean_ref[...], qn, precision=lax.Precision.HIGHEST)
            blk = lax.broadcasted_iota(jnp.int32, gate.shape, 0)
            gate = jnp.where(blk < i, gate, TAKEN)
            sel = jnp.zeros(gate.shape, F32)
            for _ in range(n_sel):
                top = jnp.max(gate, axis=0, keepdims=True)
                first = jnp.min(jnp.where(gate == top, blk, nb), axis=0, keepdims=True)
                pick = blk == first
                sel = jnp.where(pick, 1.0, sel)
                gate = jnp.where(pick, TAKEN, gate)
            sel_t = _dot_nt(eye, sel.astype(BF16))
            sel_add = (sel_t - 1.0) * -NEG_INF
        q = (qn * (HEAD_DIM ** -0.5 * LOG2E)).astype(BF16)
        return _logits_pass(i, q, kn_ref, strip_ref, sel_add, s_ref.at[i % 2])

    m_next = logits(0)
    for i in range(nb):
        m = m_next
        if i + 1 < nb:
            m_next = logits(i + 1)
        acc, l = _softmax_pass(i, m, vo_ref, s_ref.at[i % 2], p_ref.at[i % 2])
        o_ref[0, _tile_rows(i), :] = (acc / l).astype(o_ref.dtype)


def _moba(proj, strips, q_gain, k_gain):
    b, s, _ = proj.shape
    nb = s // MOBA_BLOCK
    return pl.pallas_call(
        _moba_kernel,
        out_shape=jax.ShapeDtypeStruct((b, s, MOBA_HEADS * HEAD_DIM), BF16),
        grid=(b, MOBA_HEADS),
        in_specs=[
            pl.BlockSpec((1, s, HEAD_DIM), lambda bi, h: (bi, 0, COL_MOBA_Q + h)),
            pl.BlockSpec((1, s, HEAD_DIM), lambda bi, h: (bi, 0, COL_MOBA_K + h)),
            pl.BlockSpec((1, s, HEAD_DIM), lambda bi, h: (bi, 0, COL_MOBA_V + h)),
            pl.BlockSpec((1, HEAD_DIM), lambda bi, h: (0, 0)),
            pl.BlockSpec((1, HEAD_DIM), lambda bi, h: (0, 0)),
            pl.BlockSpec((1, ATT_TILE, 2 * ATT_TILE), lambda bi, h: (h, 0, 0)),
        ],
        out_specs=pl.BlockSpec((1, s, HEAD_DIM), lambda bi, h: (bi, 0, h)),
        scratch_shapes=[
            pltpu.VMEM((s, HEAD_DIM), BF16),
            pltpu.VMEM((s, 2 * HEAD_DIM), BF16),
            pltpu.VMEM((nb, HEAD_DIM), F32),
            pltpu.VMEM((2, ATT_TILE, s), F32),
            pltpu.VMEM((2, ATT_TILE, s), BF16),
        ],
        compiler_params=_params(("parallel", "parallel")),
        name="moba",
    )(proj, proj, proj, q_gain.reshape(1, HEAD_DIM), k_gain.reshape(1, HEAD_DIM), strips)


DIFF_MAPS = 2


def _diff_kernel(lam_init, q_ref, k_ref, v_ref, qg_ref, kg_ref, lam_ref, sg_ref, strip_ref, o_ref,
                 kn_ref, vo_ref, s_ref, p_ref):
    nt = k_ref.shape[1] // ATT_TILE
    mean_mat = _group_mean_matrix(HEAD_DIM, DIFF_QK_DIM)
    _ones_columns(v_ref.at[0], vo_ref)
    for j in range(nt):
        rows = _tile_rows(j)
        kn_ref[rows, :] = _group_rms(k_ref[0, rows, :].astype(F32), kg_ref[...], mean_mat).astype(BF16)

    lp = lam_ref[...]
    lam = (jnp.exp(jnp.sum(lp[0:1] * lp[1:2], axis=-1, keepdims=True))
           - jnp.exp(jnp.sum(lp[2:3] * lp[3:4], axis=-1, keepdims=True)) + lam_init)
    scale = DIFF_QK_DIM ** -0.5 * LOG2E
    lane_map = lax.broadcasted_iota(jnp.int32, (ATT_TILE, HEAD_DIM), 1) // DIFF_QK_DIM

    def logits(i):
        qn = _group_rms(q_ref[0, _tile_rows(i), :].astype(F32), qg_ref[...], mean_mat) * scale
        q = jnp.concatenate([jnp.where(lane_map == c, qn, 0.0) for c in range(DIFF_MAPS)], axis=0).astype(BF16)
        return _logits_pass(i, q, kn_ref, strip_ref, None, s_ref.at[i % 2])

    m_next = logits(0)
    for i in range(nt):
        m = m_next
        if i + 1 < nt:
            m_next = logits(i + 1)
        acc, l = _softmax_pass(i, m, vo_ref, s_ref.at[i % 2], p_ref.at[i % 2])
        a = acc / l
        o = a[:ATT_TILE] - lam * a[ATT_TILE:]
        o_ref[0, _tile_rows(i), :] = (_rms(o, sg_ref[...]) * (1.0 - lam_init)).astype(o_ref.dtype)


def _diff(proj, strips, q_gain, k_gain, lam_params, subln, lam_init):
    b, s, _ = proj.shape
    both = lambda g: jnp.tile(g, DIFF_MAPS).reshape(1, HEAD_DIM)
    return pl.pallas_call(
        functools.partial(_diff_kernel, lam_init),
        out_shape=jax.ShapeDtypeStruct((b, s, DIFF_HEADS * HEAD_DIM), BF16),
        grid=(b, DIFF_HEADS),
        in_specs=[
            pl.BlockSpec((1, s, HEAD_DIM), lambda bi, h: (bi, 0, COL_DIFF_Q + h)),
            pl.BlockSpec((1, s, HEAD_DIM), lambda bi, h: (bi, 0, COL_DIFF_K + h)),
            pl.BlockSpec((1, s, HEAD_DIM), lambda bi, h: (bi, 0, COL_DIFF_V + h)),
            pl.BlockSpec((1, HEAD_DIM), lambda bi, h: (0, 0)),
            pl.BlockSpec((1, HEAD_DIM), lambda bi, h: (0, 0)),
            pl.BlockSpec((4, DIFF_QK_DIM), lambda bi, h: (0, 0)),
            pl.BlockSpec((1, HEAD_DIM), lambda bi, h: (0, 0)),
            pl.BlockSpec((1, ATT_TILE, 2 * ATT_TILE), lambda bi, h: (MOBA_HEADS + h, 0, 0)),
        ],
        out_specs=pl.BlockSpec((1, s, HEAD_DIM), lambda bi, h: (bi, 0, h)),
        scratch_shapes=[
            pltpu.VMEM((s, HEAD_DIM), BF16),
            pltpu.VMEM((s, 2 * HEAD_DIM), BF16),
            pltpu.VMEM((2, DIFF_MAPS * ATT_TILE, s), F32),
            pltpu.VMEM((2, DIFF_MAPS * ATT_TILE, s), BF16),
        ],
        compiler_params=_params(("parallel", "parallel")),
        name="diff",
    )(proj, proj, proj, both(q_gain), both(k_gain), lam_params, subln.reshape(1, HEAD_DIM), strips)


def _memkv_kernel(m_ref, g_ref, w_ref, kg_ref, k_ref, v_ref):
    hm = _rms(m_ref[0], g_ref[...]).astype(BF16)
    kv = _dot(hm, w_ref[...])
    width = MEM_HEADS * HEAD_DIM
    for h in range(MEM_HEADS):
        cols = slice(h * HEAD_DIM, (h + 1) * HEAD_DIM)
        k_ref[0, :, cols] = _rms(kv[:, cols], kg_ref[...]).astype(BF16)
    v_ref[0] = kv[:, width:].astype(BF16)


def _memkv(mem, gain, wkv, k_gain, l):
    b, m, d = mem.shape
    width = MEM_HEADS * HEAD_DIM
    out = jax.ShapeDtypeStruct((b, m, width), BF16)
    return pl.pallas_call(
        _memkv_kernel,
        out_shape=(out, out),
        grid=(b,),
        in_specs=[
            pl.BlockSpec((1, m, d), lambda i: (i, 0, 0)),
            pl.BlockSpec((1, d), lambda i: (0, 0)),
            pl.BlockSpec((None, d, 2 * width), lambda i: (l, 0, 0)),
            pl.BlockSpec((1, HEAD_DIM), lambda i: (0, 0)),
        ],
        out_specs=(pl.BlockSpec((1, m, width), lambda i: (i, 0, 0)), pl.BlockSpec((1, m, width), lambda i: (i, 0, 0))),
        compiler_params=_params(("parallel",)),
        name="memkv",
    )(mem, gain.reshape(1, d), wkv, k_gain.reshape(1, HEAD_DIM))


def _mixout_kernel(x_ref, yc_ref, ym_ref, yd_ref, w_ref, g_ref, wq_ref, qg_ref, k_ref, v_ref, wo_ref, o_ref, ob_ref):
    c0 = CONV_CH
    c1 = c0 + MOBA_HEADS * HEAD_DIM
    x1 = (x_ref[0] + _dot(yc_ref[0], w_ref[0:c0, :]) + _dot(ym_ref[0], w_ref[c0:c1, :])
          + _dot(yd_ref[0], w_ref[c1:, :]))
    q = _dot(_rms(x1, g_ref[...]).astype(BF16), wq_ref[...])
    mean_mat = _group_mean_matrix(HEAD_DIM, HEAD_DIM)
    for h in range(MEM_HEADS):
        cols = slice(h * HEAD_DIM, (h + 1) * HEAD_DIM)
        qh = (_group_rms(q[:, cols], qg_ref[...], mean_mat) * (HEAD_DIM ** -0.5 * LOG2E)).astype(BF16)
        s = _dot_nt(qh, k_ref[0, :, cols])
        p = jnp.exp2(s - jnp.max(s, axis=-1, keepdims=True))
        oh = _dot(p.astype(BF16), v_ref[0, :, cols]) / jnp.sum(p, axis=-1, keepdims=True)
        ob_ref[:, cols] = oh.astype(BF16)
    o_ref[0] = x1 + _dot(ob_ref[...], wo_ref[...])


def _mixout(x, yc, ym, yd, w, gain, wq, q_gain, kn, v, wo, l, tm=512):
    b, s, d = x.shape
    m = kn.shape[1]
    width = MEM_HEADS * HEAD_DIM
    once = pl.Buffered(1)
    rows = lambda a: pl.BlockSpec((1, tm, a.shape[2]), lambda bi, i: (bi, i, 0))
    return pl.pallas_call(
        _mixout_kernel,
        out_shape=jax.ShapeDtypeStruct((b, s, d), F32),
        grid=(b, s // tm),
        in_specs=[
            rows(x), rows(yc), rows(ym), rows(yd),
            pl.BlockSpec((None, w.shape[1], d), lambda bi, i: (l, 0, 0), pipeline_mode=once),
            pl.BlockSpec((1, d), lambda bi, i: (0, 0)),
            pl.BlockSpec((None, d, width), lambda bi, i: (l, 0, 0), pipeline_mode=once),
            pl.BlockSpec((1, HEAD_DIM), lambda bi, i: (0, 0)),
            pl.BlockSpec((1, m, width), lambda bi, i: (bi, 0, 0)),
            pl.BlockSpec((1, m, width), lambda bi, i: (bi, 0, 0)),
            pl.BlockSpec((None, width, d), lambda bi, i: (l, 0, 0), pipeline_mode=once),
        ],
        out_specs=rows(x),
        scratch_shapes=[pltpu.VMEM((tm, width), BF16)],
        compiler_params=_params(("parallel", "parallel")),
        name="mixout",
    )(x, yc, ym, yd, w, gain.reshape(1, d), wq, q_gain.reshape(1, HEAD_DIM), kn, v, wo)


def kernel(x, mem, rel_bias, ffn1_norm, ffn1_w_gate, ffn1_w_up, ffn1_w_down, mix_norm, w_in, conv_w, conv_b, conv_ln_g, conv_ln_b, moba_q_norm, moba_k_norm, diff_q_norm, diff_k_norm, diff_lambda, diff_subln, w_out, mem_norm_x, mem_norm_m, mem_wq, mem_wkv, mem_q_norm, mem_k_norm, mem_wo, ffn2_norm, ffn2_w_gate, ffn2_w_up, ffn2_w_down):
    b, s, d = x.shape
    depth = w_in.shape[0]
    t = b * s
    tab_flat = rel_bias.T.reshape(-1)
    strips = _bias_strips(tab_flat)
    ffn1_f32 = (ffn1_w_gate, ffn1_w_up, ffn1_w_down)
    ffn2_f32 = (ffn2_w_gate, ffn2_w_up, ffn2_w_down)
    ffn_w, ffn_tf = ffn1_f32, FFN_BLOCK_F32
    xt = x.reshape(t, d)
    for l in range(depth):
        xt, (*ffn_w, w_in_l) = _ffn(xt, ffn1_norm[l], *ffn_w, 0, side=[(w, l) for w in (*ffn2_f32, w_in)], tf=ffn_tf)
        ffn_tf = FFN_BLOCK_BF16
        proj = _proj(xt, mix_norm[l], w_in_l, 0).reshape(b, s, PROJ_WIDTH)
        y_conv, (w_out_l, wq_l, wkv_l, wo_l) = _conv(proj, conv_w[l], conv_b[l], conv_ln_g[l], conv_ln_b[l],
                                                    side=[(w, l) for w in (w_out, mem_wq, mem_wkv, mem_wo)])
        y_moba = _moba(proj, strips, moba_q_norm[l], moba_k_norm[l])
        lam_init = 0.8 - 0.6 * math.exp(-0.3 * l)
        y_diff = _diff(proj, strips, diff_q_norm[l], diff_k_norm[l], diff_lambda[l], diff_subln[l], lam_init)
        kn, v = _memkv(mem, mem_norm_m[l], wkv_l, mem_k_norm[l], 0)
        xt = _mixout(xt.reshape(b, s, d), y_conv, y_moba, y_diff, w_out_l, mem_norm_x[l], wq_l, mem_q_norm[l], kn, v,
                     wo_l, 0).reshape(t, d)
        nxt = [(w, l + 1) for w in ffn1_f32] if l + 1 < depth else []
        xt, ffn_w = _ffn(xt, ffn2_norm[l], *ffn_w, 0, side=nxt, tf=ffn_tf)
    return xt.reshape(b, s, d)
```

```python
import functools
import math

import jax
import jax.numpy as jnp
from jax import lax
from jax.experimental import pallas as pl
from jax.experimental.pallas import tpu as pltpu

F32 = jnp.float32
BF16 = jnp.bfloat16

D_MODEL = 2048
D_FF = 5632
CONV_CH = 512
CONV_WIDTH = 31
CONV_PAD = 32
HEAD_DIM = 128
LANES = 128
BF16_SUBLANES = 16
MOBA_HEADS = 6
MOBA_BLOCK = 256
MOBA_TOPK = 3
DIFF_HEADS = 6
DIFF_QK_DIM = 64
MEM_HEADS = 4
N_BUCKETS = 32
MAX_DISTANCE = 128
N_BIAS_HEADS = MOBA_HEADS + DIFF_HEADS
NEG_INF = -1e30
TAKEN = -3e38
PROJ_WIDTH = 5632
COL_CONV_A = 0
COL_CONV_G = CONV_CH // HEAD_DIM
COL_MOBA_Q = 2 * CONV_CH // HEAD_DIM
COL_MOBA_K = COL_MOBA_Q + MOBA_HEADS
COL_MOBA_V = COL_MOBA_K + MOBA_HEADS
COL_DIFF_Q = COL_MOBA_V + MOBA_HEADS
COL_DIFF_K = COL_DIFF_Q + DIFF_HEADS
COL_DIFF_V = COL_DIFF_K + DIFF_HEADS

ATT_TILE = 256
assert ATT_TILE == MOBA_BLOCK and ATT_TILE >= MAX_DISTANCE
LOG2E = 1.4426950408889634
VMEM_LIMIT = 56 * 1024 * 1024


def _params(semantics, vmem=VMEM_LIMIT):
    return pltpu.CompilerParams(dimension_semantics=semantics, vmem_limit_bytes=vmem)


def _rms(x, gain, eps=1e-6):
    return x * lax.rsqrt(jnp.mean(x * x, axis=-1, keepdims=True) + eps) * gain


def _dot(a, b):
    return jnp.dot(a, b, preferred_element_type=F32)


def _dot_nt(a, b, precision=None):
    return lax.dot_general(a, b, (((1,), (1,)), ((), ())), precision=precision, preferred_element_type=F32)


def _ffn_kernel(n_done, emit_own, n_side, x_hbm, g_ref, wg_ref, wu_ref, wd_ref, *refs):
    done_hbm, refs = (refs[0], refs[1:]) if n_done else (None, refs)
    n_own = 3 if emit_own else 0
    side_in, o_ref, refs = refs[:n_side], refs[n_side], refs[n_side + 1:]
    own_out, side_out, (h_ref, x_ref, x_sem) = refs[:n_own], refs[n_own:n_own + n_side], refs[n_own + n_side:]
    i, k = pl.program_id(0), pl.program_id(1)
    tm = x_ref.shape[0]

    def x_copy(block, src=x_hbm):
        return pltpu.make_async_copy(src.at[pl.ds(pl.multiple_of(block * tm, tm), tm), :], x_ref, x_sem)

    @pl.when(jnp.logical_and(i == 0, k == 0))
    def _():
        x_copy(0, done_hbm if n_done else x_hbm).start()

    @pl.when(k == 0)
    def _():
        x_copy(i).wait()
        o_ref[...] = x_ref[...]

    @pl.when(jnp.logical_and(k == 0, i >= n_done))
    def _():
        h_ref[...] = _rms(x_ref[...], g_ref[...]).astype(BF16)

    @pl.when(jnp.logical_and(k == 1, i + 1 < pl.num_programs(0)))
    def _():
        x_copy(i + 1).start()

    def compute():
        h = h_ref[...]
        weights = [w_ref[...].astype(BF16) for w_ref in (wg_ref, wu_ref, wd_ref)]
        for w, wb_ref in zip(weights, own_out):
            wb_ref[...] = w
        gate = _dot(h, weights[0])
        up = _dot(h, weights[1])
        act = (0.5 * gate / (1.0 + jnp.exp(-gate)) * up).astype(BF16)
        o_ref[...] += _dot(act, weights[2])
        for w_ref, wb_ref in zip(side_in, side_out):
            wb_ref[...] = w_ref[...].astype(BF16)

    if n_done:
        pl.when(i >= n_done)(compute)
    else:
        compute()


def _side_block(shape, gi, gk):
    k, n = shape
    si = 1 << (gi.bit_length() - 1)

    def clamp(i, kk):
        return jnp.clip(i, 0, si - 1), jnp.where(i < 0, 0, jnp.where(i < si, kk, gk - 1))

    if k % si == 0 and n % gk == 0 and (k // si) % BF16_SUBLANES == 0 and (n // gk) % LANES == 0:
        return (k // si, n // gk), (lambda i, kk: clamp(i, kk))
    assert k % gk == 0 and n % si == 0 and (k // gk) % BF16_SUBLANES == 0 and (n // si) % LANES == 0, shape
    return (k // gk, n // si), (lambda i, kk: clamp(i, kk)[::-1])


FFN_BLOCK_BF16 = 512
FFN_BLOCK_F32 = 256
FFN_TOKENS = 1024
FFN_VMEM_LIMIT = 60 * 1024 * 1024


def _ffn(x, gain, wg, wu, wd, l, side=(), done=None, n_blocks=None, emit_own=False, tm=FFN_TOKENS,
         tf=FFN_BLOCK_BF16):
    t, d = x.shape
    f = wg.shape[2]
    gi = n_blocks or t // tm
    gk = f // tf
    n_done = 0 if done is None else done.shape[0] // tm
    kk = (lambda i, k: jnp.where(i < n_done, 0, k)) if n_done else (lambda i, k: k)
    side_in_specs, side_out_specs, side_shapes = [], [], []
    for w, ls in side:
        blk, order = _side_block(w.shape[1:], gi - n_done, gk)
        side_in_specs.append(pl.BlockSpec((None, *blk), lambda i, k, ls=ls, o=order: (ls, *o(i - n_done, k))))
        side_out_specs.append(pl.BlockSpec((None, *blk), lambda i, k, o=order: (0, *o(i - n_done, k))))
        side_shapes.append(jax.ShapeDtypeStruct((1, *w.shape[1:]), BF16))
    up_spec = lambda lyr: pl.BlockSpec((None, d, tf), lambda i, k: (lyr, 0, kk(i, k)))
    down_spec = lambda lyr: pl.BlockSpec((None, tf, d), lambda i, k: (lyr, kk(i, k), 0))
    own_specs = [up_spec(0), up_spec(0), down_spec(0)] if emit_own else []
    own_shapes = [jax.ShapeDtypeStruct((1, *w.shape[1:]), BF16) for w in (wg, wu, wd)] if emit_own else []
    any_spec = pl.BlockSpec(memory_space=pl.ANY)
    out, *cast = pl.pallas_call(
        functools.partial(_ffn_kernel, n_done, emit_own, len(side)),
        out_shape=[jax.ShapeDtypeStruct((gi * tm, d), F32), *own_shapes, *side_shapes],
        grid=(gi, gk),
        in_specs=[
            any_spec,
            pl.BlockSpec((1, d), lambda i, k: (0, 0)),
            up_spec(l),
            up_spec(l),
            down_spec(l),
            *([any_spec] if n_done else []),
            *side_in_specs,
        ],
        out_specs=[pl.BlockSpec((tm, d), lambda i, k: (i, 0)), *own_specs, *side_out_specs],
        scratch_shapes=[pltpu.VMEM((tm, d), BF16), pltpu.VMEM((tm, d), F32), pltpu.SemaphoreType.DMA(())],
        compiler_params=_params(("arbitrary", "arbitrary"), vmem=FFN_VMEM_LIMIT),
        name="ffn",
    )(x, gain.reshape(1, d), wg, wu, wd, *([done] if n_done else []), *(w for w, _ in side))
    return out, cast[:len(own_shapes)], cast[len(own_shapes):]


def _proj_kernel(x_ref, g_ref, w_ref, o_ref, h_ref):
    @pl.when(pl.program_id(1) == 0)
    def _():
        h_ref[...] = _rms(x_ref[...], g_ref[...]).astype(BF16)

    o_ref[...] = _dot(h_ref[...], w_ref[...]).astype(o_ref.dtype)


def _proj(x, gain, w, l, tm=2048, tn=512):
    t, d = x.shape
    n = w.shape[2]
    return pl.pallas_call(
        _proj_kernel,
        out_shape=jax.ShapeDtypeStruct((t, n), BF16),
        grid=(t // tm, n // tn),
        in_specs=[
            pl.BlockSpec((tm, d), lambda i, j: (i, 0)),
            pl.BlockSpec((1, d), lambda i, j: (0, 0)),
            pl.BlockSpec((None, d, tn), lambda i, j: (l, 0, j)),
        ],
        out_specs=pl.BlockSpec((tm, tn), lambda i, j: (i, j)),
        scratch_shapes=[pltpu.VMEM((tm, d), BF16)],
        compiler_params=_params(("parallel", "arbitrary")),
        name="proj",
    )(x, gain.reshape(1, d), w)


def _bias_kernel(tab_ref, o_ref):
    h = pl.program_id(0)
    far = tab_ref[h * N_BUCKETS + N_BUCKETS - 1]
    shape = (ATT_TILE, 2 * ATT_TILE)
    dist = ATT_TILE + lax.broadcasted_iota(jnp.int32, shape, 0) - lax.broadcasted_iota(jnp.int32, shape, 1)
    dist = jnp.maximum(dist, 0)
    max_exact = N_BUCKETS // 2
    log_ratio = jnp.log(jnp.maximum(dist, 1).astype(F32) / max_exact) / math.log(MAX_DISTANCE / max_exact)
    large = jnp.minimum(max_exact + (log_ratio * (N_BUCKETS - max_exact)).astype(jnp.int32), N_BUCKETS - 1)
    bucket = jnp.where(dist < max_exact, dist, large)
    out = jnp.zeros(shape, F32)
    for b in range(N_BUCKETS):
        out = jnp.where(bucket == b, (tab_ref[h * N_BUCKETS + b] - far) * LOG2E, out)
    o_ref[0] = out


def _bias_strips(tab_flat):
    return pl.pallas_call(
        _bias_kernel,
        out_shape=jax.ShapeDtypeStruct((N_BIAS_HEADS, ATT_TILE, 2 * ATT_TILE), F32),
        grid=(N_BIAS_HEADS,),
        in_specs=[pl.BlockSpec(memory_space=pltpu.SMEM)],
        out_specs=pl.BlockSpec((1, ATT_TILE, 2 * ATT_TILE), lambda h: (h, 0, 0)),
        compiler_params=_params(("arbitrary",)),
        name="bias_strips",
    )(tab_flat)


CONV_ROWS = 32
CONV_UNROLL = 4
SUBLANES = 8
assert CONV_PAD % SUBLANES == 0 and CONV_PAD >= SUBLANES * -(-CONV_WIDTH // SUBLANES)


def _conv_kernel(n_side, a_ref, g_ref, w_ref, cb_ref, lg_ref, lb_ref, *refs):
    side_in, o_ref, side_out, u_ref = refs[:n_side], refs[n_side], refs[n_side + 1:2 * n_side + 1], refs[-1]
    for wf_ref, wb_ref in zip(side_in, side_out):
        wb_ref[...] = wf_ref[...].astype(BF16)
    s = a_ref.shape[1]
    u_ref[0:CONV_PAD, :] = jnp.zeros((CONV_PAD, CONV_CH), F32)

    def glu(r, carry):
        rows = pl.ds(pl.multiple_of(r * CONV_ROWS, CONV_ROWS), CONV_ROWS)
        a = a_ref[0, rows, :].astype(F32)
        g = g_ref[0, rows, :].astype(F32)
        u_ref[pl.ds(pl.multiple_of(CONV_PAD + r * CONV_ROWS, CONV_ROWS), CONV_ROWS), :] = a / (1.0 + jnp.exp(-g))
        return carry

    lax.fori_loop(0, s // CONV_ROWS, glu, 0)

    win_rows = CONV_ROWS + CONV_PAD
    n_groups = CONV_CH // LANES

    def conv(r, carry):
        base = pl.multiple_of(r * CONV_ROWS, CONV_ROWS)
        accs = []
        for c in range(n_groups):
            lanes = slice(c * LANES, (c + 1) * LANES)
            win = u_ref[pl.ds(base, win_rows), lanes]
            acc = jnp.zeros((CONV_ROWS, LANES), F32) + cb_ref[:, lanes]
            for b in range(SUBLANES):
                shifted = pltpu.roll(win, b, 0) if b else win
                for a in range(-(-CONV_WIDTH // SUBLANES)):
                    d = SUBLANES * a + b
                    if d < CONV_WIDTH:
                        k0 = CONV_PAD - SUBLANES * a
                        tap = CONV_WIDTH - 1 - d
                        acc = acc + shifted[k0:k0 + CONV_ROWS, :] * w_ref[tap:tap + 1, lanes]
            accs.append(acc)
        mu = sum(jnp.sum(acc, axis=-1, keepdims=True) for acc in accs) / float(CONV_CH)
        cens = [acc - mu for acc in accs]
        var = sum(jnp.sum(cen * cen, axis=-1, keepdims=True) for cen in cens) / float(CONV_CH)
        inv = lax.rsqrt(var + 1e-5)
        for c, cen in enumerate(cens):
            lanes = slice(c * LANES, (c + 1) * LANES)
            y = cen * inv * lg_ref[:, lanes] + lb_ref[:, lanes]
            o_ref[0, pl.ds(base, CONV_ROWS), lanes] = (y / (1.0 + jnp.exp(-y))).astype(o_ref.dtype)
        return carry

    lax.fori_loop(0, s // CONV_ROWS, conv, 0, unroll=CONV_UNROLL)


def _conv(proj, w, cb, lg, lb, side=()):
    b, s, _ = proj.shape
    vec = lambda v: v.reshape(1, CONV_CH)
    side_in_specs, side_out_specs, side_shapes = [], [], []
    for wf, ls in side:
        k, n = wf.shape[1:]
        assert k % b == 0 and (k // b) % BF16_SUBLANES == 0, wf.shape
        side_in_specs.append(pl.BlockSpec((None, k // b, n), lambda i, ls=ls: (ls, i, 0)))
        side_out_specs.append(pl.BlockSpec((None, k // b, n), lambda i: (0, i, 0)))
        side_shapes.append(jax.ShapeDtypeStruct((1, k, n), BF16))
    out, *cast = pl.pallas_call(
        functools.partial(_conv_kernel, len(side)),
        out_shape=[jax.ShapeDtypeStruct((b, s, CONV_CH), BF16), *side_shapes],
        grid=(b,),
        in_specs=[
            pl.BlockSpec((1, s, CONV_CH), lambda i: (i, 0, 0)),
            pl.BlockSpec((1, s, CONV_CH), lambda i: (i, 0, 1)),
            pl.BlockSpec((CONV_WIDTH, CONV_CH), lambda i: (0, 0)),
            pl.BlockSpec((1, CONV_CH), lambda i: (0, 0)),
            pl.BlockSpec((1, CONV_CH), lambda i: (0, 0)),
            pl.BlockSpec((1, CONV_CH), lambda i: (0, 0)),
            *side_in_specs,
        ],
        out_specs=[pl.BlockSpec((1, s, CONV_CH), lambda i: (i, 0, 0)), *side_out_specs],
        scratch_shapes=[pltpu.VMEM((s + CONV_PAD, CONV_CH), F32)],
        compiler_params=_params(("parallel",)),
        name="conv",
    )(proj, proj, w, vec(cb), vec(lg), vec(lb), *(wf for wf, _ in side))
    return out, cast


def _tile_rows(j):
    return slice(j * ATT_TILE, (j + 1) * ATT_TILE)


def _lane_fold(op, x):
    out = x[:, :LANES]
    for c in range(1, x.shape[1] // LANES):
        out = op(out, x[:, c * LANES:(c + 1) * LANES])
    return out


def _group_mean_matrix(width, group):
    r = lax.broadcasted_iota(jnp.int32, (width, width), 0) // group
    c = lax.broadcasted_iota(jnp.int32, (width, width), 1) // group
    return jnp.where(r == c, 1.0 / group, 0.0).astype(BF16)


def _group_rms(x, gain, mean_mat, eps=1e-6):
    ms = _dot((x * x).astype(BF16), mean_mat)
    return x * lax.rsqrt(ms + eps) * gain


def _ones_columns(v_ref, vo_ref):
    vo_ref[:, :HEAD_DIM] = v_ref[...]
    vo_ref[:, HEAD_DIM:] = jnp.ones((v_ref.shape[0], HEAD_DIM), BF16)


def _logits_pass(i, q, k_ref, strip_ref, sel_add, s_ref):
    maps = q.shape[0] // ATT_TILE
    shape = (q.shape[0], ATT_TILE)
    mask = lax.broadcasted_iota(jnp.int32, shape, 0) % ATT_TILE >= lax.broadcasted_iota(jnp.int32, shape, 1)
    rep = lambda x: x if maps == 1 else jnp.concatenate([x] * maps, axis=0)
    m_acc = None
    for j in range(i + 1):
        s = _dot_nt(q, k_ref[_tile_rows(j), :])
        if j == i:
            s = jnp.where(mask, s + rep(strip_ref[0, :, ATT_TILE:]), NEG_INF)
        elif j == i - 1:
            s = s + rep(strip_ref[0, :, :ATT_TILE])
        if sel_add is not None and j < i:
            s = s + sel_add[:, j:j + 1]
        s_ref[:, _tile_rows(j)] = s
        t = _lane_fold(jnp.maximum, s)
        m_acc = t if m_acc is None else jnp.maximum(m_acc, t)
    return jnp.max(m_acc, axis=-1, keepdims=True)


def _softmax_pass(i, m, vo_ref, s_ref, p_ref):
    for j in range(i + 1):
        p_ref[:, _tile_rows(j)] = jnp.exp2(s_ref[:, _tile_rows(j)] - m).astype(BF16)
    n = (i + 1) * ATT_TILE
    out = _dot(p_ref[:, :n], vo_ref[:n, :])
    return out[:, :HEAD_DIM], out[:, HEAD_DIM:]


def _moba_kernel(q_ref, k_ref, v_ref, qg_ref, kg_ref, strip_ref, o_ref, kn_ref, vo_ref, kmean_ref, s_ref, p_ref):
    nb = k_ref.shape[1] // MOBA_BLOCK
    mean_mat = _group_mean_matrix(HEAD_DIM, HEAD_DIM)
    _ones_columns(v_ref.at[0], vo_ref)
    for j in range(nb):
        rows = _tile_rows(j)
        kn = _group_rms(k_ref[0, rows, :].astype(F32), kg_ref[...], mean_mat)
        kn_ref[rows, :] = kn.astype(BF16)
        kmean_ref[j:j + 1, :] = jnp.sum(kn, axis=0, keepdims=True) / float(MOBA_BLOCK)

    eye = (lax.broadcasted_iota(jnp.int32, (MOBA_BLOCK, MOBA_BLOCK), 0)
           == lax.broadcasted_iota(jnp.int32, (MOBA_BLOCK, MOBA_BLOCK), 1)).astype(F32).astype(BF16)
    n_sel = min(MOBA_TOPK, nb)

    def logits(i):
        qn = _group_rms(q_ref[0, _tile_rows(i), :].astype(F32), qg_ref[...], mean_mat)
        sel_add = None
        if i > n_sel:
            gate = _dot_nt(kmean_ref[...], qn, precision=lax.Precision.HIGHEST)
            blk = lax.broadcasted_iota(jnp.int32, gate.shape, 0)
            gate = jnp.where(blk < i, gate, TAKEN)
            sel = jnp.zeros(gate.shape, F32)
            for _ in range(n_sel):
                top = jnp.max(gate, axis=0, keepdims=True)
                first = jnp.min(jnp.where(gate == top, blk, nb), axis=0, keepdims=True)
                pick = blk == first
                sel = jnp.where(pick, 1.0, sel)
                gate = jnp.where(pick, TAKEN, gate)
            sel_t = _dot_nt(eye, sel.astype(BF16))
            sel_add = (sel_t - 1.0) * -NEG_INF
        q = (qn * (HEAD_DIM ** -0.5 * LOG2E)).astype(BF16)
        return _logits_pass(i, q, kn_ref, strip_ref, sel_add, s_ref.at[i % 2])

    m_next = logits(0)
    for i in range(nb):
        m = m_next
        if i + 1 < nb:
            m_next = logits(i + 1)
        acc, l = _softmax_pass(i, m, vo_ref, s_ref.at[i % 2], p_ref.at[i % 2])
        o_ref[0, _tile_rows(i), :] = (acc / l).astype(o_ref.dtype)


def _moba(proj, strips, q_gain, k_gain):
    b, s, _ = proj.shape
    nb = s // MOBA_BLOCK
    return pl.pallas_call(
        _moba_kernel,
        out_shape=jax.ShapeDtypeStruct((b, s, MOBA_HEADS * HEAD_DIM), BF16),
        grid=(b, MOBA_HEADS),
        in_specs=[
            pl.BlockSpec((1, s, HEAD_DIM), lambda bi, h: (bi, 0, COL_MOBA_Q + h)),
            pl.BlockSpec((1, s, HEAD_DIM), lambda bi, h: (bi, 0, COL_MOBA_K + h)),
            pl.BlockSpec((1, s, HEAD_DIM), lambda bi, h: (bi, 0, COL_MOBA_V + h)),
            pl.BlockSpec((1, HEAD_DIM), lambda bi, h: (0, 0)),
            pl.BlockSpec((1, HEAD_DIM), lambda bi, h: (0, 0)),
            pl.BlockSpec((1, ATT_TILE, 2 * ATT_TILE), lambda bi, h: (h, 0, 0)),
        ],
        out_specs=pl.BlockSpec((1, s, HEAD_DIM), lambda bi, h: (bi, 0, h)),
        scratch_shapes=[
            pltpu.VMEM((s, HEAD_DIM), BF16),
            pltpu.VMEM((s, 2 * HEAD_DIM), BF16),
            pltpu.VMEM((nb, HEAD_DIM), F32),
            pltpu.VMEM((2, ATT_TILE, s), F32),
            pltpu.VMEM((2, ATT_TILE, s), BF16),
        ],
        compiler_params=_params(("parallel", "parallel")),
        name="moba",
    )(proj, proj, proj, q_gain.reshape(1, HEAD_DIM), k_gain.reshape(1, HEAD_DIM), strips)


DIFF_MAPS = 2


def _diff_kernel(lam_init, q_ref, k_ref, v_ref, qg_ref, kg_ref, lam_ref, sg_ref, strip_ref, o_ref,
                 kn_ref, vo_ref, s_ref, p_ref):
    nt = k_ref.shape[1] // ATT_TILE
    mean_mat = _group_mean_matrix(HEAD_DIM, DIFF_QK_DIM)
    _ones_columns(v_ref.at[0], vo_ref)
    for j in range(nt):
        rows = _tile_rows(j)
        kn_ref[rows, :] = _group_rms(k_ref[0, rows, :].astype(F32), kg_ref[...], mean_mat).astype(BF16)

    lp = lam_ref[...]
    lam = (jnp.exp(jnp.sum(lp[0:1] * lp[1:2], axis=-1, keepdims=True))
           - jnp.exp(jnp.sum(lp[2:3] * lp[3:4], axis=-1, keepdims=True)) + lam_init)
    scale = DIFF_QK_DIM ** -0.5 * LOG2E
    lane_map = lax.broadcasted_iota(jnp.int32, (ATT_TILE, HEAD_DIM), 1) // DIFF_QK_DIM

    def logits(i):
        qn = _group_rms(q_ref[0, _tile_rows(i), :].astype(F32), qg_ref[...], mean_mat) * scale
        q = jnp.concatenate([jnp.where(lane_map == c, qn, 0.0) for c in range(DIFF_MAPS)], axis=0).astype(BF16)
        return _logits_pass(i, q, kn_ref, strip_ref, None, s_ref.at[i % 2])

    m_next = logits(0)
    for i in range(nt):
        m = m_next
        if i + 1 < nt:
            m_next = logits(i + 1)
        acc, l = _softmax_pass(i, m, vo_ref, s_ref.at[i % 2], p_ref.at[i % 2])
        a = acc / l
        o = a[:ATT_TILE] - lam * a[ATT_TILE:]
        o_ref[0, _tile_rows(i), :] = (_rms(o, sg_ref[...]) * (1.0 - lam_init)).astype(o_ref.dtype)


def _diff(proj, strips, q_gain, k_gain, lam_params, subln, lam_init):
    b, s, _ = proj.shape
    both = lambda g: jnp.tile(g, DIFF_MAPS).reshape(1, HEAD_DIM)
    return pl.pallas_call(
        functools.partial(_diff_kernel, lam_init),
        out_shape=jax.ShapeDtypeStruct((b, s, DIFF_HEADS * HEAD_DIM), BF16),
        grid=(b, DIFF_HEADS),
        in_specs=[
            pl.BlockSpec((1, s, HEAD_DIM), lambda bi, h: (bi, 0, COL_DIFF_Q + h)),
            pl.BlockSpec((1, s, HEAD_DIM), lambda bi, h: (bi, 0, COL_DIFF_K + h)),
            pl.BlockSpec((1, s, HEAD_DIM), lambda bi, h: (bi, 0, COL_DIFF_V + h)),
            pl.BlockSpec((1, HEAD_DIM), lambda bi, h: (0, 0)),
            pl.BlockSpec((1, HEAD_DIM), lambda bi, h: (0, 0)),
            pl.BlockSpec((4, DIFF_QK_DIM), lambda bi, h: (0, 0)),
            pl.BlockSpec((1, HEAD_DIM), lambda bi, h: (0, 0)),
            pl.BlockSpec((1, ATT_TILE, 2 * ATT_TILE), lambda bi, h: (MOBA_HEADS + h, 0, 0)),
        ],
        out_specs=pl.BlockSpec((1, s, HEAD_DIM), lambda bi, h: (bi, 0, h)),
        scratch_shapes=[
            pltpu.VMEM((s, HEAD_DIM), BF16),
            pltpu.VMEM((s, 2 * HEAD_DIM), BF16),
            pltpu.VMEM((2, DIFF_MAPS * ATT_TILE, s), F32),
            pltpu.VMEM((2, DIFF_MAPS * ATT_TILE, s), BF16),
        ],
        compiler_params=_params(("parallel", "parallel")),
        name="diff",
    )(proj, proj, proj, both(q_gain), both(k_gain), lam_params, subln.reshape(1, HEAD_DIM), strips)


def _memkv_kernel(m_ref, g_ref, w_ref, kg_ref, k_ref, v_ref):
    hm = _rms(m_ref[0], g_ref[...]).astype(BF16)
    kv = _dot(hm, w_ref[...])
    width = MEM_HEADS * HEAD_DIM
    for h in range(MEM_HEADS):
        cols = slice(h * HEAD_DIM, (h + 1) * HEAD_DIM)
        k_ref[0, :, cols] = _rms(kv[:, cols], kg_ref[...]).astype(BF16)
    v_ref[0] = kv[:, width:].astype(BF16)


def _memkv(mem, gain, wkv, k_gain, l):
    b, m, d = mem.shape
    width = MEM_HEADS * HEAD_DIM
    out = jax.ShapeDtypeStruct((b, m, width), BF16)
    return pl.pallas_call(
        _memkv_kernel,
        out_shape=(out, out),
        grid=(b,),
        in_specs=[
            pl.BlockSpec((1, m, d), lambda i: (i, 0, 0)),
            pl.BlockSpec((1, d), lambda i: (0, 0)),
            pl.BlockSpec((None, d, 2 * width), lambda i: (l, 0, 0)),
            pl.BlockSpec((1, HEAD_DIM), lambda i: (0, 0)),
        ],
        out_specs=(pl.BlockSpec((1, m, width), lambda i: (i, 0, 0)), pl.BlockSpec((1, m, width), lambda i: (i, 0, 0))),
        compiler_params=_params(("parallel",)),
        name="memkv",
    )(mem, gain.reshape(1, d), wkv, k_gain.reshape(1, HEAD_DIM))


def _mixout_kernel(x_ref, yc_ref, ym_ref, yd_ref, w_ref, g_ref, wq_ref, qg_ref, k_ref, v_ref, wo_ref, o_ref, ob_ref):
    c0 = CONV_CH
    c1 = c0 + MOBA_HEADS * HEAD_DIM
    x1 = (x_ref[0] + _dot(yc_ref[0], w_ref[0:c0, :]) + _dot(ym_ref[0], w_ref[c0:c1, :])
          + _dot(yd_ref[0], w_ref[c1:, :]))
    q = _dot(_rms(x1, g_ref[...]).astype(BF16), wq_ref[...])
    mean_mat = _group_mean_matrix(HEAD_DIM, HEAD_DIM)
    for h in range(MEM_HEADS):
        cols = slice(h * HEAD_DIM, (h + 1) * HEAD_DIM)
        qh = (_group_rms(q[:, cols], qg_ref[...], mean_mat) * (HEAD_DIM ** -0.5 * LOG2E)).astype(BF16)
        s = _dot_nt(qh, k_ref[0, :, cols])
        p = jnp.exp2(s - jnp.max(s, axis=-1, keepdims=True))
        oh = _dot(p.astype(BF16), v_ref[0, :, cols]) / jnp.sum(p, axis=-1, keepdims=True)
        ob_ref[:, cols] = oh.astype(BF16)
    o_ref[0] = x1 + _dot(ob_ref[...], wo_ref[...])


def _mixout(x, yc, ym, yd, w, gain, wq, q_gain, kn, v, wo, l, tm=512):
    b, s, d = x.shape
    m = kn.shape[1]
    width = MEM_HEADS * HEAD_DIM
    once = pl.Buffered(1)
    rows = lambda a: pl.BlockSpec((1, tm, a.shape[2]), lambda bi, i: (bi, i, 0))
    return pl.pallas_call(
        _mixout_kernel,
        out_shape=jax.ShapeDtypeStruct((b, s, d), F32),
        grid=(b, s // tm),
        in_specs=[
            rows(x), rows(yc), rows(ym), rows(yd),
            pl.BlockSpec((None, w.shape[1], d), lambda bi, i: (l, 0, 0), pipeline_mode=once),
            pl.BlockSpec((1, d), lambda bi, i: (0, 0)),
            pl.BlockSpec((None, d, width), lambda bi, i: (l, 0, 0), pipeline_mode=once),
            pl.BlockSpec((1, HEAD_DIM), lambda bi, i: (0, 0)),
            pl.BlockSpec((1, m, width), lambda bi, i: (bi, 0, 0)),
            pl.BlockSpec((1, m, width), lambda bi, i: (bi, 0, 0)),
            pl.BlockSpec((None, width, d), lambda bi, i: (l, 0, 0), pipeline_mode=once),
        ],
        out_specs=rows(x),
        scratch_shapes=[pltpu.VMEM((tm, width), BF16)],
        compiler_params=_params(("parallel", "parallel")),
        name="mixout",
    )(x, yc, ym, yd, w, gain.reshape(1, d), wq, q_gain.reshape(1, HEAD_DIM), kn, v, wo)


def kernel(x, mem, rel_bias, ffn1_norm, ffn1_w_gate, ffn1_w_up, ffn1_w_down, mix_norm, w_in, conv_w, conv_b, conv_ln_g, conv_ln_b, moba_q_norm, moba_k_norm, diff_q_norm, diff_k_norm, diff_lambda, diff_subln, w_out, mem_norm_x, mem_norm_m, mem_wq, mem_wkv, mem_q_norm, mem_k_norm, mem_wo, ffn2_norm, ffn2_w_gate, ffn2_w_up, ffn2_w_down):
    b, s, d = x.shape
    depth = w_in.shape[0]
    t = b * s
    tab_flat = rel_bias.T.reshape(-1)
    strips = _bias_strips(tab_flat)
    ffn1_f32 = (ffn1_w_gate, ffn1_w_up, ffn1_w_down)
    ffn2_f32 = (ffn2_w_gate, ffn2_w_up, ffn2_w_down)
    xt = x.reshape(t, d)
    head, ffn_w, _ = _ffn(xt, ffn1_norm[0], *ffn1_f32, 0, n_blocks=1, emit_own=True, tf=FFN_BLOCK_F32)
    for l in range(depth):
        side = [(w, l) for w in (*ffn2_f32, w_in)]
        xt, _, (*ffn_w, w_in_l) = _ffn(xt, ffn1_norm[l], *ffn_w, 0, side=side, done=head if l == 0 else None)
        proj = _proj(xt, mix_norm[l], w_in_l, 0).reshape(b, s, PROJ_WIDTH)
        y_conv, (w_out_l, wq_l, wkv_l, wo_l) = _conv(proj, conv_w[l], conv_b[l], conv_ln_g[l], conv_ln_b[l],
                                                    side=[(w, l) for w in (w_out, mem_wq, mem_wkv, mem_wo)])
        y_moba = _moba(proj, strips, moba_q_norm[l], moba_k_norm[l])
        lam_init = 0.8 - 0.6 * math.exp(-0.3 * l)
        y_diff = _diff(proj, strips, diff_q_norm[l], diff_k_norm[l], diff_lambda[l], diff_subln[l], lam_init)
        kn, v = _memkv(mem, mem_norm_m[l], wkv_l, mem_k_norm[l], 0)
        xt = _mixout(xt.reshape(b, s, d), y_conv, y_moba, y_diff, w_out_l, mem_norm_x[l], wq_l, mem_q_norm[l], kn, v,
                     wo_l, 0).reshape(t, d)
        nxt = [(w, l + 1) for w in ffn1_f32] if l + 1 < depth else []
        xt, _, ffn_w = _ffn(xt, ffn2_norm[l], *ffn_w, 0, side=nxt)
    return xt.reshape(b, s, d)
```

```python
import functools
import math

import jax
import jax.numpy as jnp
from jax import lax
from jax.experimental import pallas as pl
from jax.experimental.pallas import tpu as pltpu

F32 = jnp.float32
BF16 = jnp.bfloat16

CONV_CH = 512
CONV_WIDTH = 31
CONV_PAD = 32
HEAD_DIM = 128
LANES = 128
BF16_SUBLANES = 16
MOBA_HEADS = 6
MOBA_BLOCK = 256
MOBA_TOPK = 3
DIFF_HEADS = 6
DIFF_QK_DIM = 64
MEM_HEADS = 4
N_BUCKETS = 32
MAX_DISTANCE = 128
N_BIAS_HEADS = MOBA_HEADS + DIFF_HEADS
NEG_INF = -1e30
TAKEN = -3e38
PROJ_WIDTH = 5632
COL_CONV_A = 0
COL_CONV_G = CONV_CH // HEAD_DIM
COL_MOBA_Q = 2 * CONV_CH // HEAD_DIM
COL_MOBA_K = COL_MOBA_Q + MOBA_HEADS
COL_MOBA_V = COL_MOBA_K + MOBA_HEADS
COL_DIFF_Q = COL_MOBA_V + MOBA_HEADS
COL_DIFF_K = COL_DIFF_Q + DIFF_HEADS
COL_DIFF_V = COL_DIFF_K + DIFF_HEADS

ATT_TILE = 256
assert ATT_TILE == MOBA_BLOCK and ATT_TILE >= MAX_DISTANCE
LOG2E = 1.4426950408889634
VMEM_LIMIT = 56 * 1024 * 1024


def _params(semantics, vmem=VMEM_LIMIT):
    return pltpu.CompilerParams(dimension_semantics=semantics, vmem_limit_bytes=vmem)


def _rms(x, gain, eps=1e-6):
    return x * lax.rsqrt(jnp.mean(x * x, axis=-1, keepdims=True) + eps) * gain


def _dot(a, b):
    return jnp.dot(a, b, preferred_element_type=F32)


def _dot_nt(a, b, precision=None):
    return lax.dot_general(a, b, (((1,), (1,)), ((), ())), precision=precision, preferred_element_type=F32)


def _ffn_kernel(n_side, x_hbm, g_ref, wg_ref, wu_ref, wd_ref, *refs):
    side_in, o_ref, side_out = refs[:n_side], refs[n_side], refs[n_side + 1:2 * n_side + 1]
    h_ref, x_ref, x_sem = refs[2 * n_side + 1:]
    i, k = pl.program_id(0), pl.program_id(1)
    tm = x_ref.shape[0]

    def x_copy(block):
        return pltpu.make_async_copy(x_hbm.at[pl.ds(pl.multiple_of(block * tm, tm), tm), :], x_ref, x_sem)

    @pl.when(jnp.logical_and(i == 0, k == 0))
    def _():
        x_copy(0).start()

    @pl.when(k == 0)
    def _():
        x_copy(i).wait()
        x = x_ref[...]
        h_ref[...] = _rms(x, g_ref[...]).astype(BF16)
        o_ref[...] = x

    @pl.when(jnp.logical_and(k == 1, i + 1 < pl.num_programs(0)))
    def _():
        x_copy(i + 1).start()

    h = h_ref[...]
    gate = _dot(h, wg_ref[...].astype(BF16))
    up = _dot(h, wu_ref[...].astype(BF16))
    act = (0.5 * gate / (1.0 + jnp.exp(-gate)) * up).astype(BF16)
    o_ref[...] += _dot(act, wd_ref[...].astype(BF16))
    for w_ref, wb_ref in zip(side_in, side_out):
        wb_ref[...] = w_ref[...].astype(BF16)


def _side_block(shape, gi, gk):
    k, n = shape
    if k % gi == 0 and n % gk == 0 and (k // gi) % BF16_SUBLANES == 0 and (n // gk) % LANES == 0:
        return (k // gi, n // gk), (lambda i, kk: (i, kk))
    assert k % gk == 0 and n % gi == 0 and (k // gk) % BF16_SUBLANES == 0 and (n // gi) % LANES == 0, shape
    return (k // gk, n // gi), (lambda i, kk: (kk, i))


FFN_BLOCK_BF16 = 512
FFN_BLOCK_F32 = 256


def _ffn(x, gain, wg, wu, wd, l, side=(), tm=1024, tf=FFN_BLOCK_BF16):
    t, d = x.shape
    f = wg.shape[2]
    gi, gk = t // tm, f // tf
    side_in_specs, side_out_specs, side_shapes = [], [], []
    for w, ls in side:
        blk, order = _side_block(w.shape[1:], gi, gk)
        side_in_specs.append(pl.BlockSpec((None, *blk), lambda i, k, ls=ls, order=order: (ls, *order(i, k))))
        side_out_specs.append(pl.BlockSpec((None, *blk), lambda i, k, order=order: (0, *order(i, k))))
        side_shapes.append(jax.ShapeDtypeStruct((1, *w.shape[1:]), BF16))
    out, *cast = pl.pallas_call(
        functools.partial(_ffn_kernel, len(side)),
        out_shape=[jax.ShapeDtypeStruct((t, d), F32), *side_shapes],
        grid=(gi, gk),
        in_specs=[
            pl.BlockSpec(memory_space=pl.ANY),
            pl.BlockSpec((1, d), lambda i, k: (0, 0)),
            pl.BlockSpec((None, d, tf), lambda i, k: (l, 0, k)),
            pl.BlockSpec((None, d, tf), lambda i, k: (l, 0, k)),
            pl.BlockSpec((None, tf, d), lambda i, k: (l, k, 0)),
            *side_in_specs,
        ],
        out_specs=[pl.BlockSpec((tm, d), lambda i, k: (i, 0)), *side_out_specs],
        scratch_shapes=[pltpu.VMEM((tm, d), BF16), pltpu.VMEM((tm, d), F32), pltpu.SemaphoreType.DMA(())],
        compiler_params=_params(("arbitrary", "arbitrary")),
        name="ffn",
    )(x, gain.reshape(1, d), wg, wu, wd, *(w for w, _ in side))
    return out, cast


def _proj_kernel(x_ref, g_ref, w_ref, o_ref, h_ref):
    @pl.when(pl.program_id(1) == 0)
    def _():
        h_ref[...] = _rms(x_ref[...], g_ref[...]).astype(BF16)

    o_ref[...] = _dot(h_ref[...], w_ref[...]).astype(o_ref.dtype)


def _proj(x, gain, w, l, tm=2048, tn=512):
    t, d = x.shape
    n = w.shape[2]
    return pl.pallas_call(
        _proj_kernel,
        out_shape=jax.ShapeDtypeStruct((t, n), BF16),
        grid=(t // tm, n // tn),
        in_specs=[
            pl.BlockSpec((tm, d), lambda i, j: (i, 0)),
            pl.BlockSpec((1, d), lambda i, j: (0, 0)),
            pl.BlockSpec((None, d, tn), lambda i, j: (l, 0, j)),
        ],
        out_specs=pl.BlockSpec((tm, tn), lambda i, j: (i, j)),
        scratch_shapes=[pltpu.VMEM((tm, d), BF16)],
        compiler_params=_params(("parallel", "arbitrary")),
        name="proj",
    )(x, gain.reshape(1, d), w)


def _bias_kernel(tab_ref, o_ref):
    h = pl.program_id(0)
    far = tab_ref[h * N_BUCKETS + N_BUCKETS - 1]
    shape = (ATT_TILE, 2 * ATT_TILE)
    dist = ATT_TILE + lax.broadcasted_iota(jnp.int32, shape, 0) - lax.broadcasted_iota(jnp.int32, shape, 1)
    dist = jnp.maximum(dist, 0)
    max_exact = N_BUCKETS // 2
    log_ratio = jnp.log(jnp.maximum(dist, 1).astype(F32) / max_exact) / math.log(MAX_DISTANCE / max_exact)
    large = jnp.minimum(max_exact + (log_ratio * (N_BUCKETS - max_exact)).astype(jnp.int32), N_BUCKETS - 1)
    bucket = jnp.where(dist < max_exact, dist, large)
    out = jnp.zeros(shape, F32)
    for b in range(N_BUCKETS):
        out = jnp.where(bucket == b, (tab_ref[h * N_BUCKETS + b] - far) * LOG2E, out)
    o_ref[0] = out


def _bias_strips(tab_flat):
    return pl.pallas_call(
        _bias_kernel,
        out_shape=jax.ShapeDtypeStruct((N_BIAS_HEADS, ATT_TILE, 2 * ATT_TILE), F32),
        grid=(N_BIAS_HEADS,),
        in_specs=[pl.BlockSpec(memory_space=pltpu.SMEM)],
        out_specs=pl.BlockSpec((1, ATT_TILE, 2 * ATT_TILE), lambda h: (h, 0, 0)),
        compiler_params=_params(("arbitrary",)),
        name="bias_strips",
    )(tab_flat)


CONV_ROWS = 32
CONV_UNROLL = 4
SUBLANES = 8
assert CONV_PAD % SUBLANES == 0 and CONV_PAD >= SUBLANES * -(-CONV_WIDTH // SUBLANES)


def _conv_kernel(n_side, a_ref, g_ref, w_ref, cb_ref, lg_ref, lb_ref, *refs):
    side_in, o_ref, side_out, u_ref = refs[:n_side], refs[n_side], refs[n_side + 1:2 * n_side + 1], refs[-1]
    for wf_ref, wb_ref in zip(side_in, side_out):
        wb_ref[...] = wf_ref[...].astype(BF16)
    s = a_ref.shape[1]
    u_ref[0:CONV_PAD, :] = jnp.zeros((CONV_PAD, CONV_CH), F32)

    def glu(r, carry):
        rows = pl.ds(pl.multiple_of(r * CONV_ROWS, CONV_ROWS), CONV_ROWS)
        a = a_ref[0, rows, :].astype(F32)
        g = g_ref[0, rows, :].astype(F32)
        u_ref[pl.ds(pl.multiple_of(CONV_PAD + r * CONV_ROWS, CONV_ROWS), CONV_ROWS), :] = a / (1.0 + jnp.exp(-g))
        return carry

    lax.fori_loop(0, s // CONV_ROWS, glu, 0)

    win_rows = CONV_ROWS + CONV_PAD
    n_groups = CONV_CH // LANES

    def conv(r, carry):
        base = pl.multiple_of(r * CONV_ROWS, CONV_ROWS)
        accs = []
        for c in range(n_groups):
            lanes = slice(c * LANES, (c + 1) * LANES)
            win = u_ref[pl.ds(base, win_rows), lanes]
            acc = jnp.zeros((CONV_ROWS, LANES), F32) + cb_ref[:, lanes]
            for b in range(SUBLANES):
                shifted = pltpu.roll(win, b, 0) if b else win
                for a in range(-(-CONV_WIDTH // SUBLANES)):
                    d = SUBLANES * a + b
                    if d < CONV_WIDTH:
                        k0 = CONV_PAD - SUBLANES * a
                        tap = CONV_WIDTH - 1 - d
                        acc = acc + shifted[k0:k0 + CONV_ROWS, :] * w_ref[tap:tap + 1, lanes]
            accs.append(acc)
        mu = sum(jnp.sum(acc, axis=-1, keepdims=True) for acc in accs) / float(CONV_CH)
        cens = [acc - mu for acc in accs]
        var = sum(jnp.sum(cen * cen, axis=-1, keepdims=True) for cen in cens) / float(CONV_CH)
        inv = lax.rsqrt(var + 1e-5)
        for c, cen in enumerate(cens):
            lanes = slice(c * LANES, (c + 1) * LANES)
            y = cen * inv * lg_ref[:, lanes] + lb_ref[:, lanes]
            o_ref[0, pl.ds(base, CONV_ROWS), lanes] = (y / (1.0 + jnp.exp(-y))).astype(o_ref.dtype)
        return carry

    lax.fori_loop(0, s // CONV_ROWS, conv, 0, unroll=CONV_UNROLL)


def _conv(proj, w, cb, lg, lb, side=()):
    b, s, _ = proj.shape
    vec = lambda v: v.reshape(1, CONV_CH)
    side_in_specs, side_out_specs, side_shapes = [], [], []
    for wf, ls in side:
        k, n = wf.shape[1:]
        assert k % b == 0 and (k // b) % BF16_SUBLANES == 0, wf.shape
        side_in_specs.append(pl.BlockSpec((None, k // b, n), lambda i, ls=ls: (ls, i, 0)))
        side_out_specs.append(pl.BlockSpec((None, k // b, n), lambda i: (0, i, 0)))
        side_shapes.append(jax.ShapeDtypeStruct((1, k, n), BF16))
    out, *cast = pl.pallas_call(
        functools.partial(_conv_kernel, len(side)),
        out_shape=[jax.ShapeDtypeStruct((b, s, CONV_CH), BF16), *side_shapes],
        grid=(b,),
        in_specs=[
            pl.BlockSpec((1, s, CONV_CH), lambda i: (i, 0, COL_CONV_A * HEAD_DIM // CONV_CH)),
            pl.BlockSpec((1, s, CONV_CH), lambda i: (i, 0, COL_CONV_G * HEAD_DIM // CONV_CH)),
            pl.BlockSpec((CONV_WIDTH, CONV_CH), lambda i: (0, 0)),
            pl.BlockSpec((1, CONV_CH), lambda i: (0, 0)),
            pl.BlockSpec((1, CONV_CH), lambda i: (0, 0)),
            pl.BlockSpec((1, CONV_CH), lambda i: (0, 0)),
            *side_in_specs,
        ],
        out_specs=[pl.BlockSpec((1, s, CONV_CH), lambda i: (i, 0, 0)), *side_out_specs],
        scratch_shapes=[pltpu.VMEM((s + CONV_PAD, CONV_CH), F32)],
        compiler_params=_params(("parallel",)),
        name="conv",
    )(proj, proj, w, vec(cb), vec(lg), vec(lb), *(wf for wf, _ in side))
    return out, cast


def _tile_rows(j):
    return slice(j * ATT_TILE, (j + 1) * ATT_TILE)


def _lane_fold(op, x):
    out = x[:, :LANES]
    for c in range(1, x.shape[1] // LANES):
        out = op(out, x[:, c * LANES:(c + 1) * LANES])
    return out


def _half_rms(x, gain, eps=1e-6):
    half = x.shape[-1] // 2
    low = lax.broadcasted_iota(jnp.int32, x.shape, x.ndim - 1) < half
    sq = x * x
    ms = jnp.where(low, jnp.sum(jnp.where(low, sq, 0.0), axis=-1, keepdims=True),
                   jnp.sum(jnp.where(low, 0.0, sq), axis=-1, keepdims=True)) / half
    return x * lax.rsqrt(ms + eps) * gain


def _ones_columns(v_ref, vo_ref):
    vo_ref[:, :HEAD_DIM] = v_ref[...]
    vo_ref[:, HEAD_DIM:] = jnp.ones((v_ref.shape[0], HEAD_DIM), BF16)


def _logits_pass(i, q, k_ref, strip_ref, sel_add, s_ref):
    maps = q.shape[0] // ATT_TILE
    shape = (q.shape[0], ATT_TILE)
    mask = lax.broadcasted_iota(jnp.int32, shape, 0) % ATT_TILE >= lax.broadcasted_iota(jnp.int32, shape, 1)
    rep = lambda x: x if maps == 1 else jnp.concatenate([x] * maps, axis=0)
    m_acc = None
    for j in range(i + 1):
        s = _dot_nt(q, k_ref[_tile_rows(j), :])
        if j == i:
            s = jnp.where(mask, s + rep(strip_ref[0, :, ATT_TILE:]), NEG_INF)
        elif j == i - 1:
            s = s + rep(strip_ref[0, :, :ATT_TILE])
        if sel_add is not None and j < i:
            s = s + sel_add[:, j:j + 1]
        s_ref[:, _tile_rows(j)] = s
        t = _lane_fold(jnp.maximum, s)
        m_acc = t if m_acc is None else jnp.maximum(m_acc, t)
    return jnp.max(m_acc, axis=-1, keepdims=True)


def _softmax_pass(i, m, vo_ref, s_ref, p_ref):
    for j in range(i + 1):
        p_ref[:, _tile_rows(j)] = jnp.exp2(s_ref[:, _tile_rows(j)] - m).astype(BF16)
    n = (i + 1) * ATT_TILE
    out = _dot(p_ref[:, :n], vo_ref[:n, :])
    return out[:, :HEAD_DIM], out[:, HEAD_DIM:]


def _moba_kernel(q_ref, k_ref, v_ref, qg_ref, kg_ref, strip_ref, o_ref, kn_ref, vo_ref, kmean_ref, s_ref, p_ref):
    nb = k_ref.shape[1] // MOBA_BLOCK
    _ones_columns(v_ref.at[0], vo_ref)
    for j in range(nb):
        rows = _tile_rows(j)
        kn = _rms(k_ref[0, rows, :].astype(F32), kg_ref[...])
        kn_ref[rows, :] = kn.astype(BF16)
        kmean_ref[j:j + 1, :] = jnp.sum(kn, axis=0, keepdims=True) / float(MOBA_BLOCK)

    eye = (lax.broadcasted_iota(jnp.int32, (MOBA_BLOCK, MOBA_BLOCK), 0)
           == lax.broadcasted_iota(jnp.int32, (MOBA_BLOCK, MOBA_BLOCK), 1)).astype(F32).astype(BF16)
    n_sel = min(MOBA_TOPK, nb)

    def logits(i):
        qn = _rms(q_ref[0, _tile_rows(i), :].astype(F32), qg_ref[...])
        sel_add = None
        if i > n_sel:
            gate = _dot_nt(kmean_ref[...], qn, precision=lax.Precision.HIGHEST)
            blk = lax.broadcasted_iota(jnp.int32, gate.shape, 0)
            gate = jnp.where(blk < i, gate, TAKEN)
            sel = jnp.zeros(gate.shape, F32)
            for _ in range(n_sel):
                top = jnp.max(gate, axis=0, keepdims=True)
                first = jnp.min(jnp.where(gate == top, blk, nb), axis=0, keepdims=True)
                pick = blk == first
                sel = jnp.where(pick, 1.0, sel)
                gate = jnp.where(pick, TAKEN, gate)
            sel_t = _dot_nt(eye, sel.astype(BF16))
            sel_add = (sel_t - 1.0) * -NEG_INF
        q = (qn * (HEAD_DIM ** -0.5 * LOG2E)).astype(BF16)
        return _logits_pass(i, q, kn_ref, strip_ref, sel_add, s_ref.at[i % 2])

    m_next = logits(0)
    for i in range(nb):
        m = m_next
        if i + 1 < nb:
            m_next = logits(i + 1)
        acc, l = _softmax_pass(i, m, vo_ref, s_ref.at[i % 2], p_ref.at[i % 2])
        o_ref[0, _tile_rows(i), :] = (acc / l).astype(o_ref.dtype)


def _moba(proj, strips, q_gain, k_gain):
    b, s, _ = proj.shape
    nb = s // MOBA_BLOCK
    return pl.pallas_call(
        _moba_kernel,
        out_shape=jax.ShapeDtypeStruct((b, s, MOBA_HEADS * HEAD_DIM), BF16),
        grid=(b, MOBA_HEADS),
        in_specs=[
            pl.BlockSpec((1, s, HEAD_DIM), lambda bi, h: (bi, 0, COL_MOBA_Q + h)),
            pl.BlockSpec((1, s, HEAD_DIM), lambda bi, h: (bi, 0, COL_MOBA_K + h)),
            pl.BlockSpec((1, s, HEAD_DIM), lambda bi, h: (bi, 0, COL_MOBA_V + h)),
            pl.BlockSpec((1, HEAD_DIM), lambda bi, h: (0, 0)),
            pl.BlockSpec((1, HEAD_DIM), lambda bi, h: (0, 0)),
            pl.BlockSpec((1, ATT_TILE, 2 * ATT_TILE), lambda bi, h: (h, 0, 0)),
        ],
        out_specs=pl.BlockSpec((1, s, HEAD_DIM), lambda bi, h: (bi, 0, h)),
        scratch_shapes=[
            pltpu.VMEM((s, HEAD_DIM), BF16),
            pltpu.VMEM((s, 2 * HEAD_DIM), BF16),
            pltpu.VMEM((nb, HEAD_DIM), F32),
            pltpu.VMEM((2, ATT_TILE, s), F32),
            pltpu.VMEM((2, ATT_TILE, s), BF16),
        ],
        compiler_params=_params(("parallel", "parallel")),
        name="moba",
    )(proj, proj, proj, q_gain.reshape(1, HEAD_DIM), k_gain.reshape(1, HEAD_DIM), strips)


DIFF_MAPS = 2


def _diff_kernel(lam_init, q_ref, k_ref, v_ref, qg_ref, kg_ref, lam_ref, sg_ref, strip_ref, o_ref,
                 kn_ref, vo_ref, s_ref, p_ref):
    nt = k_ref.shape[1] // ATT_TILE
    assert DIFF_MAPS * DIFF_QK_DIM == HEAD_DIM
    _ones_columns(v_ref.at[0], vo_ref)
    for j in range(nt):
        rows = _tile_rows(j)
        kn_ref[rows, :] = _half_rms(k_ref[0, rows, :].astype(F32), kg_ref[...]).astype(BF16)

    lp = lam_ref[...]
    lam = (jnp.exp(jnp.sum(lp[0:1] * lp[1:2], axis=-1, keepdims=True))
           - jnp.exp(jnp.sum(lp[2:3] * lp[3:4], axis=-1, keepdims=True)) + lam_init)
    scale = DIFF_QK_DIM ** -0.5 * LOG2E
    lane_map = lax.broadcasted_iota(jnp.int32, (ATT_TILE, HEAD_DIM), 1) // DIFF_QK_DIM

    def logits(i):
        qn = _half_rms(q_ref[0, _tile_rows(i), :].astype(F32), qg_ref[...]) * scale
        q = jnp.concatenate([jnp.where(lane_map == c, qn, 0.0) for c in range(DIFF_MAPS)], axis=0).astype(BF16)
        return _logits_pass(i, q, kn_ref, strip_ref, None, s_ref.at[i % 2])

    m_next = logits(0)
    for i in range(nt):
        m = m_next
        if i + 1 < nt:
            m_next = logits(i + 1)
        acc, l = _softmax_pass(i, m, vo_ref, s_ref.at[i % 2], p_ref.at[i % 2])
        a = acc / l
        o = a[:ATT_TILE] - lam * a[ATT_TILE:]
        o_ref[0, _tile_rows(i), :] = (_rms(o, sg_ref[...]) * (1.0 - lam_init)).astype(o_ref.dtype)


def _diff(proj, strips, q_gain, k_gain, lam_params, subln, lam_init):
    b, s, _ = proj.shape
    both = lambda g: jnp.tile(g, DIFF_MAPS).reshape(1, HEAD_DIM)
    return pl.pallas_call(
        functools.partial(_diff_kernel, lam_init),
        out_shape=jax.ShapeDtypeStruct((b, s, DIFF_HEADS * HEAD_DIM), BF16),
        grid=(b, DIFF_HEADS),
        in_specs=[
            pl.BlockSpec((1, s, HEAD_DIM), lambda bi, h: (bi, 0, COL_DIFF_Q + h)),
            pl.BlockSpec((1, s, HEAD_DIM), lambda bi, h: (bi, 0, COL_DIFF_K + h)),
            pl.BlockSpec((1, s, HEAD_DIM), lambda bi, h: (bi, 0, COL_DIFF_V + h)),
            pl.BlockSpec((1, HEAD_DIM), lambda bi, h: (0, 0)),
            pl.BlockSpec((1, HEAD_DIM), lambda bi, h: (0, 0)),
            pl.BlockSpec((4, DIFF_QK_DIM), lambda bi, h: (0, 0)),
            pl.BlockSpec((1, HEAD_DIM), lambda bi, h: (0, 0)),
            pl.BlockSpec((1, ATT_TILE, 2 * ATT_TILE), lambda bi, h: (MOBA_HEADS + h, 0, 0)),
        ],
        out_specs=pl.BlockSpec((1, s, HEAD_DIM), lambda bi, h: (bi, 0, h)),
        scratch_shapes=[
            pltpu.VMEM((s, HEAD_DIM), BF16),
            pltpu.VMEM((s, 2 * HEAD_DIM), BF16),
            pltpu.VMEM((2, DIFF_MAPS * ATT_TILE, s), F32),
            pltpu.VMEM((2, DIFF_MAPS * ATT_TILE, s), BF16),
        ],
        compiler_params=_params(("parallel", "parallel")),
        name="diff",
    )(proj, proj, proj, both(q_gain), both(k_gain), lam_params, subln.reshape(1, HEAD_DIM), strips)


def _memkv_kernel(m_ref, g_ref, w_ref, kg_ref, k_ref, v_ref):
    hm = _rms(m_ref[0], g_ref[...]).astype(BF16)
    kv = _dot(hm, w_ref[...])
    width = MEM_HEADS * HEAD_DIM
    for h in range(MEM_HEADS):
        cols = slice(h * HEAD_DIM, (h + 1) * HEAD_DIM)
        k_ref[0, :, cols] = _rms(kv[:, cols], kg_ref[...]).astype(BF16)
    v_ref[0] = kv[:, width:].astype(BF16)


def _memkv(mem, gain, wkv, k_gain, l):
    b, m, d = mem.shape
    width = MEM_HEADS * HEAD_DIM
    out = jax.ShapeDtypeStruct((b, m, width), BF16)
    return pl.pallas_call(
        _memkv_kernel,
        out_shape=(out, out),
        grid=(b,),
        in_specs=[
            pl.BlockSpec((1, m, d), lambda i: (i, 0, 0)),
            pl.BlockSpec((1, d), lambda i: (0, 0)),
            pl.BlockSpec((None, d, 2 * width), lambda i: (l, 0, 0)),
            pl.BlockSpec((1, HEAD_DIM), lambda i: (0, 0)),
        ],
        out_specs=(pl.BlockSpec((1, m, width), lambda i: (i, 0, 0)), pl.BlockSpec((1, m, width), lambda i: (i, 0, 0))),
        compiler_params=_params(("parallel",)),
        name="memkv",
    )(mem, gain.reshape(1, d), wkv, k_gain.reshape(1, HEAD_DIM))


def _mixout_kernel(x_ref, yc_ref, ym_ref, yd_ref, w_ref, g_ref, wq_ref, qg_ref, k_ref, v_ref, wo_ref, o_ref, ob_ref):
    c0 = CONV_CH
    c1 = c0 + MOBA_HEADS * HEAD_DIM
    x1 = (x_ref[0] + _dot(yc_ref[0], w_ref[0:c0, :]) + _dot(ym_ref[0], w_ref[c0:c1, :])
          + _dot(yd_ref[0], w_ref[c1:, :]))
    q = _dot(_rms(x1, g_ref[...]).astype(BF16), wq_ref[...])
    for h in range(MEM_HEADS):
        cols = slice(h * HEAD_DIM, (h + 1) * HEAD_DIM)
        qh = (_rms(q[:, cols], qg_ref[...]) * (HEAD_DIM ** -0.5 * LOG2E)).astype(BF16)
        s = _dot_nt(qh, k_ref[0, :, cols])
        p = jnp.exp2(s - jnp.max(s, axis=-1, keepdims=True))
        oh = _dot(p.astype(BF16), v_ref[0, :, cols]) / jnp.sum(p, axis=-1, keepdims=True)
        ob_ref[:, cols] = oh.astype(BF16)
    o_ref[0] = x1 + _dot(ob_ref[...], wo_ref[...])


def _mixout(x, yc, ym, yd, w, gain, wq, q_gain, kn, v, wo, l, tm=512):
    b, s, d = x.shape
    m = kn.shape[1]
    width = MEM_HEADS * HEAD_DIM
    once = pl.Buffered(1)
    rows = lambda a: pl.BlockSpec((1, tm, a.shape[2]), lambda bi, i: (bi, i, 0))
    return pl.pallas_call(
        _mixout_kernel,
        out_shape=jax.ShapeDtypeStruct((b, s, d), F32),
        grid=(b, s // tm),
        in_specs=[
            rows(x), rows(yc), rows(ym), rows(yd),
            pl.BlockSpec((None, w.shape[1], d), lambda bi, i: (l, 0, 0), pipeline_mode=once),
            pl.BlockSpec((1, d), lambda bi, i: (0, 0)),
            pl.BlockSpec((None, d, width), lambda bi, i: (l, 0, 0), pipeline_mode=once),
            pl.BlockSpec((1, HEAD_DIM), lambda bi, i: (0, 0)),
            pl.BlockSpec((1, m, width), lambda bi, i: (bi, 0, 0)),
            pl.BlockSpec((1, m, width), lambda bi, i: (bi, 0, 0)),
            pl.BlockSpec((None, width, d), lambda bi, i: (l, 0, 0), pipeline_mode=once),
        ],
        out_specs=rows(x),
        scratch_shapes=[pltpu.VMEM((tm, width), BF16)],
        compiler_params=_params(("parallel", "parallel")),
        name="mixout",
    )(x, yc, ym, yd, w, gain.reshape(1, d), wq, q_gain.reshape(1, HEAD_DIM), kn, v, wo)


def kernel(x, mem, rel_bias, ffn1_norm, ffn1_w_gate, ffn1_w_up, ffn1_w_down, mix_norm, w_in, conv_w, conv_b, conv_ln_g, conv_ln_b, moba_q_norm, moba_k_norm, diff_q_norm, diff_k_norm, diff_lambda, diff_subln, w_out, mem_norm_x, mem_norm_m, mem_wq, mem_wkv, mem_q_norm, mem_k_norm, mem_wo, ffn2_norm, ffn2_w_gate, ffn2_w_up, ffn2_w_down):
    b, s, d = x.shape
    depth = w_in.shape[0]
    t = b * s
    tab_flat = rel_bias.T.reshape(-1)
    strips = _bias_strips(tab_flat)
    ffn1_f32 = (ffn1_w_gate, ffn1_w_up, ffn1_w_down)
    ffn2_f32 = (ffn2_w_gate, ffn2_w_up, ffn2_w_down)
    ffn_w, ffn_tf = ffn1_f32, FFN_BLOCK_F32
    xt = x.reshape(t, d)
    for l in range(depth):
        xt, (*ffn_w, w_in_l) = _ffn(xt, ffn1_norm[l], *ffn_w, 0, side=[(w, l) for w in (*ffn2_f32, w_in)], tf=ffn_tf)
        ffn_tf = FFN_BLOCK_BF16
        proj = _proj(xt, mix_norm[l], w_in_l, 0).reshape(b, s, PROJ_WIDTH)
        y_conv, (w_out_l, wq_l, wkv_l, wo_l) = _conv(proj, conv_w[l], conv_b[l], conv_ln_g[l], conv_ln_b[l],
                                                    side=[(w, l) for w in (w_out, mem_wq, mem_wkv, mem_wo)])
        y_moba = _moba(proj, strips, moba_q_norm[l], moba_k_norm[l])
        lam_init = 0.8 - 0.6 * math.exp(-0.3 * l)
        y_diff = _diff(proj, strips, diff_q_norm[l], diff_k_norm[l], diff_lambda[l], diff_subln[l], lam_init)
        kn, v = _memkv(mem, mem_norm_m[l], wkv_l, mem_k_norm[l], 0)
        xt = _mixout(xt.reshape(b, s, d), y_conv, y_moba, y_diff, w_out_l, mem_norm_x[l], wq_l, mem_q_norm[l], kn, v,
                     wo_l, 0).reshape(t, d)
        nxt = [(w, l + 1) for w in ffn1_f32] if l + 1 < depth else []
        xt, ffn_w = _ffn(xt, ffn2_norm[l], *ffn_w, 0, side=nxt, tf=ffn_tf)
    return xt.reshape(b, s, d)
```

```python
import functools
import math

import jax
import jax.numpy as jnp
from jax import lax
from jax.experimental import pallas as pl
from jax.experimental.pallas import tpu as pltpu

F32 = jnp.float32
BF16 = jnp.bfloat16

CONV_CH = 512
CONV_WIDTH = 31
CONV_PAD = 32
HEAD_DIM = 128
LANES = 128
BF16_SUBLANES = 16
MOBA_HEADS = 6
MOBA_BLOCK = 256
MOBA_TOPK = 3
DIFF_HEADS = 6
DIFF_QK_DIM = 64
MEM_HEADS = 4
N_BUCKETS = 32
MAX_DISTANCE = 128
N_BIAS_HEADS = MOBA_HEADS + DIFF_HEADS
NEG_INF = -1e30
TAKEN = -3e38
PROJ_WIDTH = 5632
COL_CONV_A = 0
COL_CONV_G = CONV_CH // HEAD_DIM
COL_MOBA_Q = 2 * CONV_CH // HEAD_DIM
COL_MOBA_K = COL_MOBA_Q + MOBA_HEADS
COL_MOBA_V = COL_MOBA_K + MOBA_HEADS
COL_DIFF_Q = COL_MOBA_V + MOBA_HEADS
COL_DIFF_K = COL_DIFF_Q + DIFF_HEADS
COL_DIFF_V = COL_DIFF_K + DIFF_HEADS

ATT_TILE = 256
assert ATT_TILE == MOBA_BLOCK and ATT_TILE >= MAX_DISTANCE
LOG2E = 1.4426950408889634
VMEM_LIMIT = 56 * 1024 * 1024


def _params(semantics, vmem=VMEM_LIMIT):
    return pltpu.CompilerParams(dimension_semantics=semantics, vmem_limit_bytes=vmem)


def _rms(x, gain, eps=1e-6):
    return x * lax.rsqrt(jnp.mean(x * x, axis=-1, keepdims=True) + eps) * gain


def _dot(a, b):
    return jnp.dot(a, b, preferred_element_type=F32)


def _dot_nt(a, b, precision=None):
    return lax.dot_general(a, b, (((1,), (1,)), ((), ())), precision=precision, preferred_element_type=F32)


def _ffn_kernel(n_side, x_hbm, g_ref, wg_ref, wu_ref, wd_ref, *refs):
    side_in, o_ref, side_out = refs[:n_side], refs[n_side], refs[n_side + 1:2 * n_side + 1]
    h_ref, x_ref, x_sem = refs[2 * n_side + 1:]
    i, k = pl.program_id(0), pl.program_id(1)
    tm = x_ref.shape[0]

    def x_copy(block):
        return pltpu.make_async_copy(x_hbm.at[pl.ds(pl.multiple_of(block * tm, tm), tm), :], x_ref, x_sem)

    @pl.when(jnp.logical_and(i == 0, k == 0))
    def _():
        x_copy(0).start()

    @pl.when(k == 0)
    def _():
        x_copy(i).wait()
        x = x_ref[...]
        h_ref[...] = _rms(x, g_ref[...]).astype(BF16)
        o_ref[...] = x

    @pl.when(jnp.logical_and(k == 1, i + 1 < pl.num_programs(0)))
    def _():
        x_copy(i + 1).start()

    h = h_ref[...]
    gate = _dot(h, wg_ref[...].astype(BF16))
    up = _dot(h, wu_ref[...].astype(BF16))
    act = (0.5 * gate / (1.0 + jnp.exp(-gate)) * up).astype(BF16)
    o_ref[...] += _dot(act, wd_ref[...].astype(BF16))
    for w_ref, wb_ref in zip(side_in, side_out):
        wb_ref[...] = w_ref[...].astype(BF16)


def _side_block(shape, gi, gk):
    k, n = shape
    if k % gi == 0 and n % gk == 0 and (k // gi) % BF16_SUBLANES == 0 and (n // gk) % LANES == 0:
        return (k // gi, n // gk), (lambda i, kk: (i, kk))
    assert k % gk == 0 and n % gi == 0 and (k // gk) % BF16_SUBLANES == 0 and (n // gi) % LANES == 0, shape
    return (k // gk, n // gi), (lambda i, kk: (kk, i))


FFN_BLOCK_BF16 = 512
FFN_BLOCK_F32 = 256


def _ffn(x, gain, wg, wu, wd, l, side=(), tm=1024, tf=FFN_BLOCK_BF16):
    t, d = x.shape
    f = wg.shape[2]
    gi, gk = t // tm, f // tf
    side_in_specs, side_out_specs, side_shapes = [], [], []
    for w, ls in side:
        blk, order = _side_block(w.shape[1:], gi, gk)
        side_in_specs.append(pl.BlockSpec((None, *blk), lambda i, k, ls=ls, order=order: (ls, *order(i, k))))
        side_out_specs.append(pl.BlockSpec((None, *blk), lambda i, k, order=order: (0, *order(i, k))))
        side_shapes.append(jax.ShapeDtypeStruct((1, *w.shape[1:]), BF16))
    out, *cast = pl.pallas_call(
        functools.partial(_ffn_kernel, len(side)),
        out_shape=[jax.ShapeDtypeStruct((t, d), F32), *side_shapes],
        grid=(gi, gk),
        in_specs=[
            pl.BlockSpec(memory_space=pl.ANY),
            pl.BlockSpec((1, d), lambda i, k: (0, 0)),
            pl.BlockSpec((None, d, tf), lambda i, k: (l, 0, k)),
            pl.BlockSpec((None, d, tf), lambda i, k: (l, 0, k)),
            pl.BlockSpec((None, tf, d), lambda i, k: (l, k, 0)),
            *side_in_specs,
        ],
        out_specs=[pl.BlockSpec((tm, d), lambda i, k: (i, 0)), *side_out_specs],
        scratch_shapes=[pltpu.VMEM((tm, d), BF16), pltpu.VMEM((tm, d), F32), pltpu.SemaphoreType.DMA(())],
        compiler_params=_params(("arbitrary", "arbitrary")),
        name="ffn",
    )(x, gain.reshape(1, d), wg, wu, wd, *(w for w, _ in side))
    return out, cast


def _proj_kernel(x_ref, g_ref, w_ref, o_ref, h_ref):
    @pl.when(pl.program_id(1) == 0)
    def _():
        h_ref[...] = _rms(x_ref[...], g_ref[...]).astype(BF16)

    o_ref[...] = _dot(h_ref[...], w_ref[...]).astype(o_ref.dtype)


def _proj(x, gain, w, l, tm=2048, tn=512):
    t, d = x.shape
    n = w.shape[2]
    return pl.pallas_call(
        _proj_kernel,
        out_shape=jax.ShapeDtypeStruct((t, n), BF16),
        grid=(t // tm, n // tn),
        in_specs=[
            pl.BlockSpec((tm, d), lambda i, j: (i, 0)),
            pl.BlockSpec((1, d), lambda i, j: (0, 0)),
            pl.BlockSpec((None, d, tn), lambda i, j: (l, 0, j)),
        ],
        out_specs=pl.BlockSpec((tm, tn), lambda i, j: (i, j)),
        scratch_shapes=[pltpu.VMEM((tm, d), BF16)],
        compiler_params=_params(("parallel", "arbitrary")),
        name="proj",
    )(x, gain.reshape(1, d), w)


def _bias_kernel(tab_ref, o_ref):
    h = pl.program_id(0)
    far = tab_ref[h * N_BUCKETS + N_BUCKETS - 1]
    shape = (ATT_TILE, 2 * ATT_TILE)
    dist = ATT_TILE + lax.broadcasted_iota(jnp.int32, shape, 0) - lax.broadcasted_iota(jnp.int32, shape, 1)
    dist = jnp.maximum(dist, 0)
    max_exact = N_BUCKETS // 2
    log_ratio = jnp.log(jnp.maximum(dist, 1).astype(F32) / max_exact) / math.log(MAX_DISTANCE / max_exact)
    large = jnp.minimum(max_exact + (log_ratio * (N_BUCKETS - max_exact)).astype(jnp.int32), N_BUCKETS - 1)
    bucket = jnp.where(dist < max_exact, dist, large)
    out = jnp.zeros(shape, F32)
    for b in range(N_BUCKETS):
        out = jnp.where(bucket == b, (tab_ref[h * N_BUCKETS + b] - far) * LOG2E, out)
    o_ref[0] = out


def _bias_strips(tab_flat):
    return pl.pallas_call(
        _bias_kernel,
        out_shape=jax.ShapeDtypeStruct((N_BIAS_HEADS, ATT_TILE, 2 * ATT_TILE), F32),
        grid=(N_BIAS_HEADS,),
        in_specs=[pl.BlockSpec(memory_space=pltpu.SMEM)],
        out_specs=pl.BlockSpec((1, ATT_TILE, 2 * ATT_TILE), lambda h: (h, 0, 0)),
        compiler_params=_params(("arbitrary",)),
        name="bias_strips",
    )(tab_flat)


CONV_ROWS = 32
CONV_UNROLL = 4
SUBLANES = 8
assert CONV_PAD % SUBLANES == 0 and CONV_PAD >= SUBLANES * -(-CONV_WIDTH // SUBLANES)


def _conv_kernel(n_side, a_ref, g_ref, w_ref, cb_ref, lg_ref, lb_ref, *refs):
    side_in, o_ref, side_out, u_ref = refs[:n_side], refs[n_side], refs[n_side + 1:2 * n_side + 1], refs[-1]
    for wf_ref, wb_ref in zip(side_in, side_out):
        wb_ref[...] = wf_ref[...].astype(BF16)
    s = a_ref.shape[1]
    u_ref[0:CONV_PAD, :] = jnp.zeros((CONV_PAD, CONV_CH), F32)

    def glu(r, carry):
        rows = pl.ds(pl.multiple_of(r * CONV_ROWS, CONV_ROWS), CONV_ROWS)
        a = a_ref[0, rows, :].astype(F32)
        g = g_ref[0, rows, :].astype(F32)
        u_ref[pl.ds(pl.multiple_of(CONV_PAD + r * CONV_ROWS, CONV_ROWS), CONV_ROWS), :] = a / (1.0 + jnp.exp(-g))
        return carry

    lax.fori_loop(0, s // CONV_ROWS, glu, 0)

    win_rows = CONV_ROWS + CONV_PAD
    n_groups = CONV_CH // LANES

    def conv(r, carry):
        base = pl.multiple_of(r * CONV_ROWS, CONV_ROWS)
        accs = []
        for c in range(n_groups):
            lanes = slice(c * LANES, (c + 1) * LANES)
            win = u_ref[pl.ds(base, win_rows), lanes]
            acc = jnp.zeros((CONV_ROWS, LANES), F32) + cb_ref[:, lanes]
            for b in range(SUBLANES):
                shifted = pltpu.roll(win, b, 0) if b else win
                for a in range(-(-CONV_WIDTH // SUBLANES)):
                    d = SUBLANES * a + b
                    if d < CONV_WIDTH:
                        k0 = CONV_PAD - SUBLANES * a
                        tap = CONV_WIDTH - 1 - d
                        acc = acc + shifted[k0:k0 + CONV_ROWS, :] * w_ref[tap:tap + 1, lanes]
            accs.append(acc)
        mu = sum(jnp.sum(acc, axis=-1, keepdims=True) for acc in accs) / float(CONV_CH)
        cens = [acc - mu for acc in accs]
        var = sum(jnp.sum(cen * cen, axis=-1, keepdims=True) for cen in cens) / float(CONV_CH)
        inv = lax.rsqrt(var + 1e-5)
        for c, cen in enumerate(cens):
            lanes = slice(c * LANES, (c + 1) * LANES)
            y = cen * inv * lg_ref[:, lanes] + lb_ref[:, lanes]
            o_ref[0, pl.ds(base, CONV_ROWS), lanes] = (y / (1.0 + jnp.exp(-y))).astype(o_ref.dtype)
        return carry

    lax.fori_loop(0, s // CONV_ROWS, conv, 0, unroll=CONV_UNROLL)


def _conv(proj, w, cb, lg, lb, side=()):
    b, s, _ = proj.shape
    vec = lambda v: v.reshape(1, CONV_CH)
    side_in_specs, side_out_specs, side_shapes = [], [], []
    for wf, ls in side:
        k, n = wf.shape[1:]
        assert k % b == 0 and (k // b) % BF16_SUBLANES == 0, wf.shape
        side_in_specs.append(pl.BlockSpec((None, k // b, n), lambda i, ls=ls: (ls, i, 0)))
        side_out_specs.append(pl.BlockSpec((None, k // b, n), lambda i: (0, i, 0)))
        side_shapes.append(jax.ShapeDtypeStruct((1, k, n), BF16))
    out, *cast = pl.pallas_call(
        functools.partial(_conv_kernel, len(side)),
        out_shape=[jax.ShapeDtypeStruct((b, s, CONV_CH), BF16), *side_shapes],
        grid=(b,),
        in_specs=[
            pl.BlockSpec((1, s, CONV_CH), lambda i: (i, 0, COL_CONV_A * HEAD_DIM // CONV_CH)),
            pl.BlockSpec((1, s, CONV_CH), lambda i: (i, 0, COL_CONV_G * HEAD_DIM // CONV_CH)),
            pl.BlockSpec((CONV_WIDTH, CONV_CH), lambda i: (0, 0)),
            pl.BlockSpec((1, CONV_CH), lambda i: (0, 0)),
            pl.BlockSpec((1, CONV_CH), lambda i: (0, 0)),
            pl.BlockSpec((1, CONV_CH), lambda i: (0, 0)),
            *side_in_specs,
        ],
        out_specs=[pl.BlockSpec((1, s, CONV_CH), lambda i: (i, 0, 0)), *side_out_specs],
        scratch_shapes=[pltpu.VMEM((s + CONV_PAD, CONV_CH), F32)],
        compiler_params=_params(("parallel",)),
        name="conv",
    )(proj, proj, w, vec(cb), vec(lg), vec(lb), *(wf for wf, _ in side))
    return out, cast


def _tile_rows(j):
    return slice(j * ATT_TILE, (j + 1) * ATT_TILE)


def _lane_fold(op, x):
    out = x[:, :LANES]
    for c in range(1, x.shape[1] // LANES):
        out = op(out, x[:, c * LANES:(c + 1) * LANES])
    return out


def _half_rms(x, gain, eps=1e-6):
    half = x.shape[-1] // 2
    low = lax.broadcasted_iota(jnp.int32, x.shape, x.ndim - 1) < half
    sq = x * x
    ms = jnp.where(low, jnp.sum(jnp.where(low, sq, 0.0), axis=-1, keepdims=True),
                   jnp.sum(jnp.where(low, 0.0, sq), axis=-1, keepdims=True)) / half
    return x * lax.rsqrt(ms + eps) * gain


def _logits_pass(i, q, k_ref, strip_ref, sel_add, s_ref):
    maps = q.shape[0] // ATT_TILE
    shape = (q.shape[0], ATT_TILE)
    mask = lax.broadcasted_iota(jnp.int32, shape, 0) % ATT_TILE >= lax.broadcasted_iota(jnp.int32, shape, 1)
    rep = lambda x: x if maps == 1 else jnp.concatenate([x] * maps, axis=0)
    m_acc = None
    for j in range(i + 1):
        s = _dot_nt(q, k_ref[_tile_rows(j), :])
        if j == i:
            s = jnp.where(mask, s + rep(strip_ref[:, ATT_TILE:]), NEG_INF)
        elif j == i - 1:
            s = s + rep(strip_ref[:, :ATT_TILE])
        if sel_add is not None and j < i:
            s = s + sel_add[:, j:j + 1]
        s_ref[:, _tile_rows(j)] = s
        t = _lane_fold(jnp.maximum, s)
        m_acc = t if m_acc is None else jnp.maximum(m_acc, t)
    return jnp.max(m_acc, axis=-1, keepdims=True)


def _softmax_pass(i, m, vo_ref, s_ref, p_ref):
    for j in range(i + 1):
        p_ref[:, _tile_rows(j)] = jnp.exp2(s_ref[:, _tile_rows(j)] - m).astype(BF16)
    n = (i + 1) * ATT_TILE
    out = _dot(p_ref[:, :n], vo_ref[:n, :])
    return out[:, :HEAD_DIM], out[:, HEAD_DIM:]


ATT_HEADS_PER_STEP = 2


def _head_lanes(g):
    return slice(g * HEAD_DIM, (g + 1) * HEAD_DIM)


def _moba_kernel(q_ref, k_ref, v_ref, qg_ref, kg_ref, strip_ref, o_ref, kn_ref, vo_ref, kmean_ref, s_ref, p_ref):
    nb = k_ref.shape[1] // MOBA_BLOCK
    heads = k_ref.shape[2] // HEAD_DIM
    for g in range(heads):
        vo_ref[g, :, :HEAD_DIM] = v_ref[0, :, _head_lanes(g)]
        vo_ref[g, :, HEAD_DIM:] = jnp.ones((v_ref.shape[1], HEAD_DIM), BF16)
        for j in range(nb):
            rows = _tile_rows(j)
            kn = _rms(k_ref[0, rows, _head_lanes(g)].astype(F32), kg_ref[...])
            kn_ref[g, rows, :] = kn.astype(BF16)
            kmean_ref[g, j:j + 1, :] = jnp.sum(kn, axis=0, keepdims=True) / float(MOBA_BLOCK)

    eye = (lax.broadcasted_iota(jnp.int32, (MOBA_BLOCK, MOBA_BLOCK), 0)
           == lax.broadcasted_iota(jnp.int32, (MOBA_BLOCK, MOBA_BLOCK), 1)).astype(F32).astype(BF16)
    n_sel = min(MOBA_TOPK, nb)

    def logits(pos, g, i):
        qn = _rms(q_ref[0, _tile_rows(i), _head_lanes(g)].astype(F32), qg_ref[...])
        sel_add = None
        if i > n_sel:
            gate = _dot_nt(kmean_ref[g], qn, precision=lax.Precision.HIGHEST)
            blk = lax.broadcasted_iota(jnp.int32, gate.shape, 0)
            gate = jnp.where(blk < i, gate, TAKEN)
            sel = jnp.zeros(gate.shape, F32)
            for _ in range(n_sel):
                top = jnp.max(gate, axis=0, keepdims=True)
                first = jnp.min(jnp.where(gate == top, blk, nb), axis=0, keepdims=True)
                pick = blk == first
                sel = jnp.where(pick, 1.0, sel)
                gate = jnp.where(pick, TAKEN, gate)
            sel_t = _dot_nt(eye, sel.astype(BF16))
            sel_add = (sel_t - 1.0) * -NEG_INF
        q = (qn * (HEAD_DIM ** -0.5 * LOG2E)).astype(BF16)
        return _logits_pass(i, q, kn_ref.at[g], strip_ref.at[g], sel_add, s_ref.at[pos % 2])

    jobs = [(g, i) for g in range(heads) for i in range(nb)]
    m_next = logits(0, *jobs[0])
    for pos, (g, i) in enumerate(jobs):
        m = m_next
        if pos + 1 < len(jobs):
            m_next = logits(pos + 1, *jobs[pos + 1])
        acc, l = _softmax_pass(i, m, vo_ref.at[g], s_ref.at[pos % 2], p_ref.at[pos % 2])
        o_ref[0, _tile_rows(i), _head_lanes(g)] = (acc / l).astype(o_ref.dtype)


def _moba(proj, strips, q_gain, k_gain):
    b, s, _ = proj.shape
    nb = s // MOBA_BLOCK
    g = ATT_HEADS_PER_STEP
    assert MOBA_HEADS % g == 0 and COL_MOBA_Q % g == 0 and COL_MOBA_K % g == 0 and COL_MOBA_V % g == 0
    cols = lambda first: pl.BlockSpec((1, s, g * HEAD_DIM), lambda bi, h: (bi, 0, first // g + h))
    return pl.pallas_call(
        _moba_kernel,
        out_shape=jax.ShapeDtypeStruct((b, s, MOBA_HEADS * HEAD_DIM), BF16),
        grid=(b, MOBA_HEADS // g),
        in_specs=[
            cols(COL_MOBA_Q),
            cols(COL_MOBA_K),
            cols(COL_MOBA_V),
            pl.BlockSpec((1, HEAD_DIM), lambda bi, h: (0, 0)),
            pl.BlockSpec((1, HEAD_DIM), lambda bi, h: (0, 0)),
            pl.BlockSpec((g, ATT_TILE, 2 * ATT_TILE), lambda bi, h: (h, 0, 0)),
        ],
        out_specs=cols(0),
        scratch_shapes=[
            pltpu.VMEM((g, s, HEAD_DIM), BF16),
            pltpu.VMEM((g, s, 2 * HEAD_DIM), BF16),
            pltpu.VMEM((g, nb, HEAD_DIM), F32),
            pltpu.VMEM((2, ATT_TILE, s), F32),
            pltpu.VMEM((2, ATT_TILE, s), BF16),
        ],
        compiler_params=_params(("parallel", "parallel")),
        name="moba",
    )(proj, proj, proj, q_gain.reshape(1, HEAD_DIM), k_gain.reshape(1, HEAD_DIM), strips)


DIFF_MAPS = 2


def _diff_kernel(lam_init, q_ref, k_ref, v_ref, qg_ref, kg_ref, lam_ref, sg_ref, strip_ref, o_ref,
                 kn_ref, vo_ref, s_ref, p_ref):
    nt = k_ref.shape[1] // ATT_TILE
    heads = k_ref.shape[2] // HEAD_DIM
    assert DIFF_MAPS * DIFF_QK_DIM == HEAD_DIM
    for g in range(heads):
        vo_ref[g, :, :HEAD_DIM] = v_ref[0, :, _head_lanes(g)]
        vo_ref[g, :, HEAD_DIM:] = jnp.ones((v_ref.shape[1], HEAD_DIM), BF16)
        for j in range(nt):
            rows = _tile_rows(j)
            kn_ref[g, rows, :] = _half_rms(k_ref[0, rows, _head_lanes(g)].astype(F32), kg_ref[...]).astype(BF16)

    lp = lam_ref[...]
    lam = (jnp.exp(jnp.sum(lp[0:1] * lp[1:2], axis=-1, keepdims=True))
           - jnp.exp(jnp.sum(lp[2:3] * lp[3:4], axis=-1, keepdims=True)) + lam_init)
    scale = DIFF_QK_DIM ** -0.5 * LOG2E
    lane_map = lax.broadcasted_iota(jnp.int32, (ATT_TILE, HEAD_DIM), 1) // DIFF_QK_DIM

    def logits(pos, g, i):
        qn = _half_rms(q_ref[0, _tile_rows(i), _head_lanes(g)].astype(F32), qg_ref[...]) * scale
        q = jnp.concatenate([jnp.where(lane_map == c, qn, 0.0) for c in range(DIFF_MAPS)], axis=0).astype(BF16)
        return _logits_pass(i, q, kn_ref.at[g], strip_ref.at[g], None, s_ref.at[pos % 2])

    jobs = [(g, i) for g in range(heads) for i in range(nt)]
    m_next = logits(0, *jobs[0])
    for pos, (g, i) in enumerate(jobs):
        m = m_next
        if pos + 1 < len(jobs):
            m_next = logits(pos + 1, *jobs[pos + 1])
        acc, l = _softmax_pass(i, m, vo_ref.at[g], s_ref.at[pos % 2], p_ref.at[pos % 2])
        a = acc / l
        o = a[:ATT_TILE] - lam * a[ATT_TILE:]
        o_ref[0, _tile_rows(i), _head_lanes(g)] = (_rms(o, sg_ref[...]) * (1.0 - lam_init)).astype(o_ref.dtype)


def _diff(proj, strips, q_gain, k_gain, lam_params, subln, lam_init):
    b, s, _ = proj.shape
    g = ATT_HEADS_PER_STEP
    assert DIFF_HEADS % g == 0 and MOBA_HEADS % g == 0 and COL_DIFF_Q % g == 0 and COL_DIFF_K % g == 0
    assert COL_DIFF_V % g == 0
    both = lambda gain: jnp.tile(gain, DIFF_MAPS).reshape(1, HEAD_DIM)
    cols = lambda first: pl.BlockSpec((1, s, g * HEAD_DIM), lambda bi, h: (bi, 0, first // g + h))
    return pl.pallas_call(
        functools.partial(_diff_kernel, lam_init),
        out_shape=jax.ShapeDtypeStruct((b, s, DIFF_HEADS * HEAD_DIM), BF16),
        grid=(b, DIFF_HEADS // g),
        in_specs=[
            cols(COL_DIFF_Q),
            cols(COL_DIFF_K),
            cols(COL_DIFF_V),
            pl.BlockSpec((1, HEAD_DIM), lambda bi, h: (0, 0)),
            pl.BlockSpec((1, HEAD_DIM), lambda bi, h: (0, 0)),
            pl.BlockSpec((4, DIFF_QK_DIM), lambda bi, h: (0, 0)),
            pl.BlockSpec((1, HEAD_DIM), lambda bi, h: (0, 0)),
            pl.BlockSpec((g, ATT_TILE, 2 * ATT_TILE), lambda bi, h: (MOBA_HEADS // g + h, 0, 0)),
        ],
        out_specs=cols(0),
        scratch_shapes=[
            pltpu.VMEM((g, s, HEAD_DIM), BF16),
            pltpu.VMEM((g, s, 2 * HEAD_DIM), BF16),
            pltpu.VMEM((2, DIFF_MAPS * ATT_TILE, s), F32),
            pltpu.VMEM((2, DIFF_MAPS * ATT_TILE, s), BF16),
        ],
        compiler_params=_params(("parallel", "parallel")),
        name="diff",
    )(proj, proj, proj, both(q_gain), both(k_gain), lam_params, subln.reshape(1, HEAD_DIM), strips)


def _memkv_kernel(m_ref, g_ref, w_ref, kg_ref, k_ref, v_ref):
    hm = _rms(m_ref[0], g_ref[...]).astype(BF16)
    kv = _dot(hm, w_ref[...])
    width = MEM_HEADS * HEAD_DIM
    for h in range(MEM_HEADS):
        cols = slice(h * HEAD_DIM, (h + 1) * HEAD_DIM)
        k_ref[0, :, cols] = _rms(kv[:, cols], kg_ref[...]).astype(BF16)
    v_ref[0] = kv[:, width:].astype(BF16)


def _memkv(mem, gain, wkv, k_gain, l):
    b, m, d = mem.shape
    width = MEM_HEADS * HEAD_DIM
    out = jax.ShapeDtypeStruct((b, m, width), BF16)
    return pl.pallas_call(
        _memkv_kernel,
        out_shape=(out, out),
        grid=(b,),
        in_specs=[
            pl.BlockSpec((1, m, d), lambda i: (i, 0, 0)),
            pl.BlockSpec((1, d), lambda i: (0, 0)),
            pl.BlockSpec((None, d, 2 * width), lambda i: (l, 0, 0)),
            pl.BlockSpec((1, HEAD_DIM), lambda i: (0, 0)),
        ],
        out_specs=(pl.BlockSpec((1, m, width), lambda i: (i, 0, 0)), pl.BlockSpec((1, m, width), lambda i: (i, 0, 0))),
        compiler_params=_params(("parallel",)),
        name="memkv",
    )(mem, gain.reshape(1, d), wkv, k_gain.reshape(1, HEAD_DIM))


def _mixout_kernel(x_ref, yc_ref, ym_ref, yd_ref, w_ref, g_ref, wq_ref, qg_ref, k_ref, v_ref, wo_ref, o_ref, ob_ref):
    c0 = CONV_CH
    c1 = c0 + MOBA_HEADS * HEAD_DIM
    x1 = (x_ref[0] + _dot(yc_ref[0], w_ref[0:c0, :]) + _dot(ym_ref[0], w_ref[c0:c1, :])
          + _dot(yd_ref[0], w_ref[c1:, :]))
    q = _dot(_rms(x1, g_ref[...]).astype(BF16), wq_ref[...])
    for h in range(MEM_HEADS):
        cols = slice(h * HEAD_DIM, (h + 1) * HEAD_DIM)
        qh = (_rms(q[:, cols], qg_ref[...]) * (HEAD_DIM ** -0.5 * LOG2E)).astype(BF16)
        s = _dot_nt(qh, k_ref[0, :, cols])
        p = jnp.exp2(s - jnp.max(s, axis=-1, keepdims=True))
        oh = _dot(p.astype(BF16), v_ref[0, :, cols]) / jnp.sum(p, axis=-1, keepdims=True)
        ob_ref[:, cols] = oh.astype(BF16)
    o_ref[0] = x1 + _dot(ob_ref[...], wo_ref[...])


def _mixout(x, yc, ym, yd, w, gain, wq, q_gain, kn, v, wo, l, tm=512):
    b, s, d = x.shape
    m = kn.shape[1]
    width = MEM_HEADS * HEAD_DIM
    once = pl.Buffered(1)
    rows = lambda a: pl.BlockSpec((1, tm, a.shape[2]), lambda bi, i: (bi, i, 0))
    return pl.pallas_call(
        _mixout_kernel,
        out_shape=jax.ShapeDtypeStruct((b, s, d), F32),
        grid=(b, s // tm),
        in_specs=[
            rows(x), rows(yc), rows(ym), rows(yd),
            pl.BlockSpec((None, w.shape[1], d), lambda bi, i: (l, 0, 0), pipeline_mode=once),
            pl.BlockSpec((1, d), lambda bi, i: (0, 0)),
            pl.BlockSpec((None, d, width), lambda bi, i: (l, 0, 0), pipeline_mode=once),
            pl.BlockSpec((1, HEAD_DIM), lambda bi, i: (0, 0)),
            pl.BlockSpec((1, m, width), lambda bi, i: (bi, 0, 0)),
            pl.BlockSpec((1, m, width), lambda bi, i: (bi, 0, 0)),
            pl.BlockSpec((None, width, d), lambda bi, i: (l, 0, 0), pipeline_mode=once),
        ],
        out_specs=rows(x),
        scratch_shapes=[pltpu.VMEM((tm, width), BF16)],
        compiler_params=_params(("parallel", "parallel")),
        name="mixout",
    )(x, yc, ym, yd, w, gain.reshape(1, d), wq, q_gain.reshape(1, HEAD_DIM), kn, v, wo)


def kernel(x, mem, rel_bias, ffn1_norm, ffn1_w_gate, ffn1_w_up, ffn1_w_down, mix_norm, w_in, conv_w, conv_b, conv_ln_g, conv_ln_b, moba_q_norm, moba_k_norm, diff_q_norm, diff_k_norm, diff_lambda, diff_subln, w_out, mem_norm_x, mem_norm_m, mem_wq, mem_wkv, mem_q_norm, mem_k_norm, mem_wo, ffn2_norm, ffn2_w_gate, ffn2_w_up, ffn2_w_down):
    b, s, d = x.shape
    depth = w_in.shape[0]
    t = b * s
    tab_flat = rel_bias.T.reshape(-1)
    strips = _bias_strips(tab_flat)
    ffn1_f32 = (ffn1_w_gate, ffn1_w_up, ffn1_w_down)
    ffn2_f32 = (ffn2_w_gate, ffn2_w_up, ffn2_w_down)
    ffn_w, ffn_tf = ffn1_f32, FFN_BLOCK_F32
    xt = x.reshape(t, d)
    for l in range(depth):
        xt, (*ffn_w, w_in_l) = _ffn(xt, ffn1_norm[l], *ffn_w, 0, side=[(w, l) for w in (*ffn2_f32, w_in)], tf=ffn_tf)
        ffn_tf = FFN_BLOCK_BF16
        proj = _proj(xt, mix_norm[l], w_in_l, 0).reshape(b, s, PROJ_WIDTH)
        y_conv, (w_out_l, wq_l, wkv_l, wo_l) = _conv(proj, conv_w[l], conv_b[l], conv_ln_g[l], conv_ln_b[l],
                                                    side=[(w, l) for w in (w_out, mem_wq, mem_wkv, mem_wo)])
        y_moba = _moba(proj, strips, moba_q_norm[l], moba_k_norm[l])
        lam_init = 0.8 - 0.6 * math.exp(-0.3 * l)
        y_diff = _diff(proj, strips, diff_q_norm[l], diff_k_norm[l], diff_lambda[l], diff_subln[l], lam_init)
        kn, v = _memkv(mem, mem_norm_m[l], wkv_l, mem_k_norm[l], 0)
        xt = _mixout(xt.reshape(b, s, d), y_conv, y_moba, y_diff, w_out_l, mem_norm_x[l], wq_l, mem_q_norm[l], kn, v,
                     wo_l, 0).reshape(t, d)
        nxt = [(w, l + 1) for w in ffn1_f32] if l + 1 < depth else []
        xt, ffn_w = _ffn(xt, ffn2_norm[l], *ffn_w, 0, side=nxt, tf=ffn_tf)
    return xt.reshape(b, s, d)
```

```python
import functools
import math

import jax
import jax.numpy as jnp
from jax import lax
from jax.experimental import pallas as pl
from jax.experimental.pallas import tpu as pltpu

F32 = jnp.float32
BF16 = jnp.bfloat16

CONV_CH = 512
CONV_WIDTH = 31
CONV_PAD = 32
HEAD_DIM = 128
LANES = 128
BF16_SUBLANES = 16
MOBA_HEADS = 6
MOBA_BLOCK = 256
MOBA_TOPK = 3
DIFF_HEADS = 6
DIFF_QK_DIM = 64
MEM_HEADS = 4
N_BUCKETS = 32
MAX_DISTANCE = 128
N_BIAS_HEADS = MOBA_HEADS + DIFF_HEADS
NEG_INF = -1e30
TAKEN = -3e38
PROJ_WIDTH = 5632
COL_CONV_A = 0
COL_CONV_G = CONV_CH // HEAD_DIM
COL_MOBA_Q = 2 * CONV_CH // HEAD_DIM
COL_MOBA_K = COL_MOBA_Q + MOBA_HEADS
COL_MOBA_V = COL_MOBA_K + MOBA_HEADS
COL_DIFF_Q = COL_MOBA_V + MOBA_HEADS
COL_DIFF_K = COL_DIFF_Q + DIFF_HEADS
COL_DIFF_V = COL_DIFF_K + DIFF_HEADS

ATT_TILE = 256
assert ATT_TILE == MOBA_BLOCK and ATT_TILE >= MAX_DISTANCE
LOG2E = 1.4426950408889634
VMEM_LIMIT = 56 * 1024 * 1024


def _params(semantics, vmem=VMEM_LIMIT):
    return pltpu.CompilerParams(dimension_semantics=semantics, vmem_limit_bytes=vmem)


def _rms(x, gain, eps=1e-6):
    return x * lax.rsqrt(jnp.mean(x * x, axis=-1, keepdims=True) + eps) * gain


def _dot(a, b):
    return jnp.dot(a, b, preferred_element_type=F32)


def _dot_nt(a, b, precision=None):
    return lax.dot_general(a, b, (((1,), (1,)), ((), ())), precision=precision, preferred_element_type=F32)


def _ffn_kernel(n_side, x_hbm, g_ref, wg_ref, wu_ref, wd_ref, *refs):
    side_in, o_ref, side_out = refs[:n_side], refs[n_side], refs[n_side + 1:2 * n_side + 1]
    h_ref, x_ref, x_sem = refs[2 * n_side + 1:]
    i, k = pl.program_id(0), pl.program_id(1)
    tm = x_ref.shape[0]

    def x_copy(block):
        return pltpu.make_async_copy(x_hbm.at[pl.ds(pl.multiple_of(block * tm, tm), tm), :], x_ref, x_sem)

    @pl.when(jnp.logical_and(i == 0, k == 0))
    def _():
        x_copy(0).start()

    @pl.when(k == 0)
    def _():
        x_copy(i).wait()
        x = x_ref[...]
        h_ref[...] = _rms(x, g_ref[...]).astype(BF16)
        o_ref[...] = x

    @pl.when(jnp.logical_and(k == 1, i + 1 < pl.num_programs(0)))
    def _():
        x_copy(i + 1).start()

    h = h_ref[...]
    gate = _dot(h, wg_ref[...].astype(BF16))
    up = _dot(h, wu_ref[...].astype(BF16))
    act = (0.5 * gate / (1.0 + jnp.exp(-gate)) * up).astype(BF16)
    o_ref[...] += _dot(act, wd_ref[...].astype(BF16))
    for w_ref, wb_ref in zip(side_in, side_out):
        wb_ref[...] = w_ref[...].astype(BF16)


def _side_block(shape, gi, gk):
    k, n = shape
    if k % gi == 0 and n % gk == 0 and (k // gi) % BF16_SUBLANES == 0 and (n // gk) % LANES == 0:
        return (k // gi, n // gk), (lambda i, kk: (i, kk))
    assert k % gk == 0 and n % gi == 0 and (k // gk) % BF16_SUBLANES == 0 and (n // gi) % LANES == 0, shape
    return (k // gk, n // gi), (lambda i, kk: (kk, i))


FFN_BLOCK_BF16 = 512
FFN_BLOCK_F32 = 256


def _ffn(x, gain, wg, wu, wd, l, side=(), tm=1024, tf=FFN_BLOCK_BF16):
    t, d = x.shape
    f = wg.shape[2]
    gi, gk = t // tm, f // tf
    side_in_specs, side_out_specs, side_shapes = [], [], []
    for w, ls in side:
        blk, order = _side_block(w.shape[1:], gi, gk)
        side_in_specs.append(pl.BlockSpec((None, *blk), lambda i, k, ls=ls, order=order: (ls, *order(i, k))))
        side_out_specs.append(pl.BlockSpec((None, *blk), lambda i, k, order=order: (0, *order(i, k))))
        side_shapes.append(jax.ShapeDtypeStruct((1, *w.shape[1:]), BF16))
    out, *cast = pl.pallas_call(
        functools.partial(_ffn_kernel, len(side)),
        out_shape=[jax.ShapeDtypeStruct((t, d), F32), *side_shapes],
        grid=(gi, gk),
        in_specs=[
            pl.BlockSpec(memory_space=pl.ANY),
            pl.BlockSpec((1, d), lambda i, k: (0, 0)),
            pl.BlockSpec((None, d, tf), lambda i, k: (l, 0, k)),
            pl.BlockSpec((None, d, tf), lambda i, k: (l, 0, k)),
            pl.BlockSpec((None, tf, d), lambda i, k: (l, k, 0)),
            *side_in_specs,
        ],
        out_specs=[pl.BlockSpec((tm, d), lambda i, k: (i, 0)), *side_out_specs],
        scratch_shapes=[pltpu.VMEM((tm, d), BF16), pltpu.VMEM((tm, d), F32), pltpu.SemaphoreType.DMA(())],
        compiler_params=_params(("arbitrary", "arbitrary")),
        name="ffn",
    )(x, gain.reshape(1, d), wg, wu, wd, *(w for w, _ in side))
    return out, cast


def _proj_kernel(x_ref, g_ref, w_ref, o_ref, h_ref):
    @pl.when(pl.program_id(1) == 0)
    def _():
        h_ref[...] = _rms(x_ref[...], g_ref[...]).astype(BF16)

    o_ref[...] = _dot(h_ref[...], w_ref[...]).astype(o_ref.dtype)


def _proj(x, gain, w, l, tm=2048, tn=512):
    t, d = x.shape
    n = w.shape[2]
    return pl.pallas_call(
        _proj_kernel,
        out_shape=jax.ShapeDtypeStruct((t, n), BF16),
        grid=(t // tm, n // tn),
        in_specs=[
            pl.BlockSpec((tm, d), lambda i, j: (i, 0)),
            pl.BlockSpec((1, d), lambda i, j: (0, 0)),
            pl.BlockSpec((None, d, tn), lambda i, j: (l, 0, j)),
        ],
        out_specs=pl.BlockSpec((tm, tn), lambda i, j: (i, j)),
        scratch_shapes=[pltpu.VMEM((tm, d), BF16)],
        compiler_params=_params(("parallel", "arbitrary")),
        name="proj",
    )(x, gain.reshape(1, d), w)


def _bias_kernel(tab_ref, o_ref):
    h = pl.program_id(0)
    far = tab_ref[h * N_BUCKETS + N_BUCKETS - 1]
    shape = (ATT_TILE, 2 * ATT_TILE)
    dist = ATT_TILE + lax.broadcasted_iota(jnp.int32, shape, 0) - lax.broadcasted_iota(jnp.int32, shape, 1)
    dist = jnp.maximum(dist, 0)
    max_exact = N_BUCKETS // 2
    log_ratio = jnp.log(jnp.maximum(dist, 1).astype(F32) / max_exact) / math.log(MAX_DISTANCE / max_exact)
    large = jnp.minimum(max_exact + (log_ratio * (N_BUCKETS - max_exact)).astype(jnp.int32), N_BUCKETS - 1)
    bucket = jnp.where(dist < max_exact, dist, large)
    out = jnp.zeros(shape, F32)
    for b in range(N_BUCKETS):
        out = jnp.where(bucket == b, (tab_ref[h * N_BUCKETS + b] - far) * LOG2E, out)
    o_ref[0] = out


def _bias_strips(tab_flat):
    return pl.pallas_call(
        _bias_kernel,
        out_shape=jax.ShapeDtypeStruct((N_BIAS_HEADS, ATT_TILE, 2 * ATT_TILE), F32),
        grid=(N_BIAS_HEADS,),
        in_specs=[pl.BlockSpec(memory_space=pltpu.SMEM)],
        out_specs=pl.BlockSpec((1, ATT_TILE, 2 * ATT_TILE), lambda h: (h, 0, 0)),
        compiler_params=_params(("arbitrary",)),
        name="bias_strips",
    )(tab_flat)


CONV_ROWS = 32
CONV_UNROLL = 4
SUBLANES = 8
assert CONV_PAD % SUBLANES == 0 and CONV_PAD >= SUBLANES * -(-CONV_WIDTH // SUBLANES)


def _conv_kernel(n_side, a_ref, g_ref, w_ref, cb_ref, lg_ref, lb_ref, *refs):
    side_in, o_ref, side_out, u_ref = refs[:n_side], refs[n_side], refs[n_side + 1:2 * n_side + 1], refs[-1]
    for wf_ref, wb_ref in zip(side_in, side_out):
        wb_ref[...] = wf_ref[...].astype(BF16)
    s = a_ref.shape[1]
    u_ref[0:CONV_PAD, :] = jnp.zeros((CONV_PAD, CONV_CH), F32)

    def glu(r, carry):
        rows = pl.ds(pl.multiple_of(r * CONV_ROWS, CONV_ROWS), CONV_ROWS)
        a = a_ref[0, rows, :].astype(F32)
        g = g_ref[0, rows, :].astype(F32)
        u_ref[pl.ds(pl.multiple_of(CONV_PAD + r * CONV_ROWS, CONV_ROWS), CONV_ROWS), :] = a / (1.0 + jnp.exp(-g))
        return carry

    lax.fori_loop(0, s // CONV_ROWS, glu, 0)

    win_rows = CONV_ROWS + CONV_PAD
    n_groups = CONV_CH // LANES

    def conv(r, carry):
        base = pl.multiple_of(r * CONV_ROWS, CONV_ROWS)
        accs = []
        for c in range(n_groups):
            lanes = slice(c * LANES, (c + 1) * LANES)
            win = u_ref[pl.ds(base, win_rows), lanes]
            acc = jnp.zeros((CONV_ROWS, LANES), F32) + cb_ref[:, lanes]
            for b in range(SUBLANES):
                shifted = pltpu.roll(win, b, 0) if b else win
                for a in range(-(-CONV_WIDTH // SUBLANES)):
                    d = SUBLANES * a + b
                    if d < CONV_WIDTH:
                        k0 = CONV_PAD - SUBLANES * a
                        tap = CONV_WIDTH - 1 - d
                        acc = acc + shifted[k0:k0 + CONV_ROWS, :] * w_ref[tap:tap + 1, lanes]
            accs.append(acc)
        mu = sum(jnp.sum(acc, axis=-1, keepdims=True) for acc in accs) / float(CONV_CH)
        cens = [acc - mu for acc in accs]
        var = sum(jnp.sum(cen * cen, axis=-1, keepdims=True) for cen in cens) / float(CONV_CH)
        inv = lax.rsqrt(var + 1e-5)
        for c, cen in enumerate(cens):
            lanes = slice(c * LANES, (c + 1) * LANES)
            y = cen * inv * lg_ref[:, lanes] + lb_ref[:, lanes]
            o_ref[0, pl.ds(base, CONV_ROWS), lanes] = (y / (1.0 + jnp.exp(-y))).astype(o_ref.dtype)
        return carry

    lax.fori_loop(0, s // CONV_ROWS, conv, 0, unroll=CONV_UNROLL)


def _conv(proj, w, cb, lg, lb, side=()):
    b, s, _ = proj.shape
    vec = lambda v: v.reshape(1, CONV_CH)
    side_in_specs, side_out_specs, side_shapes = [], [], []
    for wf, ls in side:
        k, n = wf.shape[1:]
        assert k % b == 0 and (k // b) % BF16_SUBLANES == 0, wf.shape
        side_in_specs.append(pl.BlockSpec((None, k // b, n), lambda i, ls=ls: (ls, i, 0)))
        side_out_specs.append(pl.BlockSpec((None, k // b, n), lambda i: (0, i, 0)))
        side_shapes.append(jax.ShapeDtypeStruct((1, k, n), BF16))
    out, *cast = pl.pallas_call(
        functools.partial(_conv_kernel, len(side)),
        out_shape=[jax.ShapeDtypeStruct((b, s, CONV_CH), BF16), *side_shapes],
        grid=(b,),
        in_specs=[
            pl.BlockSpec((1, s, CONV_CH), lambda i: (i, 0, COL_CONV_A * HEAD_DIM // CONV_CH)),
            pl.BlockSpec((1, s, CONV_CH), lambda i: (i, 0, COL_CONV_G * HEAD_DIM // CONV_CH)),
            pl.BlockSpec((CONV_WIDTH, CONV_CH), lambda i: (0, 0)),
            pl.BlockSpec((1, CONV_CH), lambda i: (0, 0)),
            pl.BlockSpec((1, CONV_CH), lambda i: (0, 0)),
            pl.BlockSpec((1, CONV_CH), lambda i: (0, 0)),
            *side_in_specs,
        ],
        out_specs=[pl.BlockSpec((1, s, CONV_CH), lambda i: (i, 0, 0)), *side_out_specs],
        scratch_shapes=[pltpu.VMEM((s + CONV_PAD, CONV_CH), F32)],
        compiler_params=_params(("parallel",)),
        name="conv",
    )(proj, proj, w, vec(cb), vec(lg), vec(lb), *(wf for wf, _ in side))
    return out, cast


def _tile_rows(j):
    return slice(j * ATT_TILE, (j + 1) * ATT_TILE)


def _lane_fold(op, x):
    out = x[:, :LANES]
    for c in range(1, x.shape[1] // LANES):
        out = op(out, x[:, c * LANES:(c + 1) * LANES])
    return out


def _half_rms(x, gain, eps=1e-6):
    half = x.shape[-1] // 2
    low = lax.broadcasted_iota(jnp.int32, x.shape, x.ndim - 1) < half
    sq = x * x
    ms = jnp.where(low, jnp.sum(jnp.where(low, sq, 0.0), axis=-1, keepdims=True),
                   jnp.sum(jnp.where(low, 0.0, sq), axis=-1, keepdims=True)) / half
    return x * lax.rsqrt(ms + eps) * gain


def _logits_pass(i, q, k_ref, strip_ref, sel_add, s_ref):
    maps = q.shape[0] // ATT_TILE
    shape = (q.shape[0], ATT_TILE)
    mask = lax.broadcasted_iota(jnp.int32, shape, 0) % ATT_TILE >= lax.broadcasted_iota(jnp.int32, shape, 1)
    rep = lambda x: x if maps == 1 else jnp.concatenate([x] * maps, axis=0)
    m_acc = None
    for j in range(i + 1):
        s = _dot_nt(q, k_ref[_tile_rows(j), :])
        if j == i:
            s = jnp.where(mask, s + rep(strip_ref[:, ATT_TILE:]), NEG_INF)
        elif j == i - 1:
            s = s + rep(strip_ref[:, :ATT_TILE])
        if sel_add is not None and j < i:
            s = s + sel_add[:, j:j + 1]
        s_ref[:, _tile_rows(j)] = s
        t = _lane_fold(jnp.maximum, s)
        m_acc = t if m_acc is None else jnp.maximum(m_acc, t)
    return jnp.max(m_acc, axis=-1, keepdims=True)


def _softmax_pass(i, m, vo_ref, s_ref, p_ref):
    for j in range(i + 1):
        p_ref[:, _tile_rows(j)] = jnp.exp2(s_ref[:, _tile_rows(j)] - m).astype(BF16)
    n = (i + 1) * ATT_TILE
    out = _dot(p_ref[:, :n], vo_ref[:n, :])
    return out[:, :HEAD_DIM], out[:, HEAD_DIM:]


ATT_HEADS_PER_STEP = 2


def _head_lanes(g):
    return slice(g * HEAD_DIM, (g + 1) * HEAD_DIM)


def _moba_kernel(q_ref, k_ref, v_ref, qg_ref, kg_ref, strip_ref, o_ref, kn_ref, vo_ref, kmean_ref, s_ref, p_ref):
    nb = k_ref.shape[1] // MOBA_BLOCK
    heads = k_ref.shape[2] // HEAD_DIM
    for g in range(heads):
        vo_ref[g, :, :HEAD_DIM] = v_ref[0, :, _head_lanes(g)]
        vo_ref[g, :, HEAD_DIM:] = jnp.ones((v_ref.shape[1], HEAD_DIM), BF16)
        for j in range(nb):
            rows = _tile_rows(j)
            kn = _rms(k_ref[0, rows, _head_lanes(g)].astype(F32), kg_ref[...])
            kn_ref[g, rows, :] = kn.astype(BF16)
            kmean_ref[g, j:j + 1, :] = jnp.sum(kn, axis=0, keepdims=True) / float(MOBA_BLOCK)

    eye = (lax.broadcasted_iota(jnp.int32, (MOBA_BLOCK, MOBA_BLOCK), 0)
           == lax.broadcasted_iota(jnp.int32, (MOBA_BLOCK, MOBA_BLOCK), 1)).astype(F32).astype(BF16)
    n_sel = min(MOBA_TOPK, nb)

    def logits(pos, g, i):
        qn = _rms(q_ref[0, _tile_rows(i), _head_lanes(g)].astype(F32), qg_ref[...])
        sel_add = None
        if i > n_sel:
            gate = _dot_nt(kmean_ref[g], qn, precision=lax.Precision.HIGHEST)
            blk = lax.broadcasted_iota(jnp.int32, gate.shape, 0)
            gate = jnp.where(blk < i, gate, TAKEN)
            sel = jnp.zeros(gate.shape, F32)
            for _ in range(n_sel):
                top = jnp.max(gate, axis=0, keepdims=True)
                first = jnp.min(jnp.where(gate == top, blk, nb), axis=0, keepdims=True)
                pick = blk == first
                sel = jnp.where(pick, 1.0, sel)
                gate = jnp.where(pick, TAKEN, gate)
            sel_t = _dot_nt(eye, sel.astype(BF16))
            sel_add = (sel_t - 1.0) * -NEG_INF
        q = (qn * (HEAD_DIM ** -0.5 * LOG2E)).astype(BF16)
        return _logits_pass(i, q, kn_ref.at[g], strip_ref.at[g], sel_add, s_ref.at[pos % 2])

    jobs = [(g, i) for g in range(heads) for i in range(nb)]
    m_next = logits(0, *jobs[0])
    for pos, (g, i) in enumerate(jobs):
        m = m_next
        if pos + 1 < len(jobs):
            m_next = logits(pos + 1, *jobs[pos + 1])
        acc, l = _softmax_pass(i, m, vo_ref.at[g], s_ref.at[pos % 2], p_ref.at[pos % 2])
        o_ref[0, _tile_rows(i), _head_lanes(g)] = (acc / l).astype(o_ref.dtype)


def _moba(proj, strips, q_gain, k_gain):
    b, s, _ = proj.shape
    nb = s // MOBA_BLOCK
    g = ATT_HEADS_PER_STEP
    assert MOBA_HEADS % g == 0 and COL_MOBA_Q % g == 0 and COL_MOBA_K % g == 0 and COL_MOBA_V % g == 0
    cols = lambda first: pl.BlockSpec((1, s, g * HEAD_DIM), lambda bi, h: (bi, 0, first // g + h))
    return pl.pallas_call(
        _moba_kernel,
        out_shape=jax.ShapeDtypeStruct((b, s, MOBA_HEADS * HEAD_DIM), BF16),
        grid=(b, MOBA_HEADS // g),
        in_specs=[
            cols(COL_MOBA_Q),
            cols(COL_MOBA_K),
            cols(COL_MOBA_V),
            pl.BlockSpec((1, HEAD_DIM), lambda bi, h: (0, 0)),
            pl.BlockSpec((1, HEAD_DIM), lambda bi, h: (0, 0)),
            pl.BlockSpec((g, ATT_TILE, 2 * ATT_TILE), lambda bi, h: (h, 0, 0)),
        ],
        out_specs=cols(0),
        scratch_shapes=[
            pltpu.VMEM((g, s, HEAD_DIM), BF16),
            pltpu.VMEM((g, s, 2 * HEAD_DIM), BF16),
            pltpu.VMEM((g, nb, HEAD_DIM), F32),
            pltpu.VMEM((2, ATT_TILE, s), F32),
            pltpu.VMEM((2, ATT_TILE, s), BF16),
        ],
        compiler_params=_params(("parallel", "parallel")),
        name="moba",
    )(proj, proj, proj, q_gain.reshape(1, HEAD_DIM), k_gain.reshape(1, HEAD_DIM), strips)


DIFF_MAPS = 2


def _diff_kernel(lam_init, q_ref, k_ref, v_ref, qg_ref, kg_ref, lam_ref, sg_ref, strip_ref, o_ref,
                 kn_ref, vo_ref, s_ref, p_ref):
    nt = k_ref.shape[1] // ATT_TILE
    heads = k_ref.shape[2] // HEAD_DIM
    assert DIFF_MAPS * DIFF_QK_DIM == HEAD_DIM
    for g in range(heads):
        vo_ref[g, :, :HEAD_DIM] = v_ref[0, :, _head_lanes(g)]
        vo_ref[g, :, HEAD_DIM:] = jnp.ones((v_ref.shape[1], HEAD_DIM), BF16)
        for j in range(nt):
            rows = _tile_rows(j)
            kn_ref[g, rows, :] = _half_rms(k_ref[0, rows, _head_lanes(g)].astype(F32), kg_ref[...]).astype(BF16)

    lp = lam_ref[...]
    lam = (jnp.exp(jnp.sum(lp[0:1] * lp[1:2], axis=-1, keepdims=True))
           - jnp.exp(jnp.sum(lp[2:3] * lp[3:4], axis=-1, keepdims=True)) + lam_init)
    scale = DIFF_QK_DIM ** -0.5 * LOG2E
    lane_map = lax.broadcasted_iota(jnp.int32, (ATT_TILE, HEAD_DIM), 1) // DIFF_QK_DIM

    def logits(pos, g, i):
        qn = _half_rms(q_ref[0, _tile_rows(i), _head_lanes(g)].astype(F32), qg_ref[...]) * scale
        q = jnp.concatenate([jnp.where(lane_map == c, qn, 0.0) for c in range(DIFF_MAPS)], axis=0).astype(BF16)
        return _logits_pass(i, q, kn_ref.at[g], strip_ref.at[g], None, s_ref.at[pos % 2])

    jobs = [(g, i) for g in range(heads) for i in range(nt)]
    m_next = logits(0, *jobs[0])
    for pos, (g, i) in enumerate(jobs):
        m = m_next
        if pos + 1 < len(jobs):
            m_next = logits(pos + 1, *jobs[pos + 1])
        acc, l = _softmax_pass(i, m, vo_ref.at[g], s_ref.at[pos % 2], p_ref.at[pos % 2])
        a = acc / l
        o = a[:ATT_TILE] - lam * a[ATT_TILE:]
        o_ref[0, _tile_rows(i), _head_lanes(g)] = (_rms(o, sg_ref[...]) * (1.0 - lam_init)).astype(o_ref.dtype)


def _diff(proj, strips, q_gain, k_gain, lam_params, subln, lam_init):
    b, s, _ = proj.shape
    g = ATT_HEADS_PER_STEP
    assert DIFF_HEADS % g == 0 and MOBA_HEADS % g == 0 and COL_DIFF_Q % g == 0 and COL_DIFF_K % g == 0
    assert COL_DIFF_V % g == 0
    both = lambda gain: jnp.tile(gain, DIFF_MAPS).reshape(1, HEAD_DIM)
    cols = lambda first: pl.BlockSpec((1, s, g * HEAD_DIM), lambda bi, h: (bi, 0, first // g + h))
    return pl.pallas_call(
        functools.partial(_diff_kernel, lam_init),
        out_shape=jax.ShapeDtypeStruct((b, s, DIFF_HEADS * HEAD_DIM), BF16),
        grid=(b, DIFF_HEADS // g),
        in_specs=[
            cols(COL_DIFF_Q),
            cols(COL_DIFF_K),
            cols(COL_DIFF_V),
            pl.BlockSpec((1, HEAD_DIM), lambda bi, h: (0, 0)),
            pl.BlockSpec((1, HEAD_DIM), lambda bi, h: (0, 0)),
            pl.BlockSpec((4, DIFF_QK_DIM), lambda bi, h: (0, 0)),
            pl.BlockSpec((1, HEAD_DIM), lambda bi, h: (0, 0)),
            pl.BlockSpec((g, ATT_TILE, 2 * ATT_TILE), lambda bi, h: (MOBA_HEADS // g + h, 0, 0)),
        ],
        out_specs=cols(0),
        scratch_shapes=[
            pltpu.VMEM((g, s, HEAD_DIM), BF16),
            pltpu.VMEM((g, s, 2 * HEAD_DIM), BF16),
            pltpu.VMEM((2, DIFF_MAPS * ATT_TILE, s), F32),
            pltpu.VMEM((2, DIFF_MAPS * ATT_TILE, s), BF16),
        ],
        compiler_params=_params(("parallel", "parallel")),
        name="diff",
    )(proj, proj, proj, both(q_gain), both(k_gain), lam_params, subln.reshape(1, HEAD_DIM), strips)


def _memkv_kernel(m_ref, g_ref, w_ref, kg_ref, k_ref, v_ref):
    hm = _rms(m_ref[0], g_ref[...]).astype(BF16)
    kv = _dot(hm, w_ref[...])
    width = MEM_HEADS * HEAD_DIM
    for h in range(MEM_HEADS):
        cols = slice(h * HEAD_DIM, (h + 1) * HEAD_DIM)
        k_ref[0, :, cols] = _rms(kv[:, cols], kg_ref[...]).astype(BF16)
    v_ref[0] = kv[:, width:].astype(BF16)


def _memkv(mem, gain, wkv, k_gain, l):
    b, m, d = mem.shape
    width = MEM_HEADS * HEAD_DIM
    out = jax.ShapeDtypeStruct((b, m, width), BF16)
    return pl.pallas_call(
        _memkv_kernel,
        out_shape=(out, out),
        grid=(b,),
        in_specs=[
            pl.BlockSpec((1, m, d), lambda i: (i, 0, 0)),
            pl.BlockSpec((1, d), lambda i: (0, 0)),
            pl.BlockSpec((None, d, 2 * width), lambda i: (l, 0, 0)),
            pl.BlockSpec((1, HEAD_DIM), lambda i: (0, 0)),
        ],
        out_specs=(pl.BlockSpec((1, m, width), lambda i: (i, 0, 0)), pl.BlockSpec((1, m, width), lambda i: (i, 0, 0))),
        compiler_params=_params(("parallel",)),
        name="memkv",
    )(mem, gain.reshape(1, d), wkv, k_gain.reshape(1, HEAD_DIM))


def _mixout_kernel(x_ref, yc_ref, ym_ref, yd_ref, w_ref, g_ref, wq_ref, qg_ref, k_ref, v_ref, wo_ref, o_ref, ob_ref):
    c0 = CONV_CH
    c1 = c0 + MOBA_HEADS * HEAD_DIM
    x1 = (x_ref[0] + _dot(yc_ref[0], w_ref[0:c0, :]) + _dot(ym_ref[0], w_ref[c0:c1, :])
          + _dot(yd_ref[0], w_ref[c1:, :]))
    q = _dot(_rms(x1, g_ref[...]).astype(BF16), wq_ref[...])
    for h in range(MEM_HEADS):
        cols = slice(h * HEAD_DIM, (h + 1) * HEAD_DIM)
        qh = (_rms(q[:, cols], qg_ref[...]) * (HEAD_DIM ** -0.5 * LOG2E)).astype(BF16)
        s = _dot_nt(qh, k_ref[0, :, cols])
        p = jnp.exp2(s - jnp.max(s, axis=-1, keepdims=True))
        oh = _dot(p.astype(BF16), v_ref[0, :, cols]) / jnp.sum(p, axis=-1, keepdims=True)
        ob_ref[:, cols] = oh.astype(BF16)
    o_ref[0] = x1 + _dot(ob_ref[...], wo_ref[...])


def _mixout(x, yc, ym, yd, w, gain, wq, q_gain, kn, v, wo, l, tm=512):
    b, s, d = x.shape
    m = kn.shape[1]
    width = MEM_HEADS * HEAD_DIM
    rows = lambda a: pl.BlockSpec((1, tm, a.shape[2]), lambda bi, i: (bi, i, 0))
    return pl.pallas_call(
        _mixout_kernel,
        out_shape=jax.ShapeDtypeStruct((b, s, d), F32),
        grid=(b, s // tm),
        in_specs=[
            rows(x), rows(yc), rows(ym), rows(yd),
            pl.BlockSpec((None, w.shape[1], d), lambda bi, i: (l, 0, 0)),
            pl.BlockSpec((1, d), lambda bi, i: (0, 0)),
            pl.BlockSpec((None, d, width), lambda bi, i: (l, 0, 0)),
            pl.BlockSpec((1, HEAD_DIM), lambda bi, i: (0, 0)),
            pl.BlockSpec((1, m, width), lambda bi, i: (bi, 0, 0)),
            pl.BlockSpec((1, m, width), lambda bi, i: (bi, 0, 0)),
            pl.BlockSpec((None, width, d), lambda bi, i: (l, 0, 0)),
        ],
        out_specs=rows(x),
        scratch_shapes=[pltpu.VMEM((tm, width), BF16)],
        compiler_params=_params(("parallel", "parallel")),
        name="mixout",
    )(x, yc, ym, yd, w, gain.reshape(1, d), wq, q_gain.reshape(1, HEAD_DIM), kn, v, wo)


def kernel(x, mem, rel_bias, ffn1_norm, ffn1_w_gate, ffn1_w_up, ffn1_w_down, mix_norm, w_in, conv_w, conv_b, conv_ln_g, conv_ln_b, moba_q_norm, moba_k_norm, diff_q_norm, diff_k_norm, diff_lambda, diff_subln, w_out, mem_norm_x, mem_norm_m, mem_wq, mem_wkv, mem_q_norm, mem_k_norm, mem_wo, ffn2_norm, ffn2_w_gate, ffn2_w_up, ffn2_w_down):
    b, s, d = x.shape
    depth = w_in.shape[0]
    t = b * s
    tab_flat = rel_bias.T.reshape(-1)
    strips = _bias_strips(tab_flat)
    ffn1_f32 = (ffn1_w_gate, ffn1_w_up, ffn1_w_down)
    ffn2_f32 = (ffn2_w_gate, ffn2_w_up, ffn2_w_down)
    ffn_w, ffn_tf = ffn1_f32, FFN_BLOCK_F32
    xt = x.reshape(t, d)
    for l in range(depth):
        xt, (*ffn_w, w_in_l) = _ffn(xt, ffn1_norm[l], *ffn_w, 0, side=[(w, l) for w in (*ffn2_f32, w_in)], tf=ffn_tf)
        ffn_tf = FFN_BLOCK_BF16
        proj = _proj(xt, mix_norm[l], w_in_l, 0).reshape(b, s, PROJ_WIDTH)
        y_conv, (w_out_l, wq_l, wkv_l, wo_l) = _conv(proj, conv_w[l], conv_b[l], conv_ln_g[l], conv_ln_b[l],
                                                    side=[(w, l) for w in (w_out, mem_wq, mem_wkv, mem_wo)])
        y_moba = _moba(proj, strips, moba_q_norm[l], moba_k_norm[l])
        lam_init = 0.8 - 0.6 * math.exp(-0.3 * l)
        y_diff = _diff(proj, strips, diff_q_norm[l], diff_k_norm[l], diff_lambda[l], diff_subln[l], lam_init)
        kn, v = _memkv(mem, mem_norm_m[l], wkv_l, mem_k_norm[l], 0)
        xt = _mixout(xt.reshape(b, s, d), y_conv, y_moba, y_diff, w_out_l, mem_norm_x[l], wq_l, mem_q_norm[l], kn, v,
                     wo_l, 0).reshape(t, d)
        nxt = [(w, l + 1) for w in ffn1_f32] if l + 1 < depth else []
        xt, ffn_w = _ffn(xt, ffn2_norm[l], *ffn_w, 0, side=nxt, tf=ffn_tf)
    return xt.reshape(b, s, d)
```

```python
import functools
import math

import jax
import jax.numpy as jnp
from jax import lax
from jax.experimental import pallas as pl
from jax.experimental.pallas import tpu as pltpu

F32 = jnp.float32
BF16 = jnp.bfloat16

CONV_CH = 512
CONV_WIDTH = 31
CONV_PAD = 32
HEAD_DIM = 128
LANES = 128
BF16_SUBLANES = 16
MOBA_HEADS = 6
MOBA_BLOCK = 256
MOBA_TOPK = 3
DIFF_HEADS = 6
DIFF_QK_DIM = 64
MEM_HEADS = 4
N_BUCKETS = 32
MAX_DISTANCE = 128
N_BIAS_HEADS = MOBA_HEADS + DIFF_HEADS
NEG_INF = -1e30
TAKEN = -3e38
PROJ_WIDTH = 5632
COL_CONV_A = 0
COL_CONV_G = CONV_CH // HEAD_DIM
COL_MOBA_Q = 2 * CONV_CH // HEAD_DIM
COL_MOBA_K = COL_MOBA_Q + MOBA_HEADS
COL_MOBA_V = COL_MOBA_K + MOBA_HEADS
COL_DIFF_Q = COL_MOBA_V + MOBA_HEADS
COL_DIFF_K = COL_DIFF_Q + DIFF_HEADS
COL_DIFF_V = COL_DIFF_K + DIFF_HEADS

ATT_TILE = 256
assert ATT_TILE == MOBA_BLOCK and ATT_TILE >= MAX_DISTANCE
LOG2E = 1.4426950408889634
VMEM_LIMIT = 56 * 1024 * 1024


def _params(semantics, vmem=VMEM_LIMIT):
    return pltpu.CompilerParams(dimension_semantics=semantics, vmem_limit_bytes=vmem)


def _rms(x, gain, eps=1e-6):
    return x * lax.rsqrt(jnp.mean(x * x, axis=-1, keepdims=True) + eps) * gain


def _dot(a, b):
    return jnp.dot(a, b, preferred_element_type=F32)


def _dot_nt(a, b, precision=None):
    return lax.dot_general(a, b, (((1,), (1,)), ((), ())), precision=precision, preferred_element_type=F32)


def _ffn_kernel(n_side, x_hbm, g_ref, wg_ref, wu_ref, wd_ref, *refs):
    side_in, o_ref, side_out = refs[:n_side], refs[n_side], refs[n_side + 1:2 * n_side + 1]
    h_ref, x_ref, x_sem = refs[2 * n_side + 1:]
    i, k = pl.program_id(0), pl.program_id(1)
    tm = x_ref.shape[0]

    def x_copy(block):
        return pltpu.make_async_copy(x_hbm.at[pl.ds(pl.multiple_of(block * tm, tm), tm), :], x_ref, x_sem)

    @pl.when(jnp.logical_and(i == 0, k == 0))
    def _():
        x_copy(0).start()

    @pl.when(k == 0)
    def _():
        x_copy(i).wait()
        x = x_ref[...]
        h_ref[...] = _rms(x, g_ref[...]).astype(BF16)
        o_ref[...] = x

    @pl.when(jnp.logical_and(k == 1, i + 1 < pl.num_programs(0)))
    def _():
        x_copy(i + 1).start()

    h = h_ref[...]
    gate = _dot(h, wg_ref[...].astype(BF16))
    up = _dot(h, wu_ref[...].astype(BF16))
    act = (0.5 * gate / (1.0 + jnp.exp(-gate)) * up).astype(BF16)
    o_ref[...] += _dot(act, wd_ref[...].astype(BF16))
    for w_ref, wb_ref in zip(side_in, side_out):
        wb_ref[...] = w_ref[...].astype(BF16)


def _side_block(shape, gi, gk):
    k, n = shape
    if k % gi == 0 and n % gk == 0 and (k // gi) % BF16_SUBLANES == 0 and (n // gk) % LANES == 0:
        return (k // gi, n // gk), (lambda i, kk: (i, kk))
    assert k % gk == 0 and n % gi == 0 and (k // gk) % BF16_SUBLANES == 0 and (n // gi) % LANES == 0, shape
    return (k // gk, n // gi), (lambda i, kk: (kk, i))


FFN_BLOCK_BF16 = 512
FFN_BLOCK_F32 = 256


def _ffn(x, gain, wg, wu, wd, l, side=(), tm=1024, tf=FFN_BLOCK_BF16):
    t, d = x.shape
    f = wg.shape[2]
    gi, gk = t // tm, f // tf
    side_in_specs, side_out_specs, side_shapes = [], [], []
    for w, ls in side:
        blk, order = _side_block(w.shape[1:], gi, gk)
        side_in_specs.append(pl.BlockSpec((None, *blk), lambda i, k, ls=ls, order=order: (ls, *order(i, k))))
        side_out_specs.append(pl.BlockSpec((None, *blk), lambda i, k, order=order: (0, *order(i, k))))
        side_shapes.append(jax.ShapeDtypeStruct((1, *w.shape[1:]), BF16))
    out, *cast = pl.pallas_call(
        functools.partial(_ffn_kernel, len(side)),
        out_shape=[jax.ShapeDtypeStruct((t, d), F32), *side_shapes],
        grid=(gi, gk),
        in_specs=[
            pl.BlockSpec(memory_space=pl.ANY),
            pl.BlockSpec((1, d), lambda i, k: (0, 0)),
            pl.BlockSpec((None, d, tf), lambda i, k: (l, 0, k)),
            pl.BlockSpec((None, d, tf), lambda i, k: (l, 0, k)),
            pl.BlockSpec((None, tf, d), lambda i, k: (l, k, 0)),
            *side_in_specs,
        ],
        out_specs=[pl.BlockSpec((tm, d), lambda i, k: (i, 0)), *side_out_specs],
        scratch_shapes=[pltpu.VMEM((tm, d), BF16), pltpu.VMEM((tm, d), F32), pltpu.SemaphoreType.DMA(())],
        compiler_params=_params(("arbitrary", "arbitrary")),
        name="ffn",
    )(x, gain.reshape(1, d), wg, wu, wd, *(w for w, _ in side))
    return out, cast


def _proj_kernel(x_ref, g_ref, w_ref, o_ref, h_ref):
    @pl.when(pl.program_id(1) == 0)
    def _():
        h_ref[...] = _rms(x_ref[...], g_ref[...]).astype(BF16)

    o_ref[...] = _dot(h_ref[...], w_ref[...]).astype(o_ref.dtype)


def _proj(x, gain, w, l, tm=2048, tn=512):
    t, d = x.shape
    n = w.shape[2]
    return pl.pallas_call(
        _proj_kernel,
        out_shape=jax.ShapeDtypeStruct((t, n), BF16),
        grid=(t // tm, n // tn),
        in_specs=[
            pl.BlockSpec((tm, d), lambda i, j: (i, 0)),
            pl.BlockSpec((1, d), lambda i, j: (0, 0)),
            pl.BlockSpec((None, d, tn), lambda i, j: (l, 0, j)),
        ],
        out_specs=pl.BlockSpec((tm, tn), lambda i, j: (i, j)),
        scratch_shapes=[pltpu.VMEM((tm, d), BF16)],
        compiler_params=_params(("parallel", "arbitrary")),
        name="proj",
    )(x, gain.reshape(1, d), w)


def _bias_kernel(tab_ref, o_ref):
    h = pl.program_id(0)
    far = tab_ref[h * N_BUCKETS + N_BUCKETS - 1]
    shape = (ATT_TILE, 2 * ATT_TILE)
    dist = ATT_TILE + lax.broadcasted_iota(jnp.int32, shape, 0) - lax.broadcasted_iota(jnp.int32, shape, 1)
    dist = jnp.maximum(dist, 0)
    max_exact = N_BUCKETS // 2
    log_ratio = jnp.log(jnp.maximum(dist, 1).astype(F32) / max_exact) / math.log(MAX_DISTANCE / max_exact)
    large = jnp.minimum(max_exact + (log_ratio * (N_BUCKETS - max_exact)).astype(jnp.int32), N_BUCKETS - 1)
    bucket = jnp.where(dist < max_exact, dist, large)
    out = jnp.zeros(shape, F32)
    for b in range(N_BUCKETS):
        out = jnp.where(bucket == b, (tab_ref[h * N_BUCKETS + b] - far) * LOG2E, out)
    o_ref[0] = out


def _bias_strips(tab_flat):
    return pl.pallas_call(
        _bias_kernel,
        out_shape=jax.ShapeDtypeStruct((N_BIAS_HEADS, ATT_TILE, 2 * ATT_TILE), F32),
        grid=(N_BIAS_HEADS,),
        in_specs=[pl.BlockSpec(memory_space=pltpu.SMEM)],
        out_specs=pl.BlockSpec((1, ATT_TILE, 2 * ATT_TILE), lambda h: (h, 0, 0)),
        compiler_params=_params(("arbitrary",)),
        name="bias_strips",
    )(tab_flat)


CONV_ROWS = 32
CONV_UNROLL = 8
SUBLANES = 8
assert CONV_PAD % SUBLANES == 0 and CONV_PAD >= SUBLANES * -(-CONV_WIDTH // SUBLANES)


def _conv_kernel(n_side, a_ref, g_ref, w_ref, cb_ref, lg_ref, lb_ref, *refs):
    side_in, o_ref, side_out, u_ref = refs[:n_side], refs[n_side], refs[n_side + 1:2 * n_side + 1], refs[-1]
    for wf_ref, wb_ref in zip(side_in, side_out):
        wb_ref[...] = wf_ref[...].astype(BF16)
    s = a_ref.shape[1]
    u_ref[0:CONV_PAD, :] = jnp.zeros((CONV_PAD, CONV_CH), F32)

    def glu(r, carry):
        rows = pl.ds(pl.multiple_of(r * CONV_ROWS, CONV_ROWS), CONV_ROWS)
        a = a_ref[0, rows, :].astype(F32)
        g = g_ref[0, rows, :].astype(F32)
        u_ref[pl.ds(pl.multiple_of(CONV_PAD + r * CONV_ROWS, CONV_ROWS), CONV_ROWS), :] = a / (1.0 + jnp.exp(-g))
        return carry

    lax.fori_loop(0, s // CONV_ROWS, glu, 0)

    win_rows = CONV_ROWS + CONV_PAD
    n_groups = CONV_CH // LANES

    def conv(r, carry):
        base = pl.multiple_of(r * CONV_ROWS, CONV_ROWS)
        accs = []
        for c in range(n_groups):
            lanes = slice(c * LANES, (c + 1) * LANES)
            win = u_ref[pl.ds(base, win_rows), lanes]
            acc = jnp.zeros((CONV_ROWS, LANES), F32) + cb_ref[:, lanes]
            for b in range(SUBLANES):
                shifted = pltpu.roll(win, b, 0) if b else win
                for a in range(-(-CONV_WIDTH // SUBLANES)):
                    d = SUBLANES * a + b
                    if d < CONV_WIDTH:
                        k0 = CONV_PAD - SUBLANES * a
                        tap = CONV_WIDTH - 1 - d
                        acc = acc + shifted[k0:k0 + CONV_ROWS, :] * w_ref[tap:tap + 1, lanes]
            accs.append(acc)
        mu = sum(jnp.sum(acc, axis=-1, keepdims=True) for acc in accs) / float(CONV_CH)
        cens = [acc - mu for acc in accs]
        var = sum(jnp.sum(cen * cen, axis=-1, keepdims=True) for cen in cens) / float(CONV_CH)
        inv = lax.rsqrt(var + 1e-5)
        for c, cen in enumerate(cens):
            lanes = slice(c * LANES, (c + 1) * LANES)
            y = cen * inv * lg_ref[:, lanes] + lb_ref[:, lanes]
            o_ref[0, pl.ds(base, CONV_ROWS), lanes] = (y / (1.0 + jnp.exp(-y))).astype(o_ref.dtype)
        return carry

    lax.fori_loop(0, s // CONV_ROWS, conv, 0, unroll=CONV_UNROLL)


def _conv(proj, w, cb, lg, lb, side=()):
    b, s, _ = proj.shape
    vec = lambda v: v.reshape(1, CONV_CH)
    side_in_specs, side_out_specs, side_shapes = [], [], []
    for wf, ls in side:
        k, n = wf.shape[1:]
        assert k % b == 0 and (k // b) % BF16_SUBLANES == 0, wf.shape
        side_in_specs.append(pl.BlockSpec((None, k // b, n), lambda i, ls=ls: (ls, i, 0)))
        side_out_specs.append(pl.BlockSpec((None, k // b, n), lambda i: (0, i, 0)))
        side_shapes.append(jax.ShapeDtypeStruct((1, k, n), BF16))
    out, *cast = pl.pallas_call(
        functools.partial(_conv_kernel, len(side)),
        out_shape=[jax.ShapeDtypeStruct((b, s, CONV_CH), BF16), *side_shapes],
        grid=(b,),
        in_specs=[
            pl.BlockSpec((1, s, CONV_CH), lambda i: (i, 0, COL_CONV_A * HEAD_DIM // CONV_CH)),
            pl.BlockSpec((1, s, CONV_CH), lambda i: (i, 0, COL_CONV_G * HEAD_DIM // CONV_CH)),
            pl.BlockSpec((CONV_WIDTH, CONV_CH), lambda i: (0, 0)),
            pl.BlockSpec((1, CONV_CH), lambda i: (0, 0)),
            pl.BlockSpec((1, CONV_CH), lambda i: (0, 0)),
            pl.BlockSpec((1, CONV_CH), lambda i: (0, 0)),
            *side_in_specs,
        ],
        out_specs=[pl.BlockSpec((1, s, CONV_CH), lambda i: (i, 0, 0)), *side_out_specs],
        scratch_shapes=[pltpu.VMEM((s + CONV_PAD, CONV_CH), F32)],
        compiler_params=_params(("parallel",)),
        name="conv",
    )(proj, proj, w, vec(cb), vec(lg), vec(lb), *(wf for wf, _ in side))
    return out, cast


def _tile_rows(j):
    return slice(j * ATT_TILE, (j + 1) * ATT_TILE)


def _lane_fold(op, x):
    out = x[:, :LANES]
    for c in range(1, x.shape[1] // LANES):
        out = op(out, x[:, c * LANES:(c + 1) * LANES])
    return out


def _half_rms(x, gain, eps=1e-6):
    half = x.shape[-1] // 2
    low = lax.broadcasted_iota(jnp.int32, x.shape, x.ndim - 1) < half
    sq = x * x
    ms = jnp.where(low, jnp.sum(jnp.where(low, sq, 0.0), axis=-1, keepdims=True),
                   jnp.sum(jnp.where(low, 0.0, sq), axis=-1, keepdims=True)) / half
    return x * lax.rsqrt(ms + eps) * gain


def _logits_pass(i, q, k_ref, strip_ref, sel_add, s_ref):
    maps = q.shape[0] // ATT_TILE
    shape = (q.shape[0], ATT_TILE)
    mask = lax.broadcasted_iota(jnp.int32, shape, 0) % ATT_TILE >= lax.broadcasted_iota(jnp.int32, shape, 1)
    rep = lambda x: x if maps == 1 else jnp.concatenate([x] * maps, axis=0)
    m_acc = None
    for j in range(i + 1):
        s = _dot_nt(q, k_ref[_tile_rows(j), :])
        if j == i:
            s = jnp.where(mask, s + rep(strip_ref[:, ATT_TILE:]), NEG_INF)
        elif j == i - 1:
            s = s + rep(strip_ref[:, :ATT_TILE])
        if sel_add is not None and j < i:
            s = s + sel_add[:, j:j + 1]
        s_ref[:, _tile_rows(j)] = s
        t = _lane_fold(jnp.maximum, s)
        m_acc = t if m_acc is None else jnp.maximum(m_acc, t)
    return jnp.max(m_acc, axis=-1, keepdims=True)


def _softmax_pass(i, m, vo_ref, s_ref, p_ref):
    for j in range(i + 1):
        p_ref[:, _tile_rows(j)] = jnp.exp2(s_ref[:, _tile_rows(j)] - m).astype(BF16)
    n = (i + 1) * ATT_TILE
    out = _dot(p_ref[:, :n], vo_ref[:n, :])
    return out[:, :HEAD_DIM], out[:, HEAD_DIM:]


ATT_HEADS_PER_STEP = 2


def _head_lanes(g):
    return slice(g * HEAD_DIM, (g + 1) * HEAD_DIM)


def _moba_kernel(q_ref, k_ref, v_ref, qg_ref, kg_ref, strip_ref, o_ref, kn_ref, vo_ref, kmean_ref, s_ref, p_ref):
    nb = k_ref.shape[1] // MOBA_BLOCK
    heads = k_ref.shape[2] // HEAD_DIM
    for g in range(heads):
        vo_ref[g, :, :HEAD_DIM] = v_ref[0, :, _head_lanes(g)]
        vo_ref[g, :, HEAD_DIM:] = jnp.ones((v_ref.shape[1], HEAD_DIM), BF16)
        for j in range(nb):
            rows = _tile_rows(j)
            kn = _rms(k_ref[0, rows, _head_lanes(g)].astype(F32), kg_ref[...])
            kn_ref[g, rows, :] = kn.astype(BF16)
            kmean_ref[g, j:j + 1, :] = jnp.sum(kn, axis=0, keepdims=True) / float(MOBA_BLOCK)

    eye = (lax.broadcasted_iota(jnp.int32, (MOBA_BLOCK, MOBA_BLOCK), 0)
           == lax.broadcasted_iota(jnp.int32, (MOBA_BLOCK, MOBA_BLOCK), 1)).astype(F32).astype(BF16)
    n_sel = min(MOBA_TOPK, nb)

    def logits(pos, g, i):
        qn = _rms(q_ref[0, _tile_rows(i), _head_lanes(g)].astype(F32), qg_ref[...])
        sel_add = None
        if i > n_sel:
            gate = _dot_nt(kmean_ref[g], qn, precision=lax.Precision.HIGHEST)
            blk = lax.broadcasted_iota(jnp.int32, gate.shape, 0)
            gate = jnp.where(blk < i, gate, TAKEN)
            sel = jnp.zeros(gate.shape, F32)
            for _ in range(n_sel):
                top = jnp.max(gate, axis=0, keepdims=True)
                first = jnp.min(jnp.where(gate == top, blk, nb), axis=0, keepdims=True)
                pick = blk == first
                sel = jnp.where(pick, 1.0, sel)
                gate = jnp.where(pick, TAKEN, gate)
            sel_t = _dot_nt(eye, sel.astype(BF16))
            sel_add = (sel_t - 1.0) * -NEG_INF
        q = (qn * (HEAD_DIM ** -0.5 * LOG2E)).astype(BF16)
        return _logits_pass(i, q, kn_ref.at[g], strip_ref.at[g], sel_add, s_ref.at[pos % 2])

    jobs = [(g, i) for g in range(heads) for i in range(nb)]
    m_next = logits(0, *jobs[0])
    for pos, (g, i) in enumerate(jobs):
        m = m_next
        if pos + 1 < len(jobs):
            m_next = logits(pos + 1, *jobs[pos + 1])
        acc, l = _softmax_pass(i, m, vo_ref.at[g], s_ref.at[pos % 2], p_ref.at[pos % 2])
        o_ref[0, _tile_rows(i), _head_lanes(g)] = (acc / l).astype(o_ref.dtype)


def _moba(proj, strips, q_gain, k_gain):
    b, s, _ = proj.shape
    nb = s // MOBA_BLOCK
    g = ATT_HEADS_PER_STEP
    assert MOBA_HEADS % g == 0 and COL_MOBA_Q % g == 0 and COL_MOBA_K % g == 0 and COL_MOBA_V % g == 0
    cols = lambda first: pl.BlockSpec((1, s, g * HEAD_DIM), lambda bi, h: (bi, 0, first // g + h))
    return pl.pallas_call(
        _moba_kernel,
        out_shape=jax.ShapeDtypeStruct((b, s, MOBA_HEADS * HEAD_DIM), BF16),
        grid=(b, MOBA_HEADS // g),
        in_specs=[
            cols(COL_MOBA_Q),
            cols(COL_MOBA_K),
            cols(COL_MOBA_V),
            pl.BlockSpec((1, HEAD_DIM), lambda bi, h: (0, 0)),
            pl.BlockSpec((1, HEAD_DIM), lambda bi, h: (0, 0)),
            pl.BlockSpec((g, ATT_TILE, 2 * ATT_TILE), lambda bi, h: (h, 0, 0)),
        ],
        out_specs=cols(0),
        scratch_shapes=[
            pltpu.VMEM((g, s, HEAD_DIM), BF16),
            pltpu.VMEM((g, s, 2 * HEAD_DIM), BF16),
            pltpu.VMEM((g, nb, HEAD_DIM), F32),
            pltpu.VMEM((2, ATT_TILE, s), F32),
            pltpu.VMEM((2, ATT_TILE, s), BF16),
        ],
        compiler_params=_params(("parallel", "parallel")),
        name="moba",
    )(proj, proj, proj, q_gain.reshape(1, HEAD_DIM), k_gain.reshape(1, HEAD_DIM), strips)


DIFF_MAPS = 2


def _diff_kernel(lam_init, q_ref, k_ref, v_ref, qg_ref, kg_ref, lam_ref, sg_ref, strip_ref, o_ref,
                 kn_ref, vo_ref, s_ref, p_ref):
    nt = k_ref.shape[1] // ATT_TILE
    heads = k_ref.shape[2] // HEAD_DIM
    assert DIFF_MAPS * DIFF_QK_DIM == HEAD_DIM
    for g in range(heads):
        vo_ref[g, :, :HEAD_DIM] = v_ref[0, :, _head_lanes(g)]
        vo_ref[g, :, HEAD_DIM:] = jnp.ones((v_ref.shape[1], HEAD_DIM), BF16)
        for j in range(nt):
            rows = _tile_rows(j)
            kn_ref[g, rows, :] = _half_rms(k_ref[0, rows, _head_lanes(g)].astype(F32), kg_ref[...]).astype(BF16)

    lp = lam_ref[...]
    lam = (jnp.exp(jnp.sum(lp[0:1] * lp[1:2], axis=-1, keepdims=True))
           - jnp.exp(jnp.sum(lp[2:3] * lp[3:4], axis=-1, keepdims=True)) + lam_init)
    scale = DIFF_QK_DIM ** -0.5 * LOG2E
    lane_map = lax.broadcasted_iota(jnp.int32, (ATT_TILE, HEAD_DIM), 1) // DIFF_QK_DIM

    def logits(pos, g, i):
        qn = _half_rms(q_ref[0, _tile_rows(i), _head_lanes(g)].astype(F32), qg_ref[...]) * scale
        q = jnp.concatenate([jnp.where(lane_map == c, qn, 0.0) for c in range(DIFF_MAPS)], axis=0).astype(BF16)
        return _logits_pass(i, q, kn_ref.at[g], strip_ref.at[g], None, s_ref.at[pos % 2])

    jobs = [(g, i) for g in range(heads) for i in reversed(range(nt))]
    m_next = logits(0, *jobs[0])
    for pos, (g, i) in enumerate(jobs):
        m = m_next
        if pos + 1 < len(jobs):
            m_next = logits(pos + 1, *jobs[pos + 1])
        acc, l = _softmax_pass(i, m, vo_ref.at[g], s_ref.at[pos % 2], p_ref.at[pos % 2])
        a = acc / l
        o = a[:ATT_TILE] - lam * a[ATT_TILE:]
        o_ref[0, _tile_rows(i), _head_lanes(g)] = (_rms(o, sg_ref[...]) * (1.0 - lam_init)).astype(o_ref.dtype)


def _diff(proj, strips, q_gain, k_gain, lam_params, subln, lam_init):
    b, s, _ = proj.shape
    g = ATT_HEADS_PER_STEP
    assert DIFF_HEADS % g == 0 and MOBA_HEADS % g == 0 and COL_DIFF_Q % g == 0 and COL_DIFF_K % g == 0
    assert COL_DIFF_V % g == 0
    both = lambda gain: jnp.tile(gain, DIFF_MAPS).reshape(1, HEAD_DIM)
    cols = lambda first: pl.BlockSpec((1, s, g * HEAD_DIM), lambda bi, h: (bi, 0, first // g + h))
    return pl.pallas_call(
        functools.partial(_diff_kernel, lam_init),
        out_shape=jax.ShapeDtypeStruct((b, s, DIFF_HEADS * HEAD_DIM), BF16),
        grid=(b, DIFF_HEADS // g),
        in_specs=[
            cols(COL_DIFF_Q),
            cols(COL_DIFF_K),
            cols(COL_DIFF_V),
            pl.BlockSpec((1, HEAD_DIM), lambda bi, h: (0, 0)),
            pl.BlockSpec((1, HEAD_DIM), lambda bi, h: (0, 0)),
            pl.BlockSpec((4, DIFF_QK_DIM), lambda bi, h: (0, 0)),
            pl.BlockSpec((1, HEAD_DIM), lambda bi, h: (0, 0)),
            pl.BlockSpec((g, ATT_TILE, 2 * ATT_TILE), lambda bi, h: (MOBA_HEADS // g + h, 0, 0)),
        ],
        out_specs=cols(0),
        scratch_shapes=[
            pltpu.VMEM((g, s, HEAD_DIM), BF16),
            pltpu.VMEM((g, s, 2 * HEAD_DIM), BF16),
            pltpu.VMEM((2, DIFF_MAPS * ATT_TILE, s), F32),
            pltpu.VMEM((2, DIFF_MAPS * ATT_TILE, s), BF16),
        ],
        compiler_params=_params(("parallel", "parallel")),
        name="diff",
    )(proj, proj, proj, both(q_gain), both(k_gain), lam_params, subln.reshape(1, HEAD_DIM), strips)


def _memkv_kernel(m_ref, g_ref, w_ref, kg_ref, k_ref, v_ref):
    hm = _rms(m_ref[...], g_ref[...]).astype(BF16)
    kv = _dot(hm, w_ref[...])
    width = MEM_HEADS * HEAD_DIM
    for h in range(MEM_HEADS):
        cols = slice(h * HEAD_DIM, (h + 1) * HEAD_DIM)
        k_ref[:, cols] = _rms(kv[:, cols], kg_ref[...]).astype(BF16)
    v_ref[...] = kv[:, width:].astype(BF16)


def _memkv(mem, gain, wkv, k_gain, l):
    b, m, d = mem.shape
    width = MEM_HEADS * HEAD_DIM
    out = jax.ShapeDtypeStruct((b * m, width), BF16)
    kn, v = pl.pallas_call(
        _memkv_kernel,
        out_shape=(out, out),
        grid=(1,),
        in_specs=[
            pl.BlockSpec((b * m, d), lambda i: (0, 0)),
            pl.BlockSpec((1, d), lambda i: (0, 0)),
            pl.BlockSpec((None, d, 2 * width), lambda i: (l, 0, 0)),
            pl.BlockSpec((1, HEAD_DIM), lambda i: (0, 0)),
        ],
        out_specs=(pl.BlockSpec((b * m, width), lambda i: (0, 0)), pl.BlockSpec((b * m, width), lambda i: (0, 0))),
        compiler_params=_params(("arbitrary",)),
        name="memkv",
    )(mem.reshape(b * m, d), gain.reshape(1, d), wkv, k_gain.reshape(1, HEAD_DIM))
    return kn.reshape(b, m, width), v.reshape(b, m, width)


def _mixout_kernel(x_ref, yc_ref, ym_ref, yd_ref, w_ref, g_ref, wq_ref, qg_ref, k_ref, v_ref, wo_ref, o_ref, ob_ref):
    c0 = CONV_CH
    c1 = c0 + MOBA_HEADS * HEAD_DIM
    x1 = (x_ref[0] + _dot(yc_ref[0], w_ref[0:c0, :]) + _dot(ym_ref[0], w_ref[c0:c1, :])
          + _dot(yd_ref[0], w_ref[c1:, :]))
    q = _dot(_rms(x1, g_ref[...]).astype(BF16), wq_ref[...])
    for h in range(MEM_HEADS):
        cols = slice(h * HEAD_DIM, (h + 1) * HEAD_DIM)
        qh = (_rms(q[:, cols], qg_ref[...]) * (HEAD_DIM ** -0.5 * LOG2E)).astype(BF16)
        s = _dot_nt(qh, k_ref[0, :, cols])
        p = jnp.exp2(s - jnp.max(s, axis=-1, keepdims=True))
        oh = _dot(p.astype(BF16), v_ref[0, :, cols]) / jnp.sum(p, axis=-1, keepdims=True)
        ob_ref[:, cols] = oh.astype(BF16)
    o_ref[0] = x1 + _dot(ob_ref[...], wo_ref[...])


def _mixout(x, yc, ym, yd, w, gain, wq, q_gain, kn, v, wo, l, tm=512):
    b, s, d = x.shape
    m = kn.shape[1]
    width = MEM_HEADS * HEAD_DIM
    once = pl.Buffered(1)
    rows = lambda a: pl.BlockSpec((1, tm, a.shape[2]), lambda bi, i: (bi, i, 0))
    return pl.pallas_call(
        _mixout_kernel,
        out_shape=jax.ShapeDtypeStruct((b, s, d), F32),
        grid=(b, s // tm),
        in_specs=[
            rows(x), rows(yc), rows(ym), rows(yd),
            pl.BlockSpec((None, w.shape[1], d), lambda bi, i: (l, 0, 0), pipeline_mode=once),
            pl.BlockSpec((1, d), lambda bi, i: (0, 0)),
            pl.BlockSpec((None, d, width), lambda bi, i: (l, 0, 0), pipeline_mode=once),
            pl.BlockSpec((1, HEAD_DIM), lambda bi, i: (0, 0)),
            pl.BlockSpec((1, m, width), lambda bi, i: (bi, 0, 0)),
            pl.BlockSpec((1, m, width), lambda bi, i: (bi, 0, 0)),
            pl.BlockSpec((None, width, d), lambda bi, i: (l, 0, 0), pipeline_mode=once),
        ],
        out_specs=rows(x),
        scratch_shapes=[pltpu.VMEM((tm, width), BF16)],
        compiler_params=_params(("parallel", "parallel")),
        name="mixout",
    )(x, yc, ym, yd, w, gain.reshape(1, d), wq, q_gain.reshape(1, HEAD_DIM), kn, v, wo)


def kernel(x, mem, rel_bias, ffn1_norm, ffn1_w_gate, ffn1_w_up, ffn1_w_down, mix_norm, w_in, conv_w, conv_b, conv_ln_g, conv_ln_b, moba_q_norm, moba_k_norm, diff_q_norm, diff_k_norm, diff_lambda, diff_subln, w_out, mem_norm_x, mem_norm_m, mem_wq, mem_wkv, mem_q_norm, mem_k_norm, mem_wo, ffn2_norm, ffn2_w_gate, ffn2_w_up, ffn2_w_down):
    b, s, d = x.shape
    depth = w_in.shape[0]
    t = b * s
    tab_flat = rel_bias.T.reshape(-1)
    strips = _bias_strips(tab_flat)
    ffn1_f32 = (ffn1_w_gate, ffn1_w_up, ffn1_w_down)
    ffn2_f32 = (ffn2_w_gate, ffn2_w_up, ffn2_w_down)
    ffn_w, ffn_tf = ffn1_f32, FFN_BLOCK_F32
    xt = x.reshape(t, d)
    for l in range(depth):
        xt, (*ffn_w, w_in_l) = _ffn(xt, ffn1_norm[l], *ffn_w, 0, side=[(w, l) for w in (*ffn2_f32, w_in)], tf=ffn_tf)
        ffn_tf = FFN_BLOCK_BF16
        proj = _proj(xt, mix_norm[l], w_in_l, 0).reshape(b, s, PROJ_WIDTH)
        y_conv, (w_out_l, wq_l, wkv_l, wo_l) = _conv(proj, conv_w[l], conv_b[l], conv_ln_g[l], conv_ln_b[l],
                                                    side=[(w, l) for w in (w_out, mem_wq, mem_wkv, mem_wo)])
        y_moba = _moba(proj, strips, moba_q_norm[l], moba_k_norm[l])
        lam_init = 0.8 - 0.6 * math.exp(-0.3 * l)
        y_diff = _diff(proj, strips, diff_q_norm[l], diff_k_norm[l], diff_lambda[l], diff_subln[l], lam_init)
        kn, v = _memkv(mem, mem_norm_m[l], wkv_l, mem_k_norm[l], 0)
        xt = _mixout(xt.reshape(b, s, d), y_conv, y_moba, y_diff, w_out_l, mem_norm_x[l], wq_l, mem_q_norm[l], kn, v,
                     wo_l, 0).reshape(t, d)
        nxt = [(w, l + 1) for w in ffn1_f32] if l + 1 < depth else []
        xt, ffn_w = _ffn(xt, ffn2_norm[l], *ffn_w, 0, side=nxt, tf=ffn_tf)
    return xt.reshape(b, s, d)
```

```python
import functools
import math

import jax
import jax.numpy as jnp
from jax import lax
from jax.experimental import pallas as pl
from jax.experimental.pallas import tpu as pltpu

F32 = jnp.float32
BF16 = jnp.bfloat16

CONV_CH = 512
CONV_WIDTH = 31
CONV_PAD = 32
HEAD_DIM = 128
LANES = 128
BF16_SUBLANES = 16
MOBA_HEADS = 6
MOBA_BLOCK = 256
MOBA_TOPK = 3
DIFF_HEADS = 6
DIFF_QK_DIM = 64
MEM_HEADS = 4
N_BUCKETS = 32
MAX_DISTANCE = 128
N_BIAS_HEADS = MOBA_HEADS + DIFF_HEADS
NEG_INF = -1e30
TAKEN = -3e38
PROJ_WIDTH = 5632
COL_CONV_A = 0
COL_CONV_G = CONV_CH // HEAD_DIM
COL_MOBA_Q = 2 * CONV_CH // HEAD_DIM
COL_MOBA_K = COL_MOBA_Q + MOBA_HEADS
COL_MOBA_V = COL_MOBA_K + MOBA_HEADS
COL_DIFF_Q = COL_MOBA_V + MOBA_HEADS
COL_DIFF_K = COL_DIFF_Q + DIFF_HEADS
COL_DIFF_V = COL_DIFF_K + DIFF_HEADS

ATT_TILE = 256
assert ATT_TILE == MOBA_BLOCK and ATT_TILE >= MAX_DISTANCE
LOG2E = 1.4426950408889634
VMEM_LIMIT = 56 * 1024 * 1024


def _params(semantics, vmem=VMEM_LIMIT):
    return pltpu.CompilerParams(dimension_semantics=semantics, vmem_limit_bytes=vmem)


def _rms(x, gain, eps=1e-6):
    return x * lax.rsqrt(jnp.mean(x * x, axis=-1, keepdims=True) + eps) * gain


def _dot(a, b):
    return jnp.dot(a, b, preferred_element_type=F32)


def _dot_nt(a, b, precision=None):
    return lax.dot_general(a, b, (((1,), (1,)), ((), ())), precision=precision, preferred_element_type=F32)


def _ffn_kernel(n_side, x_hbm, g_ref, wg_ref, wu_ref, wd_ref, *refs):
    side_in, o_ref, side_out = refs[:n_side], refs[n_side], refs[n_side + 1:2 * n_side + 1]
    h_ref, x_ref, x_sem = refs[2 * n_side + 1:]
    i, k = pl.program_id(0), pl.program_id(1)
    tm = x_ref.shape[0]

    def x_copy(block):
        return pltpu.make_async_copy(x_hbm.at[pl.ds(pl.multiple_of(block * tm, tm), tm), :], x_ref, x_sem)

    @pl.when(jnp.logical_and(i == 0, k == 0))
    def _():
        x_copy(0).start()

    @pl.when(k == 0)
    def _():
        x_copy(i).wait()
        x = x_ref[...]
        h_ref[...] = _rms(x, g_ref[...]).astype(BF16)
        o_ref[...] = x

    @pl.when(jnp.logical_and(k == 1, i + 1 < pl.num_programs(0)))
    def _():
        x_copy(i + 1).start()

    pin = None
    for w_ref, wb_ref in zip(side_in, side_out):
        w = w_ref[...]
        wb_ref[...] = w.astype(BF16)
        fold = w[0:SUBLANES, :]
        for r in range(1, w.shape[0] // SUBLANES):
            fold = jnp.maximum(fold, w[r * SUBLANES:(r + 1) * SUBLANES, :])
        fold = _lane_fold(jnp.maximum, fold)
        pin = fold if pin is None else jnp.maximum(pin, fold)
    h = h_ref[...]
    gate = _dot(h, wg_ref[...].astype(BF16))
    up = _dot(h, wu_ref[...].astype(BF16))
    if pin is not None:
        zero = pltpu.bitcast(lax.shift_right_logical(pltpu.bitcast(pin, jnp.uint32), jnp.uint32(32)), F32)
        gate = gate + zero[0:1, 0:1]
    act = (0.5 * gate / (1.0 + jnp.exp(-gate)) * up).astype(BF16)
    o_ref[...] += _dot(act, wd_ref[...].astype(BF16))


def _side_block(shape, gi, gk):
    k, n = shape
    if k % gi == 0 and n % gk == 0 and (k // gi) % BF16_SUBLANES == 0 and (n // gk) % LANES == 0:
        return (k // gi, n // gk), (lambda i, kk: (i, kk))
    assert k % gk == 0 and n % gi == 0 and (k // gk) % BF16_SUBLANES == 0 and (n // gi) % LANES == 0, shape
    return (k // gk, n // gi), (lambda i, kk: (kk, i))


FFN_BLOCK_BF16 = 512
FFN_BLOCK_F32 = 256


def _ffn(x, gain, wg, wu, wd, l, side=(), tm=1024, tf=FFN_BLOCK_BF16):
    t, d = x.shape
    f = wg.shape[2]
    gi, gk = t // tm, f // tf
    side_in_specs, side_out_specs, side_shapes = [], [], []
    for w, ls in side:
        blk, order = _side_block(w.shape[1:], gi, gk)
        side_in_specs.append(pl.BlockSpec((None, *blk), lambda i, k, ls=ls, order=order: (ls, *order(i, k))))
        side_out_specs.append(pl.BlockSpec((None, *blk), lambda i, k, order=order: (0, *order(i, k))))
        side_shapes.append(jax.ShapeDtypeStruct((1, *w.shape[1:]), BF16))
    out, *cast = pl.pallas_call(
        functools.partial(_ffn_kernel, len(side)),
        out_shape=[jax.ShapeDtypeStruct((t, d), F32), *side_shapes],
        grid=(gi, gk),
        in_specs=[
            pl.BlockSpec(memory_space=pl.ANY),
            pl.BlockSpec((1, d), lambda i, k: (0, 0)),
            pl.BlockSpec((None, d, tf), lambda i, k: (l, 0, k)),
            pl.BlockSpec((None, d, tf), lambda i, k: (l, 0, k)),
            pl.BlockSpec((None, tf, d), lambda i, k: (l, k, 0)),
            *side_in_specs,
        ],
        out_specs=[pl.BlockSpec((tm, d), lambda i, k: (i, 0)), *side_out_specs],
        scratch_shapes=[pltpu.VMEM((tm, d), BF16), pltpu.VMEM((tm, d), F32), pltpu.SemaphoreType.DMA(())],
        compiler_params=_params(("arbitrary", "arbitrary")),
        name="ffn",
    )(x, gain.reshape(1, d), wg, wu, wd, *(w for w, _ in side))
    return out, cast


def _proj_kernel(x_ref, g_ref, w_ref, o_ref, h_ref):
    @pl.when(pl.program_id(1) == 0)
    def _():
        h_ref[...] = _rms(x_ref[...], g_ref[...]).astype(BF16)

    o_ref[...] = _dot(h_ref[...], w_ref[...]).astype(o_ref.dtype)


def _proj(x, gain, w, l, tm=2048, tn=512):
    t, d = x.shape
    n = w.shape[2]
    return pl.pallas_call(
        _proj_kernel,
        out_shape=jax.ShapeDtypeStruct((t, n), BF16),
        grid=(t // tm, n // tn),
        in_specs=[
            pl.BlockSpec((tm, d), lambda i, j: (i, 0)),
            pl.BlockSpec((1, d), lambda i, j: (0, 0)),
            pl.BlockSpec((None, d, tn), lambda i, j: (l, 0, j)),
        ],
        out_specs=pl.BlockSpec((tm, tn), lambda i, j: (i, j)),
        scratch_shapes=[pltpu.VMEM((tm, d), BF16)],
        compiler_params=_params(("parallel", "arbitrary")),
        name="proj",
    )(x, gain.reshape(1, d), w)


def _bias_kernel(tab_ref, o_ref):
    h = pl.program_id(0)
    far = tab_ref[h * N_BUCKETS + N_BUCKETS - 1]
    shape = (ATT_TILE, 2 * ATT_TILE)
    dist = ATT_TILE + lax.broadcasted_iota(jnp.int32, shape, 0) - lax.broadcasted_iota(jnp.int32, shape, 1)
    dist = jnp.maximum(dist, 0)
    max_exact = N_BUCKETS // 2
    log_ratio = jnp.log(jnp.maximum(dist, 1).astype(F32) / max_exact) / math.log(MAX_DISTANCE / max_exact)
    large = jnp.minimum(max_exact + (log_ratio * (N_BUCKETS - max_exact)).astype(jnp.int32), N_BUCKETS - 1)
    bucket = jnp.where(dist < max_exact, dist, large)
    out = jnp.zeros(shape, F32)
    for b in range(N_BUCKETS):
        out = jnp.where(bucket == b, (tab_ref[h * N_BUCKETS + b] - far) * LOG2E, out)
    o_ref[0] = out


def _bias_strips(tab_flat):
    return pl.pallas_call(
        _bias_kernel,
        out_shape=jax.ShapeDtypeStruct((N_BIAS_HEADS, ATT_TILE, 2 * ATT_TILE), F32),
        grid=(N_BIAS_HEADS,),
        in_specs=[pl.BlockSpec(memory_space=pltpu.SMEM)],
        out_specs=pl.BlockSpec((1, ATT_TILE, 2 * ATT_TILE), lambda h: (h, 0, 0)),
        compiler_params=_params(("arbitrary",)),
        name="bias_strips",
    )(tab_flat)


CONV_ROWS = 32
CONV_UNROLL = 8
SUBLANES = 8
assert CONV_PAD % SUBLANES == 0 and CONV_PAD >= SUBLANES * -(-CONV_WIDTH // SUBLANES)


def _conv_kernel(n_side, a_ref, g_ref, w_ref, cb_ref, lg_ref, lb_ref, *refs):
    side_in, o_ref, side_out, u_ref = refs[:n_side], refs[n_side], refs[n_side + 1:2 * n_side + 1], refs[-1]
    for wf_ref, wb_ref in zip(side_in, side_out):
        wb_ref[...] = wf_ref[...].astype(BF16)
    s = a_ref.shape[1]
    u_ref[0:CONV_PAD, :] = jnp.zeros((CONV_PAD, CONV_CH), F32)

    def glu(r, carry):
        rows = pl.ds(pl.multiple_of(r * CONV_ROWS, CONV_ROWS), CONV_ROWS)
        a = a_ref[0, rows, :].astype(F32)
        g = g_ref[0, rows, :].astype(F32)
        u_ref[pl.ds(pl.multiple_of(CONV_PAD + r * CONV_ROWS, CONV_ROWS), CONV_ROWS), :] = a / (1.0 + jnp.exp(-g))
        return carry

    lax.fori_loop(0, s // CONV_ROWS, glu, 0)

    win_rows = CONV_ROWS + CONV_PAD
    n_groups = CONV_CH // LANES

    def conv(r, carry):
        base = pl.multiple_of(r * CONV_ROWS, CONV_ROWS)
        accs = []
        for c in range(n_groups):
            lanes = slice(c * LANES, (c + 1) * LANES)
            win = u_ref[pl.ds(base, win_rows), lanes]
            acc = jnp.zeros((CONV_ROWS, LANES), F32) + cb_ref[:, lanes]
            for b in range(SUBLANES):
                shifted = pltpu.roll(win, b, 0) if b else win
                for a in range(-(-CONV_WIDTH // SUBLANES)):
                    d = SUBLANES * a + b
                    if d < CONV_WIDTH:
                        k0 = CONV_PAD - SUBLANES * a
                        tap = CONV_WIDTH - 1 - d
                        acc = acc + shifted[k0:k0 + CONV_ROWS, :] * w_ref[tap:tap + 1, lanes]
            accs.append(acc)
        mu = sum(jnp.sum(acc, axis=-1, keepdims=True) for acc in accs) / float(CONV_CH)
        cens = [acc - mu for acc in accs]
        var = sum(jnp.sum(cen * cen, axis=-1, keepdims=True) for cen in cens) / float(CONV_CH)
        inv = lax.rsqrt(var + 1e-5)
        for c, cen in enumerate(cens):
            lanes = slice(c * LANES, (c + 1) * LANES)
            y = cen * inv * lg_ref[:, lanes] + lb_ref[:, lanes]
            o_ref[0, pl.ds(base, CONV_ROWS), lanes] = (y / (1.0 + jnp.exp(-y))).astype(o_ref.dtype)
        return carry

    lax.fori_loop(0, s // CONV_ROWS, conv, 0, unroll=CONV_UNROLL)


def _conv(proj, w, cb, lg, lb, side=()):
    b, s, _ = proj.shape
    vec = lambda v: v.reshape(1, CONV_CH)
    side_in_specs, side_out_specs, side_shapes = [], [], []
    for wf, ls in side:
        k, n = wf.shape[1:]
        assert k % b == 0 and (k // b) % BF16_SUBLANES == 0, wf.shape
        side_in_specs.append(pl.BlockSpec((None, k // b, n), lambda i, ls=ls: (ls, i, 0)))
        side_out_specs.append(pl.BlockSpec((None, k // b, n), lambda i: (0, i, 0)))
        side_shapes.append(jax.ShapeDtypeStruct((1, k, n), BF16))
    out, *cast = pl.pallas_call(
        functools.partial(_conv_kernel, len(side)),
        out_shape=[jax.ShapeDtypeStruct((b, s, CONV_CH), BF16), *side_shapes],
        grid=(b,),
        in_specs=[
            pl.BlockSpec((1, s, CONV_CH), lambda i: (i, 0, COL_CONV_A * HEAD_DIM // CONV_CH)),
            pl.BlockSpec((1, s, CONV_CH), lambda i: (i, 0, COL_CONV_G * HEAD_DIM // CONV_CH)),
            pl.BlockSpec((CONV_WIDTH, CONV_CH), lambda i: (0, 0)),
            pl.BlockSpec((1, CONV_CH), lambda i: (0, 0)),
            pl.BlockSpec((1, CONV_CH), lambda i: (0, 0)),
            pl.BlockSpec((1, CONV_CH), lambda i: (0, 0)),
            *side_in_specs,
        ],
        out_specs=[pl.BlockSpec((1, s, CONV_CH), lambda i: (i, 0, 0)), *side_out_specs],
        scratch_shapes=[pltpu.VMEM((s + CONV_PAD, CONV_CH), F32)],
        compiler_params=_params(("parallel",)),
        name="conv",
    )(proj, proj, w, vec(cb), vec(lg), vec(lb), *(wf for wf, _ in side))
    return out, cast


def _tile_rows(j):
    return slice(j * ATT_TILE, (j + 1) * ATT_TILE)


def _lane_fold(op, x):
    out = x[:, :LANES]
    for c in range(1, x.shape[1] // LANES):
        out = op(out, x[:, c * LANES:(c + 1) * LANES])
    return out


def _half_rms(x, gain, eps=1e-6):
    half = x.shape[-1] // 2
    low = lax.broadcasted_iota(jnp.int32, x.shape, x.ndim - 1) < half
    sq = x * x
    ms = jnp.where(low, jnp.sum(jnp.where(low, sq, 0.0), axis=-1, keepdims=True),
                   jnp.sum(jnp.where(low, 0.0, sq), axis=-1, keepdims=True)) / half
    return x * lax.rsqrt(ms + eps) * gain


def _logits_pass(i, q, k_ref, strip_ref, sel_add, s_ref):
    maps = q.shape[0] // ATT_TILE
    shape = (q.shape[0], ATT_TILE)
    mask = lax.broadcasted_iota(jnp.int32, shape, 0) % ATT_TILE >= lax.broadcasted_iota(jnp.int32, shape, 1)
    rep = lambda x: x if maps == 1 else jnp.concatenate([x] * maps, axis=0)
    m_acc = None
    for j in range(i + 1):
        s = _dot_nt(q, k_ref[_tile_rows(j), :])
        if j == i:
            s = jnp.where(mask, s + rep(strip_ref[:, ATT_TILE:]), NEG_INF)
        elif j == i - 1:
            s = s + rep(strip_ref[:, :ATT_TILE])
        if sel_add is not None and j < i:
            s = s + sel_add[:, j:j + 1]
        s_ref[:, _tile_rows(j)] = s
        t = _lane_fold(jnp.maximum, s)
        m_acc = t if m_acc is None else jnp.maximum(m_acc, t)
    return jnp.max(m_acc, axis=-1, keepdims=True)


def _softmax_pass(i, m, vo_ref, s_ref, p_ref):
    for j in range(i + 1):
        p_ref[:, _tile_rows(j)] = jnp.exp2(s_ref[:, _tile_rows(j)] - m).astype(BF16)
    n = (i + 1) * ATT_TILE
    out = _dot(p_ref[:, :n], vo_ref[:n, :])
    return out[:, :HEAD_DIM], out[:, HEAD_DIM:]


ATT_HEADS_PER_STEP = 2


def _head_lanes(g):
    return slice(g * HEAD_DIM, (g + 1) * HEAD_DIM)


def _moba_kernel(q_ref, k_ref, v_ref, qg_ref, kg_ref, strip_ref, o_ref, kn_ref, vo_ref, kmean_ref, s_ref, p_ref):
    nb = k_ref.shape[1] // MOBA_BLOCK
    heads = k_ref.shape[2] // HEAD_DIM
    for g in range(heads):
        vo_ref[g, :, :HEAD_DIM] = v_ref[0, :, _head_lanes(g)]
        vo_ref[g, :, HEAD_DIM:] = jnp.ones((v_ref.shape[1], HEAD_DIM), BF16)
        for j in range(nb):
            rows = _tile_rows(j)
            kn = _rms(k_ref[0, rows, _head_lanes(g)].astype(F32), kg_ref[...])
            kn_ref[g, rows, :] = kn.astype(BF16)
            kmean_ref[g, j:j + 1, :] = jnp.sum(kn, axis=0, keepdims=True) / float(MOBA_BLOCK)

    eye = (lax.broadcasted_iota(jnp.int32, (MOBA_BLOCK, MOBA_BLOCK), 0)
           == lax.broadcasted_iota(jnp.int32, (MOBA_BLOCK, MOBA_BLOCK), 1)).astype(F32).astype(BF16)
    n_sel = min(MOBA_TOPK, nb)

    def logits(pos, g, i):
        qn = _rms(q_ref[0, _tile_rows(i), _head_lanes(g)].astype(F32), qg_ref[...])
        sel_add = None
        if i > n_sel:
            gate = _dot_nt(kmean_ref[g], qn, precision=lax.Precision.HIGHEST)
            blk = lax.broadcasted_iota(jnp.int32, gate.shape, 0)
            gate = jnp.where(blk < i, gate, TAKEN)
            sel = jnp.zeros(gate.shape, F32)
            for _ in range(n_sel):
                top = jnp.max(gate, axis=0, keepdims=True)
                first = jnp.min(jnp.where(gate == top, blk, nb), axis=0, keepdims=True)
                pick = blk == first
                sel = jnp.where(pick, 1.0, sel)
                gate = jnp.where(pick, TAKEN, gate)
            sel_t = _dot_nt(eye, sel.astype(BF16))
            sel_add = (sel_t - 1.0) * -NEG_INF
        q = (qn * (HEAD_DIM ** -0.5 * LOG2E)).astype(BF16)
        return _logits_pass(i, q, kn_ref.at[g], strip_ref.at[g], sel_add, s_ref.at[pos % 2])

    jobs = [(g, i) for g in range(heads) for i in range(nb)]
    m_next = logits(0, *jobs[0])
    for pos, (g, i) in enumerate(jobs):
        m = m_next
        if pos + 1 < len(jobs):
            m_next = logits(pos + 1, *jobs[pos + 1])
        acc, l = _softmax_pass(i, m, vo_ref.at[g], s_ref.at[pos % 2], p_ref.at[pos % 2])
        o_ref[0, _tile_rows(i), _head_lanes(g)] = (acc / l).astype(o_ref.dtype)


def _moba(proj, strips, q_gain, k_gain):
    b, s, _ = proj.shape
    nb = s // MOBA_BLOCK
    g = ATT_HEADS_PER_STEP
    assert MOBA_HEADS % g == 0 and COL_MOBA_Q % g == 0 and COL_MOBA_K % g == 0 and COL_MOBA_V % g == 0
    cols = lambda first: pl.BlockSpec((1, s, g * HEAD_DIM), lambda bi, h: (bi, 0, first // g + h))
    return pl.pallas_call(
        _moba_kernel,
        out_shape=jax.ShapeDtypeStruct((b, s, MOBA_HEADS * HEAD_DIM), BF16),
        grid=(b, MOBA_HEADS // g),
        in_specs=[
            cols(COL_MOBA_Q),
            cols(COL_MOBA_K),
            cols(COL_MOBA_V),
            pl.BlockSpec((1, HEAD_DIM), lambda bi, h: (0, 0)),
            pl.BlockSpec((1, HEAD_DIM), lambda bi, h: (0, 0)),
            pl.BlockSpec((g, ATT_TILE, 2 * ATT_TILE), lambda bi, h: (h, 0, 0)),
        ],
        out_specs=cols(0),
        scratch_shapes=[
            pltpu.VMEM((g, s, HEAD_DIM), BF16),
            pltpu.VMEM((g, s, 2 * HEAD_DIM), BF16),
            pltpu.VMEM((g, nb, HEAD_DIM), F32),
            pltpu.VMEM((2, ATT_TILE, s), F32),
            pltpu.VMEM((2, ATT_TILE, s), BF16),
        ],
        compiler_params=_params(("parallel", "parallel")),
        name="moba",
    )(proj, proj, proj, q_gain.reshape(1, HEAD_DIM), k_gain.reshape(1, HEAD_DIM), strips)


DIFF_MAPS = 2


def _diff_kernel(lam_init, q_ref, k_ref, v_ref, qg_ref, kg_ref, lam_ref, sg_ref, strip_ref, o_ref,
                 kn_ref, vo_ref, s_ref, p_ref):
    nt = k_ref.shape[1] // ATT_TILE
    heads = k_ref.shape[2] // HEAD_DIM
    assert DIFF_MAPS * DIFF_QK_DIM == HEAD_DIM
    for g in range(heads):
        vo_ref[g, :, :HEAD_DIM] = v_ref[0, :, _head_lanes(g)]
        vo_ref[g, :, HEAD_DIM:] = jnp.ones((v_ref.shape[1], HEAD_DIM), BF16)
        for j in range(nt):
            rows = _tile_rows(j)
            kn_ref[g, rows, :] = _half_rms(k_ref[0, rows, _head_lanes(g)].astype(F32), kg_ref[...]).astype(BF16)

    lp = lam_ref[...]
    lam = (jnp.exp(jnp.sum(lp[0:1] * lp[1:2], axis=-1, keepdims=True))
           - jnp.exp(jnp.sum(lp[2:3] * lp[3:4], axis=-1, keepdims=True)) + lam_init)
    scale = DIFF_QK_DIM ** -0.5 * LOG2E
    lane_map = lax.broadcasted_iota(jnp.int32, (ATT_TILE, HEAD_DIM), 1) // DIFF_QK_DIM

    def logits(pos, g, i):
        qn = _half_rms(q_ref[0, _tile_rows(i), _head_lanes(g)].astype(F32), qg_ref[...]) * scale
        q = jnp.concatenate([jnp.where(lane_map == c, qn, 0.0) for c in range(DIFF_MAPS)], axis=0).astype(BF16)
        return _logits_pass(i, q, kn_ref.at[g], strip_ref.at[g], None, s_ref.at[pos % 2])

    jobs = [(g, i) for g in range(heads) for i in range(nt)]
    m_next = logits(0, *jobs[0])
    for pos, (g, i) in enumerate(jobs):
        m = m_next
        if pos + 1 < len(jobs):
            m_next = logits(pos + 1, *jobs[pos + 1])
        acc, l = _softmax_pass(i, m, vo_ref.at[g], s_ref.at[pos % 2], p_ref.at[pos % 2])
        a = acc / l
        o = a[:ATT_TILE] - lam * a[ATT_TILE:]
        o_ref[0, _tile_rows(i), _head_lanes(g)] = (_rms(o, sg_ref[...]) * (1.0 - lam_init)).astype(o_ref.dtype)


def _diff(proj, strips, q_gain, k_gain, lam_params, subln, lam_init):
    b, s, _ = proj.shape
    g = ATT_HEADS_PER_STEP
    assert DIFF_HEADS % g == 0 and MOBA_HEADS % g == 0 and COL_DIFF_Q % g == 0 and COL_DIFF_K % g == 0
    assert COL_DIFF_V % g == 0
    both = lambda gain: jnp.tile(gain, DIFF_MAPS).reshape(1, HEAD_DIM)
    cols = lambda first: pl.BlockSpec((1, s, g * HEAD_DIM), lambda bi, h: (bi, 0, first // g + h))
    return pl.pallas_call(
        functools.partial(_diff_kernel, lam_init),
        out_shape=jax.ShapeDtypeStruct((b, s, DIFF_HEADS * HEAD_DIM), BF16),
        grid=(b, DIFF_HEADS // g),
        in_specs=[
            cols(COL_DIFF_Q),
            cols(COL_DIFF_K),
            cols(COL_DIFF_V),
            pl.BlockSpec((1, HEAD_DIM), lambda bi, h: (0, 0)),
            pl.BlockSpec((1, HEAD_DIM), lambda bi, h: (0, 0)),
            pl.BlockSpec((4, DIFF_QK_DIM), lambda bi, h: (0, 0)),
            pl.BlockSpec((1, HEAD_DIM), lambda bi, h: (0, 0)),
            pl.BlockSpec((g, ATT_TILE, 2 * ATT_TILE), lambda bi, h: (MOBA_HEADS // g + h, 0, 0)),
        ],
        out_specs=cols(0),
        scratch_shapes=[
            pltpu.VMEM((g, s, HEAD_DIM), BF16),
            pltpu.VMEM((g, s, 2 * HEAD_DIM), BF16),
            pltpu.VMEM((2, DIFF_MAPS * ATT_TILE, s), F32),
            pltpu.VMEM((2, DIFF_MAPS * ATT_TILE, s), BF16),
        ],
        compiler_params=_params(("parallel", "parallel")),
        name="diff",
    )(proj, proj, proj, both(q_gain), both(k_gain), lam_params, subln.reshape(1, HEAD_DIM), strips)


def _memkv_kernel(m_ref, g_ref, w_ref, kg_ref, k_ref, v_ref):
    hm = _rms(m_ref[0], g_ref[...]).astype(BF16)
    kv = _dot(hm, w_ref[...])
    width = MEM_HEADS * HEAD_DIM
    for h in range(MEM_HEADS):
        cols = slice(h * HEAD_DIM, (h + 1) * HEAD_DIM)
        k_ref[0, :, cols] = _rms(kv[:, cols], kg_ref[...]).astype(BF16)
    v_ref[0] = kv[:, width:].astype(BF16)


def _memkv(mem, gain, wkv, k_gain, l):
    b, m, d = mem.shape
    width = MEM_HEADS * HEAD_DIM
    out = jax.ShapeDtypeStruct((b, m, width), BF16)
    return pl.pallas_call(
        _memkv_kernel,
        out_shape=(out, out),
        grid=(b,),
        in_specs=[
            pl.BlockSpec((1, m, d), lambda i: (i, 0, 0)),
            pl.BlockSpec((1, d), lambda i: (0, 0)),
            pl.BlockSpec((None, d, 2 * width), lambda i: (l, 0, 0)),
            pl.BlockSpec((1, HEAD_DIM), lambda i: (0, 0)),
        ],
        out_specs=(pl.BlockSpec((1, m, width), lambda i: (i, 0, 0)), pl.BlockSpec((1, m, width), lambda i: (i, 0, 0))),
        compiler_params=_params(("parallel",)),
        name="memkv",
    )(mem, gain.reshape(1, d), wkv, k_gain.reshape(1, HEAD_DIM))


def _mixout_kernel(x_ref, yc_ref, ym_ref, yd_ref, w_ref, g_ref, wq_ref, qg_ref, k_ref, v_ref, wo_ref, o_ref, ob_ref):
    c0 = CONV_CH
    c1 = c0 + MOBA_HEADS * HEAD_DIM
    x1 = (x_ref[0] + _dot(yc_ref[0], w_ref[0:c0, :]) + _dot(ym_ref[0], w_ref[c0:c1, :])
          + _dot(yd_ref[0], w_ref[c1:, :]))
    q = _dot(_rms(x1, g_ref[...]).astype(BF16), wq_ref[...])
    for h in range(MEM_HEADS):
        cols = slice(h * HEAD_DIM, (h + 1) * HEAD_DIM)
        qh = (_rms(q[:, cols], qg_ref[...]) * (HEAD_DIM ** -0.5 * LOG2E)).astype(BF16)
        s = _dot_nt(qh, k_ref[0, :, cols])
        p = jnp.exp2(s - jnp.max(s, axis=-1, keepdims=True))
        oh = _dot(p.astype(BF16), v_ref[0, :, cols]) / jnp.sum(p, axis=-1, keepdims=True)
        ob_ref[:, cols] = oh.astype(BF16)
    o_ref[0] = x1 + _dot(ob_ref[...], wo_ref[...])


def _mixout(x, yc, ym, yd, w, gain, wq, q_gain, kn, v, wo, l, tm=512):
    b, s, d = x.shape
    m = kn.shape[1]
    width = MEM_HEADS * HEAD_DIM
    once = pl.Buffered(1)
    rows = lambda a: pl.BlockSpec((1, tm, a.shape[2]), lambda bi, i: (bi, i, 0))
    return pl.pallas_call(
        _mixout_kernel,
        out_shape=jax.ShapeDtypeStruct((b, s, d), F32),
        grid=(b, s // tm),
        in_specs=[
            rows(x), rows(yc), rows(ym), rows(yd),
            pl.BlockSpec((None, w.shape[1], d), lambda bi, i: (l, 0, 0), pipeline_mode=once),
            pl.BlockSpec((1, d), lambda bi, i: (0, 0)),
            pl.BlockSpec((None, d, width), lambda bi, i: (l, 0, 0), pipeline_mode=once),
            pl.BlockSpec((1, HEAD_DIM), lambda bi, i: (0, 0)),
            pl.BlockSpec((1, m, width), lambda bi, i: (bi, 0, 0)),
            pl.BlockSpec((1, m, width), lambda bi, i: (bi, 0, 0)),
            pl.BlockSpec((None, width, d), lambda bi, i: (l, 0, 0), pipeline_mode=once),
        ],
        out_specs=rows(x),
        scratch_shapes=[pltpu.VMEM((tm, width), BF16)],
        compiler_params=_params(("parallel", "parallel")),
        name="mixout",
    )(x, yc, ym, yd, w, gain.reshape(1, d), wq, q_gain.reshape(1, HEAD_DIM), kn, v, wo)


def kernel(x, mem, rel_bias, ffn1_norm, ffn1_w_gate, ffn1_w_up, ffn1_w_down, mix_norm, w_in, conv_w, conv_b, conv_ln_g, conv_ln_b, moba_q_norm, moba_k_norm, diff_q_norm, diff_k_norm, diff_lambda, diff_subln, w_out, mem_norm_x, mem_norm_m, mem_wq, mem_wkv, mem_q_norm, mem_k_norm, mem_wo, ffn2_norm, ffn2_w_gate, ffn2_w_up, ffn2_w_down):
    b, s, d = x.shape
    depth = w_in.shape[0]
    t = b * s
    tab_flat = rel_bias.T.reshape(-1)
    strips = _bias_strips(tab_flat)
    ffn1_f32 = (ffn1_w_gate, ffn1_w_up, ffn1_w_down)
    ffn2_f32 = (ffn2_w_gate, ffn2_w_up, ffn2_w_down)
    ffn_w, ffn_tf = ffn1_f32, FFN_BLOCK_F32
    xt = x.reshape(t, d)
    for l in range(depth):
        xt, (*ffn_w, w_in_l) = _ffn(xt, ffn1_norm[l], *ffn_w, 0, side=[(w, l) for w in (*ffn2_f32, w_in)], tf=ffn_tf)
        ffn_tf = FFN_BLOCK_BF16
        proj = _proj(xt, mix_norm[l], w_in_l, 0).reshape(b, s, PROJ_WIDTH)
        y_conv, (w_out_l, wq_l, wkv_l, wo_l) = _conv(proj, conv_w[l], conv_b[l], conv_ln_g[l], conv_ln_b[l],
                                                    side=[(w, l) for w in (w_out, mem_wq, mem_wkv, mem_wo)])
        y_moba = _moba(proj, strips, moba_q_norm[l], moba_k_norm[l])
        lam_init = 0.8 - 0.6 * math.exp(-0.3 * l)
        y_diff = _diff(proj, strips, diff_q_norm[l], diff_k_norm[l], diff_lambda[l], diff_subln[l], lam_init)
        kn, v = _memkv(mem, mem_norm_m[l], wkv_l, mem_k_norm[l], 0)
        xt = _mixout(xt.reshape(b, s, d), y_conv, y_moba, y_diff, w_out_l, mem_norm_x[l], wq_l, mem_q_norm[l], kn, v,
                     wo_l, 0).reshape(t, d)
        nxt = [(w, l + 1) for w in ffn1_f32] if l + 1 < depth else []
        xt, ffn_w = _ffn(xt, ffn2_norm[l], *ffn_w, 0, side=nxt, tf=ffn_tf)
    return xt.reshape(b, s, d)
```
